```python
import jax, jax.numpy as jnp
from jax import lax
import numpy as np

D_MODEL = 2048
BATCH = 4
SEQ = 8192
DEPTH = 1

D_MIX = D_MODEL
HEAD_DIM = 64
ATTN_WIDTH = D_MIX // 2
SGU_WIDTH = D_MIX - ATTN_WIDTH
N_ATTN_HEADS = ATTN_WIDTH // HEAD_DIM
N_SGU_GROUPS = 16
SGU_GROUP_DIM = SGU_WIDTH // N_SGU_GROUPS
D_IN_PROJ = 3 * ATTN_WIDTH + 2 * SGU_WIDTH

GRID_W = 64
WIN_ROWS_MAX = 8
WIN_COLS = 16
RPB_ROWS = 2 * WIN_ROWS_MAX - 1
RPB_COLS = 2 * WIN_COLS - 1

CHUNK = 128

N_EXPERTS = 32
TOP_K = 4
D_EXPERT = D_MODEL
SWIGLU_LIMIT = 7.0
SWIGLU_ALPHA = 1.702

EPS = 1e-6

kernel_name = "hymba_natten_sgu_moe_block"


def rms_norm(x, g):
    xf = x.astype(jnp.float32)
    y = xf * lax.rsqrt(jnp.mean(xf * xf, axis=-1, keepdims=True) + EPS)
    return (y * g.astype(jnp.float32)).astype(x.dtype)


def neighbourhood_attention(q, k, v, q_norm_g, k_norm_g, rpb):
    B, S, H, dh = q.shape
    rows = S // GRID_W
    kh = min(WIN_ROWS_MAX, rows)
    n_keys = kh * WIN_COLS
    q = rms_norm(q, q_norm_g)
    k = rms_norm(k, k_norm_g)
    scale = dh ** -0.5

    cols = np.arange(GRID_W, dtype=np.int32)
    col_start = np.clip(cols - WIN_COLS // 2, 0, GRID_W - WIN_COLS).astype(np.int32)
    col_idx = (col_start[:, None] + np.arange(WIN_COLS, dtype=np.int32)[None, :]).astype(np.int32)
    col_off = (col_idx - cols[:, None] + WIN_COLS - 1).astype(np.int32)
    rpb_cols = rpb.astype(jnp.float32)[:, :, col_off]

    def one_row(r):
        row_start = jnp.clip(r - kh // 2, 0, rows - kh)
        key_rows = row_start + jnp.arange(kh, dtype=jnp.int32)
        key_pos = key_rows[None, :, None] * GRID_W + col_idx[:, None, :]
        key_pos = key_pos.reshape(-1)
        q_blk = lax.dynamic_slice_in_dim(q, r * GRID_W, GRID_W, axis=1)
        k_blk = jnp.take(k, key_pos, axis=1).reshape(B, GRID_W, n_keys, H, dh)
        v_blk = jnp.take(v, key_pos, axis=1).reshape(B, GRID_W, n_keys, H, dh)
        row_off = key_rows - r + WIN_ROWS_MAX - 1
        bias = rpb_cols[:, row_off]
        bias = bias.transpose(0, 2, 1, 3).reshape(H, GRID_W, n_keys)
        s = jnp.einsum('bqhd,bqkhd->bhqk', q_blk, k_blk,
                       preferred_element_type=jnp.float32) * scale + bias[None]
        p = jax.nn.softmax(s, axis=-1).astype(v.dtype)
        return jnp.einsum('bhqk,bqkhd->bqhd', p, v_blk)

    out = lax.map(one_row, jnp.arange(rows, dtype=jnp.int32))
    return out.transpose(1, 0, 2, 3, 4).reshape(B, S, H * dh)


def spatial_gating(u, g, sgu_norm_g, w_spatial, b_spatial):
    B, S, _ = u.shape
    n_chunks = S // CHUNK
    u = jax.nn.gelu(u, approximate=False)
    g = rms_norm(jax.nn.gelu(g, approximate=False), sgu_norm_g)
    g = g.reshape(B, n_chunks, CHUNK, N_SGU_GROUPS, SGU_GROUP_DIM)
    mixed = jnp.einsum('gpq,bcqgd->bcpgd', w_spatial, g)
    mixed = mixed + b_spatial.T[None, None, :, :, None]
    return u * mixed.reshape(B, S, SGU_WIDTH)


def moe_ffn(x, w_router, b_router, w_gate_up, b_gate_up, w_down, b_down):
    B, S, D = x.shape
    t = x.reshape(B * S, D)
    logits = jnp.einsum('td,de->te', t, w_router,
                        preferred_element_type=jnp.float32) + b_router.astype(jnp.float32)
    top_vals, top_idx = lax.top_k(logits, TOP_K)
    gates = jax.nn.softmax(top_vals, axis=-1)
    combine = jnp.sum(jax.nn.one_hot(top_idx, N_EXPERTS, dtype=jnp.float32) * gates[..., None], axis=1)
    y = jnp.zeros((B * S, D), jnp.float32)
    for e in range(N_EXPERTS):
        h = t @ w_gate_up[e] + b_gate_up[e]
        gate = jnp.minimum(h[:, :D_EXPERT], SWIGLU_LIMIT)
        lin = jnp.clip(h[:, D_EXPERT:], -SWIGLU_LIMIT, SWIGLU_LIMIT)
        act = gate * jax.nn.sigmoid(SWIGLU_ALPHA * gate) * (lin + 1)
        out = act @ w_down[e] + b_down[e]
        y = y + combine[:, e:e + 1] * out.astype(jnp.float32)
    return y.astype(x.dtype).reshape(B, S, D)


def setup_inputs(seed: int = 0) -> dict:
    key = jax.random.key(seed)
    ks = jax.random.split(key, 20)
    f32 = jnp.float32
    L = DEPTH

    def nrm(k, shape, scale):
        return jax.random.normal(k, shape, f32) * scale

    return {
        "x": jax.random.normal(ks[0], (BATCH, SEQ, D_MODEL), f32),
        "norm1_g": 1.0 + nrm(ks[1], (L, D_MODEL), 0.02),
        "w_in": nrm(ks[2], (L, D_MODEL, D_IN_PROJ), D_MODEL ** -0.5),
        "q_norm_g": 1.0 + nrm(ks[3], (L, HEAD_DIM), 0.02),
        "k_norm_g": 1.0 + nrm(ks[4], (L, HEAD_DIM), 0.02),
        "rpb": nrm(ks[5], (L, N_ATTN_HEADS, RPB_ROWS, RPB_COLS), 0.1),
        "sgu_norm_g": 1.0 + nrm(ks[6], (L, SGU_WIDTH), 0.02),
        "w_spatial": nrm(ks[7], (L, N_SGU_GROUPS, CHUNK, CHUNK), CHUNK ** -0.5),
        "b_spatial": 1.0 + nrm(ks[8], (L, N_SGU_GROUPS, CHUNK), 0.1),
        "attn_out_g": 1.0 + nrm(ks[9], (L, ATTN_WIDTH), 0.02),
        "sgu_out_g": 1.0 + nrm(ks[10], (L, SGU_WIDTH), 0.02),
        "w_out": nrm(ks[11], (L, D_MIX, D_MODEL), D_MIX ** -0.5),
        "norm2_g": 1.0 + nrm(ks[12], (L, D_MODEL), 0.02),
        "w_router": nrm(ks[13], (L, D_MODEL, N_EXPERTS), D_MODEL ** -0.5),
        "b_router": nrm(ks[14], (L, N_EXPERTS), 0.01),
        "w_gate_up": nrm(ks[15], (L, N_EXPERTS, D_MODEL, 2 * D_EXPERT), D_MODEL ** -0.5),
        "b_gate_up": nrm(ks[16], (L, N_EXPERTS, 2 * D_EXPERT), 0.02),
        "w_down": nrm(ks[17], (L, N_EXPERTS, D_EXPERT, D_MODEL), D_EXPERT ** -0.5),
        "b_down": nrm(ks[18], (L, N_EXPERTS, D_MODEL), 0.02),
    }


def reference(x, norm1_g, w_in, q_norm_g, k_norm_g, rpb, sgu_norm_g, w_spatial, b_spatial,
              attn_out_g, sgu_out_g, w_out, norm2_g, w_router, b_router,
              w_gate_up, b_gate_up, w_down, b_down):
    B, S, _ = x.shape
    for l in range(DEPTH):
        h = rms_norm(x, norm1_g[l])
        proj = h @ w_in[l]
        q, k, v, u, g = jnp.split(
            proj, [ATTN_WIDTH, 2 * ATTN_WIDTH, 3 * ATTN_WIDTH, 3 * ATTN_WIDTH + SGU_WIDTH], axis=-1)
        q = q.reshape(B, S, N_ATTN_HEADS, HEAD_DIM)
        k = k.reshape(B, S, N_ATTN_HEADS, HEAD_DIM)
        v = v.reshape(B, S, N_ATTN_HEADS, HEAD_DIM)
        a_out = neighbourhood_attention(q, k, v, q_norm_g[l], k_norm_g[l], rpb[l])
        s_out = spatial_gating(u, g, sgu_norm_g[l], w_spatial[l], b_spatial[l])
        mix = jnp.concatenate([rms_norm(a_out, attn_out_g[l]),
                               rms_norm(s_out, sgu_out_g[l])], axis=-1)
        x = x + mix @ w_out[l]
        x = x + moe_ffn(rms_norm(x, norm2_g[l]), w_router[l], b_router[l],
                        w_gate_up[l], b_gate_up[l], w_down[l], b_down[l])
    return x
```

```python
import functools

import jax
import jax.numpy as jnp
from jax import lax
from jax.experimental import pallas as pl
from jax.experimental.pallas import tpu as pltpu

GRID_W = 64
TOP_K = 4
SWIGLU_LIMIT = 7.0
SWIGLU_ALPHA = 1.702
EPS = 1e-6
LANES = 128
NEG_BIG = -1e30
VMEM_LIMIT_BYTES = 56 * 1024 * 1024

_F32 = jnp.float32
_BF16 = jnp.bfloat16


def _cparams(n_axes):
    return pltpu.CompilerParams(
        dimension_semantics=("arbitrary",) * n_axes,
        vmem_limit_bytes=VMEM_LIMIT_BYTES)


def _rms(x):
    return x * lax.rsqrt(jnp.mean(x * x, axis=-1, keepdims=True) + EPS)


def _gelu(x):
    return 0.5 * x * (1.0 + lax.erf(x * 0.7071067811865476))


def _norm_proj_kernel(x_ref, g_ref, w_ref, hg_ref, o_ref, h_ref, *, slab_out, n_norm_tiles, head_dim):
    j = pl.program_id(1)

    @pl.when(j == 0)
    def _():
        h_ref[...] = (_rms(x_ref[...]) * g_ref[...]).astype(_BF16)

    res = jnp.dot(h_ref[...], w_ref[...], preferred_element_type=_F32)
    if not slab_out:
        o_ref[...] = res
        return

    n_slabs = res.shape[1] // LANES
    lane = lax.broadcasted_iota(jnp.int32, (1, LANES), 1)
    left = lane < head_dim
    for s in range(n_slabs):
        r = res[:, s * LANES:(s + 1) * LANES]

        @pl.when(j < n_norm_tiles)
        def _():
            ss = r * r
            sl = jnp.sum(jnp.where(left, ss, 0.0), axis=-1, keepdims=True)
            sr = jnp.sum(jnp.where(left, 0.0, ss), axis=-1, keepdims=True)
            inv = jnp.where(left, lax.rsqrt(sl / head_dim + EPS), lax.rsqrt(sr / head_dim + EPS))
            o_ref[s] = r * inv * hg_ref[0]

        @pl.when(j >= n_norm_tiles)
        def _():
            o_ref[s] = r


def _norm_proj(x, g, w_bf16, head_gain, *, slab_out, attn_width, head_dim):
    t, d = x.shape
    n = w_bf16.shape[1]
    tm = min(1024, t)
    tn = min(1024, attn_width if slab_out else n)
    assert t % tm == 0 and n % tn == 0 and tn % LANES == 0
    if slab_out:
        assert attn_width % tn == 0 and 2 * head_dim == LANES
        tiles_per_part = attn_width // tn
        out_shape = jax.ShapeDtypeStruct((n // LANES, t, LANES), _F32)
        out_spec = pl.BlockSpec((tn // LANES, tm, LANES), lambda i, j: (j, i, 0))
        hg_spec = pl.BlockSpec((1, 1, LANES), lambda i, j: (jnp.minimum(j // tiles_per_part, 2), 0, 0))
        n_norm_tiles = 2 * tiles_per_part
    else:
        out_shape = jax.ShapeDtypeStruct((t, n), _F32)
        out_spec = pl.BlockSpec((tm, tn), lambda i, j: (i, j))
        hg_spec = pl.BlockSpec((1, 1, LANES), lambda i, j: (0, 0, 0))
        n_norm_tiles = 0
    kern = functools.partial(_norm_proj_kernel, slab_out=slab_out, n_norm_tiles=n_norm_tiles, head_dim=head_dim)
    return pl.pallas_call(
        kern,
        grid=(t // tm, n // tn),
        in_specs=[
            pl.BlockSpec((tm, d), lambda i, j: (i, 0)),
            pl.BlockSpec((1, d), lambda i, j: (0, 0)),
            pl.BlockSpec((d, tn), lambda i, j: (0, j)),
            hg_spec,
        ],
        out_specs=out_spec,
        out_shape=out_shape,
        scratch_shapes=[pltpu.VMEM((tm, d), _BF16)],
        compiler_params=_cparams(2),
        name="norm_proj_qkv" if slab_out else "norm_proj_ug",
    )(x, g, w_bf16, head_gain)


def _attn_kernel(q_ref, kp_ref, kc_ref, kn_ref, vp_ref, vc_ref, vn_ref, bias_ref, o_ref, kwin, vwin,
                 *, rows, rblk, kh, head_dim, scale):
    rb = pl.program_id(2)
    blk = rblk * GRID_W
    for w, (kr, vr) in enumerate(((kp_ref, vp_ref), (kc_ref, vc_ref), (kn_ref, vn_ref))):
        kwin[w * blk:(w + 1) * blk, :] = kr[...].astype(_BF16)
        vwin[w * blk:(w + 1) * blk, :] = vr[...].astype(_BF16)
    lane = lax.broadcasted_iota(jnp.int32, (1, LANES), 1)
    left = lane < head_dim
    nk = kh * GRID_W
    for i in range(rblk):
        r = rb * rblk + i
        row_start = jnp.clip(r - kh // 2, 0, rows - kh)
        off = pl.multiple_of((row_start - (rb * rblk - rblk)) * GRID_W, GRID_W)
        sh = r - row_start
        ks = kwin[pl.ds(off, nk), :]
        vs = vwin[pl.ds(off, nk), :]
        qi = q_ref[i * GRID_W:(i + 1) * GRID_W, :] * scale
        halves = []
        for half in range(2):
            keep = left if half == 0 else jnp.logical_not(left)
            qm = jnp.where(keep, qi, 0.0).astype(_BF16)
            s = lax.dot_general(qm, ks, (((1,), (1,)), ((), ())), preferred_element_type=_F32)
            s = s + bias_ref[sh, half]
            p = jnp.exp(s - jnp.max(s, axis=-1, keepdims=True))
            denom = jnp.sum(p, axis=-1, keepdims=True)
            pv = jnp.dot(p.astype(_BF16), vs, preferred_element_type=_F32)
            halves.append(pv / denom)
        o_ref[i * GRID_W:(i + 1) * GRID_W, :] = jnp.where(left, halves[0], halves[1])


def _attention(qkv, bias_tab, *, batch, seq, n_heads, head_dim, kh):
    n_hp = n_heads // 2
    rows = seq // GRID_W
    rblk = kh
    assert rows % rblk == 0 and rows >= kh and rblk >= 4
    nrb = rows // rblk
    blk = rblk * GRID_W
    t = batch * seq

    def qmap(b, hp, rb):
        return (hp, b * nrb + rb, 0)

    def kvmap(part, delta):
        def f(b, hp, rb):
            return (part * n_hp + hp, b * nrb + jnp.clip(rb + delta, 0, nrb - 1), 0)
        return f

    slab = lambda m: pl.BlockSpec((None, blk, LANES), m)
    kern = functools.partial(_attn_kernel, rows=rows, rblk=rblk, kh=kh, head_dim=head_dim,
                             scale=float(head_dim) ** -0.5)
    return pl.pallas_call(
        kern,
        grid=(batch, n_hp, nrb),
        in_specs=[
            slab(qmap),
            slab(kvmap(1, -1)), slab(kvmap(1, 0)), slab(kvmap(1, 1)),
            slab(kvmap(2, -1)), slab(kvmap(2, 0)), slab(kvmap(2, 1)),
            pl.BlockSpec((kh, 2, GRID_W, kh * GRID_W), lambda b, hp, rb: (0, hp, 0, 0)),
        ],
        out_specs=pl.BlockSpec((blk, LANES), lambda b, hp, rb: (b * nrb + rb, hp)),
        out_shape=jax.ShapeDtypeStruct((t, n_hp * LANES), _F32),
        scratch_shapes=[pltpu.VMEM((3 * blk, LANES), _BF16), pltpu.VMEM((3 * blk, LANES), _BF16)],
        compiler_params=_cparams(3),
        name="nbr_attention",
    )(qkv, qkv, qkv, qkv, qkv, qkv, qkv, bias_tab)


def _attention_bias_table(rpb, *, kh, win_cols):
    n_rpb_rows = rpb.shape[1]
    win_rows_max = (n_rpb_rows + 1) // 2
    sh = jnp.arange(kh)[:, None]
    j = jnp.arange(kh)[None, :]
    row_off = j - sh + win_rows_max - 1
    qc = jnp.arange(GRID_W)[:, None]
    kc = jnp.arange(GRID_W)[None, :]
    col_start = jnp.clip(qc - win_cols // 2, 0, GRID_W - win_cols)
    valid = (kc >= col_start) & (kc < col_start + win_cols)
    col_off = jnp.clip(kc - qc + win_cols - 1, 0, rpb.shape[2] - 1)
    tab = rpb.astype(_F32)[:, row_off[:, :, None, None], col_off[None, None, :, :]]
    tab = jnp.where(valid[None, None, None], tab, NEG_BIG)
    tab = tab.transpose(1, 0, 3, 2, 4)
    return tab.reshape(kh, rpb.shape[0], GRID_W, kh * GRID_W)


def _sgu_kernel(u_ref, g_ref, ng_ref, w_ref, b_ref, og_ref, o_ref, *, chunk, group_dim):
    tm, width = u_ref.shape
    gg = _gelu(g_ref[...])
    gn = (_rms(gg) * ng_ref[...]).astype(_BF16)
    lane = lax.broadcasted_iota(jnp.int32, (1, LANES), 1)
    left = lane < group_dim
    for c in range(tm // chunk):
        rs = slice(c * chunk, (c + 1) * chunk)
        for gp in range(width // LANES):
            cs = slice(gp * LANES, (gp + 1) * LANES)
            xg = gn[rs, cs]
            ma = jnp.dot(w_ref[2 * gp], xg, preferred_element_type=_F32)
            mb = jnp.dot(w_ref[2 * gp + 1], xg, preferred_element_type=_F32)
            mixed = jnp.where(left, ma, mb) + b_ref[:, cs]
            s = _gelu(u_ref[rs, cs]) * mixed
            o_ref[rs, cs] = s
    s_all = o_ref[...]
    o_ref[...] = _rms(s_all) * og_ref[...]


def _sgu(ug, norm_g, w_sp_bf16, b_full, out_g, *, width, chunk, group_dim):
    t = ug.shape[0]
    tm = min(512, t)
    assert t % tm == 0 and tm % chunk == 0 and width % LANES == 0 and 2 * group_dim == LANES
    n_groups = w_sp_bf16.shape[0]
    kern = functools.partial(_sgu_kernel, chunk=chunk, group_dim=group_dim)
    return pl.pallas_call(
        kern,
        grid=(t // tm,),
        in_specs=[
            pl.BlockSpec((tm, width), lambda i: (i, 0)),
            pl.BlockSpec((tm, width), lambda i: (i, 1)),
            pl.BlockSpec((1, width), lambda i: (0, 0)),
            pl.BlockSpec((n_groups, chunk, chunk), lambda i: (0, 0, 0)),
            pl.BlockSpec((chunk, width), lambda i: (0, 0)),
            pl.BlockSpec((1, width), lambda i: (0, 0)),
        ],
        out_specs=pl.BlockSpec((tm, width), lambda i: (i, 0)),
        out_shape=jax.ShapeDtypeStruct((t, width), _F32),
        compiler_params=_cparams(1),
        name="spatial_gating",
    )(ug, ug, norm_g, w_sp_bf16, b_full, out_g)


def _out_router_kernel(a_ref, s_ref, x_ref, ag_ref, w_ref, n2g_ref, wr_ref, br_ref,
                       x2_ref, xn_ref, idx_ref, gate_ref):
    an = _rms(a_ref[...]) * ag_ref[...]
    mix = jnp.concatenate([an, s_ref[...]], axis=-1).astype(_BF16)
    x2 = x_ref[...] + jnp.dot(mix, w_ref[...], preferred_element_type=_F32)
    x2_ref[...] = x2
    xn = _rms(x2) * n2g_ref[...]
    xn_ref[...] = xn
    logits = jnp.dot(xn, wr_ref[...], preferred_element_type=_F32,
                     precision=lax.Precision.HIGHEST) + br_ref[...]
    tm = logits.shape[0]
    lane = lax.broadcasted_iota(jnp.int32, (tm, LANES), 1)
    lane_f = lane.astype(_F32)
    idx_acc = jnp.zeros((tm, LANES), _F32)
    vals = []
    cur = logits
    for k in range(TOP_K):
        m = jnp.max(cur, axis=-1, keepdims=True)
        ik = jnp.min(jnp.where(cur == m, lane_f, float(LANES)), axis=-1, keepdims=True)
        vals.append(m)
        idx_acc = jnp.where(lane == k, ik, idx_acc)
        cur = jnp.where(lane_f == ik, -jnp.inf, cur)
    exps = [jnp.exp(v - vals[0]) for v in vals]
    denom = exps[0]
    for e in exps[1:]:
        denom = denom + e
    gate_acc = jnp.zeros((tm, LANES), _F32)
    for k in range(TOP_K):
        gate_acc = jnp.where(lane == k, exps[k] / denom, gate_acc)
    idx_ref[...] = idx_acc.astype(jnp.int32)
    gate_ref[...] = gate_acc


def _out_router(a_out, s_out, x, attn_out_g, w_out_bf16, norm2_g, w_router_pad, b_router_pad):
    t, d = x.shape
    wa = a_out.shape[1]
    ws = s_out.shape[1]
    tm = min(512, t)
    assert t % tm == 0
    row = lambda n: pl.BlockSpec((tm, n), lambda i: (i, 0))
    full = lambda a, b: pl.BlockSpec((a, b), lambda i: (0, 0))
    return pl.pallas_call(
        _out_router_kernel,
        grid=(t // tm,),
        in_specs=[row(wa), row(ws), row(d), full(1, wa), full(wa + ws, d), full(1, d),
                  full(d, LANES), full(1, LANES)],
        out_specs=[row(d), row(d), row(LANES), row(LANES)],
        out_shape=[jax.ShapeDtypeStruct((t, d), _F32), jax.ShapeDtypeStruct((t, d), _F32),
                   jax.ShapeDtypeStruct((t, LANES), jnp.int32), jax.ShapeDtypeStruct((t, LANES), _F32)],
        compiler_params=_cparams(1),
        name="out_proj_router",
    )(a_out, s_out, x, attn_out_g, w_out_bf16, norm2_g, w_router_pad, b_router_pad)


def _membership(idx):
    lane = lax.broadcasted_iota(jnp.int32, idx.shape, 1)
    return [lane == idx[:, k:k + 1] for k in range(TOP_K)]


def _count_kernel(idx_ref, cnt_ref):
    @pl.when(pl.program_id(0) == 0)
    def _():
        cnt_ref[...] = jnp.zeros_like(cnt_ref)

    member = sum(oh.astype(_F32) for oh in _membership(idx_ref[...]))
    cnt_ref[...] += jnp.sum(member, axis=0, keepdims=True)


def _slot_kernel(idx_ref, start_ref, dst_ref, carry_ref):
    @pl.when(pl.program_id(0) == 0)
    def _():
        carry_ref[...] = jnp.zeros_like(carry_ref)

    onehots = _membership(idx_ref[...])
    member = sum(oh.astype(_F32) for oh in onehots)
    tb = member.shape[0]
    earlier = (lax.broadcasted_iota(jnp.int32, (tb, tb), 0) > lax.broadcasted_iota(jnp.int32, (tb, tb), 1))
    before = jnp.dot(earlier.astype(_BF16), member.astype(_BF16), preferred_element_type=_F32)
    slot_e = start_ref[...] + carry_ref[...] + before
    lane = lax.broadcasted_iota(jnp.int32, member.shape, 1)
    dst = jnp.zeros(member.shape, _F32)
    for k in range(TOP_K):
        dk = jnp.sum(jnp.where(onehots[k], slot_e, 0.0), axis=-1, keepdims=True)
        dst = jnp.where(lane == k, dk, dst)
    dst_ref[...] = dst.astype(jnp.int32)
    carry_ref[...] += jnp.sum(member, axis=0, keepdims=True)


def _route_counts(idx_pad):
    t = idx_pad.shape[0]
    tb = min(512, t)
    return pl.pallas_call(
        _count_kernel,
        grid=(t // tb,),
        in_specs=[pl.BlockSpec((tb, LANES), lambda i: (i, 0))],
        out_specs=pl.BlockSpec((1, LANES), lambda i: (0, 0)),
        out_shape=jax.ShapeDtypeStruct((1, LANES), _F32),
        compiler_params=_cparams(1),
        name="route_counts",
    )(idx_pad)


def _route_slots(idx_pad, starts_f32):
    t = idx_pad.shape[0]
    tb = min(512, t)
    return pl.pallas_call(
        _slot_kernel,
        grid=(t // tb,),
        in_specs=[pl.BlockSpec((tb, LANES), lambda i: (i, 0)), pl.BlockSpec((1, LANES), lambda i: (0, 0))],
        out_specs=pl.BlockSpec((tb, LANES), lambda i: (i, 0)),
        out_shape=jax.ShapeDtypeStruct((t, LANES), jnp.int32),
        scratch_shapes=[pltpu.VMEM((1, LANES), _F32)],
        compiler_params=_cparams(1),
        name="route_slots",
    )(idx_pad, starts_f32)


ROUTE_TB = 256


def _scatter_rows_kernel(dst_ref, xn_ref, xs_in_ref, xs_ref, sem):
    del xs_in_ref
    tb = xn_ref.shape[0]

    def row_copy(t, d):
        return pltpu.make_async_copy(xn_ref.at[pl.ds(t, 1)], xs_ref.at[pl.ds(d, 1)], sem)

    def issue(t, c):
        for k in range(TOP_K):
            row_copy(t, dst_ref[t * TOP_K + k]).start()
        return c

    def drain(t, c):
        for k in range(TOP_K):
            row_copy(t, dst_ref[t * TOP_K + k]).wait()
        return c

    lax.fori_loop(0, tb, issue, 0)
    lax.fori_loop(0, tb, drain, 0)


def _scatter_rows(dst_flat, xn, n_slots):
    t, d = xn.shape
    tb = min(ROUTE_TB, t)
    zeros = jnp.zeros((n_slots, d), _F32)
    return pl.pallas_call(
        _scatter_rows_kernel,
        grid=(t // tb,),
        in_specs=[
            pl.BlockSpec((tb * TOP_K,), lambda i: (i,), memory_space=pltpu.SMEM),
            pl.BlockSpec((tb, d), lambda i: (i, 0)),
            pl.BlockSpec(memory_space=pl.ANY),
        ],
        out_specs=pl.BlockSpec(memory_space=pl.ANY),
        out_shape=jax.ShapeDtypeStruct((n_slots, d), _F32),
        scratch_shapes=[pltpu.SemaphoreType.DMA(())],
        input_output_aliases={2: 0},
        compiler_params=_cparams(1),
        name="scatter_rows",
    )(dst_flat, xn, zeros)


def _combine_kernel(dst_ref, gate_ref, x2_ref, ys_ref, o_ref, rows_ref, sem):
    tb = x2_ref.shape[0]

    def row_copy(t, k, d):
        return pltpu.make_async_copy(ys_ref.at[pl.ds(d, 1)], rows_ref.at[k, pl.ds(t, 1)], sem)

    def issue(t, c):
        for k in range(TOP_K):
            row_copy(t, k, dst_ref[t * TOP_K + k]).start()
        return c

    def drain(t, c):
        for k in range(TOP_K):
            row_copy(t, k, dst_ref[t * TOP_K + k]).wait()
        return c

    lax.fori_loop(0, tb, issue, 0)
    lax.fori_loop(0, tb, drain, 0)
    acc = x2_ref[...]
    gates = gate_ref[...]
    for k in range(TOP_K):
        acc = acc + gates[:, k:k + 1] * rows_ref[k]
    o_ref[...] = acc


def _combine(dst_flat, gates_pad, x2, ys):
    t, d = x2.shape
    tb = min(ROUTE_TB, t)
    return pl.pallas_call(
        _combine_kernel,
        grid=(t // tb,),
        in_specs=[
            pl.BlockSpec((tb * TOP_K,), lambda i: (i,), memory_space=pltpu.SMEM),
            pl.BlockSpec((tb, LANES), lambda i: (i, 0)),
            pl.BlockSpec((tb, d), lambda i: (i, 0)),
            pl.BlockSpec(memory_space=pl.ANY),
        ],
        out_specs=pl.BlockSpec((tb, d), lambda i: (i, 0)),
        out_shape=jax.ShapeDtypeStruct((t, d), _F32),
        scratch_shapes=[pltpu.VMEM((TOP_K, tb, d), _F32), pltpu.SemaphoreType.DMA(())],
        compiler_params=_cparams(1),
        name="combine_rows",
    )(dst_flat, gates_pad, x2, ys)


def _ffn_kernel(te_ref, tr_ref, tblk_ref, x_ref, wg_ref, wl_ref, bg_ref, bl_ref, wd_ref, bd_ref, ys_in_ref,
                o_ref, xb_ref, *, sub):
    del te_ref, tblk_ref, ys_in_ref
    i = pl.program_id(0)
    j = pl.program_id(1)
    n_rows = tr_ref[i]
    tm, d = o_ref.shape

    @pl.when(jnp.logical_and(n_rows > 0, j == 0))
    def _():
        xb_ref[...] = x_ref[...].astype(_BF16)
        o_ref[...] = jnp.broadcast_to(bd_ref[...], (tm, d))

    def sub_block(s, c):
        r0 = pl.multiple_of(s * sub, sub)
        xb = xb_ref[pl.ds(r0, sub), :]
        hg = jnp.dot(xb, wg_ref[...], preferred_element_type=_F32) + bg_ref[...]
        hl = jnp.dot(xb, wl_ref[...], preferred_element_type=_F32) + bl_ref[...]
        gate = jnp.minimum(hg, SWIGLU_LIMIT)
        lin = jnp.clip(hl, -SWIGLU_LIMIT, SWIGLU_LIMIT)
        act = gate * (1.0 / (1.0 + jnp.exp(-SWIGLU_ALPHA * gate))) * (lin + 1.0)
        o_ref[pl.ds(r0, sub), :] += jnp.dot(act.astype(_BF16), wd_ref[...], preferred_element_type=_F32)
        return c

    lax.fori_loop(0, (n_rows + sub - 1) // sub, sub_block, 0)


def _expert_ffn(tile_expert, tile_rows, tile_blk, xs, w_gu_bf16, b_gu, w_d_bf16, b_d, *, tm, tn, sub):
    n_slots, d = xs.shape
    n_exp, _, de2 = w_gu_bf16.shape
    de = de2 // 2
    nj = de // tn
    n_tiles = n_slots // tm
    assert de % tn == 0 and n_slots % tm == 0 and tm % sub == 0
    grid_spec = pltpu.PrefetchScalarGridSpec(
        num_scalar_prefetch=3,
        grid=(n_tiles, nj),
        in_specs=[
            pl.BlockSpec((tm, d), lambda i, j, te, tr, tb: (tb[i], 0)),
            pl.BlockSpec((None, d, tn), lambda i, j, te, tr, tb: (te[i], 0, j)),
            pl.BlockSpec((None, d, tn), lambda i, j, te, tr, tb: (te[i], 0, nj + j)),
            pl.BlockSpec((None, 1, tn), lambda i, j, te, tr, tb: (te[i], 0, j)),
            pl.BlockSpec((None, 1, tn), lambda i, j, te, tr, tb: (te[i], 0, nj + j)),
            pl.BlockSpec((None, tn, d), lambda i, j, te, tr, tb: (te[i], j, 0)),
            pl.BlockSpec((None, 1, d), lambda i, j, te, tr, tb: (te[i], 0, 0)),
            pl.BlockSpec(memory_space=pl.ANY),
        ],
        out_specs=pl.BlockSpec((tm, d), lambda i, j, te, tr, tb: (tb[i], 0)),
        scratch_shapes=[pltpu.VMEM((tm, d), _BF16)],
    )
    return pl.pallas_call(
        functools.partial(_ffn_kernel, sub=sub),
        grid_spec=grid_spec,
        out_shape=jax.ShapeDtypeStruct((n_slots, d), _F32),
        input_output_aliases={10: 0},
        compiler_params=_cparams(2),
        name="expert_ffn",
    )(tile_expert, tile_rows, tile_blk, xs, w_gu_bf16, w_gu_bf16,
      b_gu.reshape(n_exp, 1, de2), b_gu.reshape(n_exp, 1, de2), w_d_bf16, b_d.reshape(n_exp, 1, d),
      jnp.zeros((n_slots, d), _F32))


def _tile_plan(counts, *, tm, n_tiles):
    n_exp = counts.shape[0]
    tiles_e = (counts + tm - 1) // tm
    tile_end = jnp.cumsum(tiles_e)
    tile_beg = tile_end - tiles_e
    starts = tile_beg * tm
    n_active = tile_end[-1]
    tid = jnp.minimum(jnp.arange(n_tiles, dtype=jnp.int32), n_active - 1)
    expert = jnp.minimum(jnp.searchsorted(tile_end, tid, side="right"), n_exp - 1).astype(jnp.int32)
    rows = jnp.clip(counts[expert] - (tid - tile_beg[expert]) * tm, 0, tm)
    rows = jnp.where(jnp.arange(n_tiles) < n_active, rows, 0).astype(jnp.int32)
    return starts, expert, rows, tid.astype(jnp.int32)


def _layer(x, norm1_g, w_in, q_norm_g, k_norm_g, rpb, sgu_norm_g, w_spatial, b_spatial, attn_out_g,
           sgu_out_g, w_out, norm2_g, w_router, b_router, w_gate_up, b_gate_up, w_down, b_down):
    batch, seq, d = x.shape
    t = batch * seq
    head_dim = q_norm_g.shape[-1]
    n_heads = rpb.shape[0]
    attn_width = n_heads * head_dim
    n_groups, chunk, _ = w_spatial.shape
    sgu_width = sgu_norm_g.shape[-1]
    group_dim = sgu_width // n_groups
    win_rows_max = (rpb.shape[1] + 1) // 2
    win_cols = (rpb.shape[2] + 1) // 2
    rows = seq // GRID_W
    kh = min(win_rows_max, rows)
    n_exp = w_router.shape[-1]
    assert n_exp <= LANES and seq % GRID_W == 0 and w_in.shape[1] == 3 * attn_width + 2 * sgu_width

    xf = x.reshape(t, d)
    row = lambda v: v.reshape(1, -1).astype(_F32)

    w_in_b = w_in.astype(_BF16)
    pair = lambda g: jnp.concatenate([g, g]).reshape(1, 1, LANES)
    head_gain = jnp.concatenate([pair(q_norm_g), pair(k_norm_g), jnp.ones((1, 1, LANES), _F32)], axis=0)
    qkv = _norm_proj(xf, row(norm1_g), w_in_b[:, :3 * attn_width], head_gain,
                     slab_out=True, attn_width=attn_width, head_dim=head_dim)
    ug = _norm_proj(xf, row(norm1_g), w_in_b[:, 3 * attn_width:], head_gain,
                    slab_out=False, attn_width=attn_width, head_dim=head_dim)

    bias_tab = _attention_bias_table(rpb, kh=kh, win_cols=win_cols)
    a_out = _attention(qkv, bias_tab, batch=batch, seq=seq, n_heads=n_heads, head_dim=head_dim, kh=kh)

    b_full = jnp.repeat(b_spatial.T, group_dim, axis=1)
    s_out = _sgu(ug, row(sgu_norm_g), w_spatial.astype(_BF16), b_full, row(sgu_out_g),
                 width=sgu_width, chunk=chunk, group_dim=group_dim)

    w_router_pad = jnp.zeros((d, LANES), _F32).at[:, :n_exp].set(w_router)
    b_router_pad = jnp.full((1, LANES), NEG_BIG, _F32).at[0, :n_exp].set(b_router)
    x2, xn, idx_pad, gates_pad = _out_router(a_out, s_out, xf, row(attn_out_g), w_out.astype(_BF16),
                                             row(norm2_g), w_router_pad, b_router_pad)

    de = w_down.shape[1]
    tm = min(1024, t)
    tn = min(512, de)
    sub = min(256, tm)
    n_tiles = (t * TOP_K) // tm + n_exp
    counts = _route_counts(idx_pad)[0, :n_exp].astype(jnp.int32)
    starts, tile_expert, tile_rows, tile_blk = _tile_plan(counts, tm=tm, n_tiles=n_tiles)
    starts_pad = jnp.zeros((1, LANES), _F32).at[0, :n_exp].set(starts.astype(_F32))
    dst_pad = _route_slots(idx_pad, starts_pad)
    dst_flat = dst_pad[:, :TOP_K].reshape(-1)

    xs = _scatter_rows(dst_flat, xn, n_tiles * tm)
    ys = _expert_ffn(tile_expert, tile_rows, tile_blk, xs, w_gate_up.astype(_BF16), b_gate_up,
                     w_down.astype(_BF16), b_down, tm=tm, tn=tn, sub=sub)
    out = _combine(dst_flat, gates_pad, x2, ys)
    return out.reshape(batch, seq, d)


def kernel(x, norm1_g, w_in, q_norm_g, k_norm_g, rpb, sgu_norm_g, w_spatial, b_spatial, attn_out_g,
           sgu_out_g, w_out, norm2_g, w_router, b_router, w_gate_up, b_gate_up, w_down, b_down):
    depth = norm1_g.shape[0]
    for l in range(depth):
        x = _layer(x, norm1_g[l], w_in[l], q_norm_g[l], k_norm_g[l], rpb[l], sgu_norm_g[l], w_spatial[l],
                   b_spatial[l], attn_out_g[l], sgu_out_g[l], w_out[l], norm2_g[l], w_router[l], b_router[l],
                   w_gate_up[l], b_gate_up[l], w_down[l], b_down[l])
    return x
```

```python
import functools

import jax
import jax.numpy as jnp
from jax import lax
from jax.experimental import pallas as pl
from jax.experimental.pallas import tpu as pltpu

GRID_W = 64
TOP_K = 4
SWIGLU_LIMIT = 7.0
SWIGLU_ALPHA = 1.702
EPS = 1e-6
LANES = 128
NEG_BIG = -1e30
VMEM_LIMIT_BYTES = 56 * 1024 * 1024

_F32 = jnp.float32
_BF16 = jnp.bfloat16


def _cparams(n_axes):
    return pltpu.CompilerParams(
        dimension_semantics=("arbitrary",) * n_axes,
        vmem_limit_bytes=VMEM_LIMIT_BYTES)


def _rms(x):
    return x * lax.rsqrt(jnp.mean(x * x, axis=-1, keepdims=True) + EPS)


def _gelu(x):
    return 0.5 * x * (1.0 + lax.erf(x * 0.7071067811865476))


def _norm_proj_kernel(x_ref, g_ref, w_ref, hg_ref, o_ref, h_ref, *, slab_out, n_norm_tiles, head_dim):
    j = pl.program_id(1)

    @pl.when(j == 0)
    def _():
        h_ref[...] = (_rms(x_ref[...]) * g_ref[...]).astype(_BF16)

    res = jnp.dot(h_ref[...], w_ref[...], preferred_element_type=_F32)
    if not slab_out:
        o_ref[...] = res
        return

    n_slabs = res.shape[1] // LANES
    row_first = jnp.where(lax.broadcasted_iota(jnp.int32, (LANES, LANES), 0) < head_dim, 1.0, 0.0)
    col_first = jnp.where(lax.broadcasted_iota(jnp.int32, (LANES, LANES), 1) < head_dim, 1.0, 0.0)
    same_head = (1.0 - jnp.abs(row_first - col_first)).astype(_BF16)
    for s in range(n_slabs):
        r = res[:, s * LANES:(s + 1) * LANES]

        @pl.when(j < n_norm_tiles)
        def _():
            ss = r * r
            hi = ss.astype(_BF16)
            lo = (ss - hi.astype(_F32)).astype(_BF16)
            head_ss = (jnp.dot(hi, same_head, preferred_element_type=_F32)
                       + jnp.dot(lo, same_head, preferred_element_type=_F32))
            o_ref[s] = r * lax.rsqrt(head_ss / head_dim + EPS) * hg_ref[0]

        @pl.when(j >= n_norm_tiles)
        def _():
            o_ref[s] = r


def _norm_proj(x, g, w_bf16, head_gain, *, slab_out, attn_width, head_dim):
    t, d = x.shape
    n = w_bf16.shape[1]
    tm = min(1024, t)
    tn = min(1024, attn_width if slab_out else n)
    assert t % tm == 0 and n % tn == 0 and tn % LANES == 0
    if slab_out:
        assert attn_width % tn == 0 and 2 * head_dim == LANES
        tiles_per_part = attn_width // tn
        out_shape = jax.ShapeDtypeStruct((n // LANES, t, LANES), _F32)
        out_spec = pl.BlockSpec((tn // LANES, tm, LANES), lambda i, j: (j, i, 0))
        hg_spec = pl.BlockSpec((1, 1, LANES), lambda i, j: (jnp.minimum(j // tiles_per_part, 2), 0, 0))
        n_norm_tiles = 2 * tiles_per_part
    else:
        out_shape = jax.ShapeDtypeStruct((t, n), _F32)
        out_spec = pl.BlockSpec((tm, tn), lambda i, j: (i, j))
        hg_spec = pl.BlockSpec((1, 1, LANES), lambda i, j: (0, 0, 0))
        n_norm_tiles = 0
    kern = functools.partial(_norm_proj_kernel, slab_out=slab_out, n_norm_tiles=n_norm_tiles, head_dim=head_dim)
    return pl.pallas_call(
        kern,
        grid=(t // tm, n // tn),
        in_specs=[
            pl.BlockSpec((tm, d), lambda i, j: (i, 0)),
            pl.BlockSpec((1, d), lambda i, j: (0, 0)),
            pl.BlockSpec((d, tn), lambda i, j: (0, j)),
            hg_spec,
        ],
        out_specs=out_spec,
        out_shape=out_shape,
        scratch_shapes=[pltpu.VMEM((tm, d), _BF16)],
        compiler_params=_cparams(2),
        name="norm_proj_qkv" if slab_out else "norm_proj_ug",
    )(x, g, w_bf16, head_gain)


def _attn_kernel(q_ref, kp_ref, kc_ref, kn_ref, vp_ref, vc_ref, vn_ref, bias_ref, o_ref, kwin, vwin,
                 *, rows, rblk, kh, head_dim, scale):
    rb = pl.program_id(2)
    blk = rblk * GRID_W
    for w, (kr, vr) in enumerate(((kp_ref, vp_ref), (kc_ref, vc_ref), (kn_ref, vn_ref))):
        kwin[w * blk:(w + 1) * blk, :] = kr[...].astype(_BF16)
        vwin[w * blk:(w + 1) * blk, :] = vr[...].astype(_BF16)
    lane = lax.broadcasted_iota(jnp.int32, (1, LANES), 1)
    left = lane < head_dim
    nk = kh * GRID_W
    for i in range(rblk):
        r = rb * rblk + i
        row_start = jnp.clip(r - kh // 2, 0, rows - kh)
        off = pl.multiple_of((row_start - (rb * rblk - rblk)) * GRID_W, GRID_W)
        sh = r - row_start
        ks = kwin[pl.ds(off, nk), :]
        vs = vwin[pl.ds(off, nk), :]
        qi = q_ref[i * GRID_W:(i + 1) * GRID_W, :] * scale
        halves = []
        for half in range(2):
            keep = left if half == 0 else jnp.logical_not(left)
            qm = jnp.where(keep, qi, 0.0).astype(_BF16)
            s = lax.dot_general(qm, ks, (((1,), (1,)), ((), ())), preferred_element_type=_F32)
            s = s + bias_ref[sh, half]
            p = jnp.exp(s - jnp.max(s, axis=-1, keepdims=True))
            denom = jnp.sum(p, axis=-1, keepdims=True)
            pv = jnp.dot(p.astype(_BF16), vs, preferred_element_type=_F32)
            halves.append(pv / denom)
        o_ref[i * GRID_W:(i + 1) * GRID_W, :] = jnp.where(left, halves[0], halves[1])


def _attention(qkv, bias_tab, *, batch, seq, n_heads, head_dim, kh):
    n_hp = n_heads // 2
    rows = seq // GRID_W
    rblk = kh
    assert rows % rblk == 0 and rows >= kh and rblk >= 4
    nrb = rows // rblk
    blk = rblk * GRID_W
    t = batch * seq

    def qmap(b, hp, rb):
        return (hp, b * nrb + rb, 0)

    def kvmap(part, delta):
        def f(b, hp, rb):
            return (part * n_hp + hp, b * nrb + jnp.clip(rb + delta, 0, nrb - 1), 0)
        return f

    slab = lambda m: pl.BlockSpec((None, blk, LANES), m)
    kern = functools.partial(_attn_kernel, rows=rows, rblk=rblk, kh=kh, head_dim=head_dim,
                             scale=float(head_dim) ** -0.5)
    return pl.pallas_call(
        kern,
        grid=(batch, n_hp, nrb),
        in_specs=[
            slab(qmap),
            slab(kvmap(1, -1)), slab(kvmap(1, 0)), slab(kvmap(1, 1)),
            slab(kvmap(2, -1)), slab(kvmap(2, 0)), slab(kvmap(2, 1)),
            pl.BlockSpec((kh, 2, GRID_W, kh * GRID_W), lambda b, hp, rb: (0, hp, 0, 0)),
        ],
        out_specs=pl.BlockSpec((blk, LANES), lambda b, hp, rb: (b * nrb + rb, hp)),
        out_shape=jax.ShapeDtypeStruct((t, n_hp * LANES), _F32),
        scratch_shapes=[pltpu.VMEM((3 * blk, LANES), _BF16), pltpu.VMEM((3 * blk, LANES), _BF16)],
        compiler_params=_cparams(3),
        name="nbr_attention",
    )(qkv, qkv, qkv, qkv, qkv, qkv, qkv, bias_tab)


def _attention_bias_table(rpb, *, kh, win_cols):
    n_heads, n_rpb_rows, _ = rpb.shape
    win_rows_max = (n_rpb_rows + 1) // 2
    qc = jnp.arange(GRID_W)[:, None]
    kc = jnp.arange(GRID_W)[None, :]
    col_start = jnp.clip(qc - win_cols // 2, 0, GRID_W - win_cols)
    valid = (kc >= col_start) & (kc < col_start + win_cols)
    padded = jnp.pad(rpb.astype(_F32), ((0, 0), (0, 0), (GRID_W, GRID_W)))
    base = GRID_W + win_cols - 1
    cols = jnp.stack([padded[:, :, base - q:base - q + GRID_W] for q in range(GRID_W)], axis=2)
    cols = jnp.where(valid[None, None], cols, NEG_BIG)
    tab = jnp.stack([cols[:, win_rows_max - 1 - sh:win_rows_max - 1 - sh + kh] for sh in range(kh)], axis=0)
    tab = tab.transpose(0, 1, 3, 2, 4)
    return tab.reshape(kh, n_heads, GRID_W, kh * GRID_W)


def _sgu_kernel(u_ref, g_ref, ng_ref, w_ref, b_ref, og_ref, o_ref, *, chunk, group_dim):
    tm, width = u_ref.shape
    gg = _gelu(g_ref[...])
    gn = (_rms(gg) * ng_ref[...]).astype(_BF16)
    lane = lax.broadcasted_iota(jnp.int32, (1, LANES), 1)
    left = lane < group_dim
    for c in range(tm // chunk):
        rs = slice(c * chunk, (c + 1) * chunk)
        for gp in range(width // LANES):
            cs = slice(gp * LANES, (gp + 1) * LANES)
            xg = gn[rs, cs]
            ma = jnp.dot(w_ref[2 * gp], xg, preferred_element_type=_F32)
            mb = jnp.dot(w_ref[2 * gp + 1], xg, preferred_element_type=_F32)
            mixed = jnp.where(left, ma, mb) + b_ref[:, cs]
            s = _gelu(u_ref[rs, cs]) * mixed
            o_ref[rs, cs] = s
    s_all = o_ref[...]
    o_ref[...] = _rms(s_all) * og_ref[...]


def _sgu(ug, norm_g, w_sp_bf16, b_full, out_g, *, width, chunk, group_dim):
    t = ug.shape[0]
    tm = min(512, t)
    assert t % tm == 0 and tm % chunk == 0 and width % LANES == 0 and 2 * group_dim == LANES
    n_groups = w_sp_bf16.shape[0]
    kern = functools.partial(_sgu_kernel, chunk=chunk, group_dim=group_dim)
    return pl.pallas_call(
        kern,
        grid=(t // tm,),
        in_specs=[
            pl.BlockSpec((tm, width), lambda i: (i, 0)),
            pl.BlockSpec((tm, width), lambda i: (i, 1)),
            pl.BlockSpec((1, width), lambda i: (0, 0)),
            pl.BlockSpec((n_groups, chunk, chunk), lambda i: (0, 0, 0)),
            pl.BlockSpec((chunk, width), lambda i: (0, 0)),
            pl.BlockSpec((1, width), lambda i: (0, 0)),
        ],
        out_specs=pl.BlockSpec((tm, width), lambda i: (i, 0)),
        out_shape=jax.ShapeDtypeStruct((t, width), _F32),
        compiler_params=_cparams(1),
        name="spatial_gating",
    )(ug, ug, norm_g, w_sp_bf16, b_full, out_g)


def _out_router_kernel(a_ref, s_ref, x_ref, ag_ref, w_ref, n2g_ref, wr_ref, br_ref,
                       x2_ref, xn_ref, idx_ref, gate_ref):
    an = _rms(a_ref[...]) * ag_ref[...]
    mix = jnp.concatenate([an, s_ref[...]], axis=-1).astype(_BF16)
    x2 = x_ref[...] + jnp.dot(mix, w_ref[...], preferred_element_type=_F32)
    x2_ref[...] = x2
    xn = _rms(x2) * n2g_ref[...]
    xn_ref[...] = xn
    xh = xn.astype(_BF16)
    xl = (xn - xh.astype(_F32)).astype(_BF16)
    parts = jnp.dot(jnp.concatenate([xh, xl], axis=-1), wr_ref[...], preferred_element_type=_F32)
    logits = parts + pltpu.roll(parts, LANES // 2, axis=1) + br_ref[...]
    tm = logits.shape[0]
    lane = lax.broadcasted_iota(jnp.int32, (tm, LANES), 1)
    lane_f = lane.astype(_F32)
    idx_acc = jnp.zeros((tm, LANES), _F32)
    vals = []
    cur = logits
    for k in range(TOP_K):
        m = jnp.max(cur, axis=-1, keepdims=True)
        ik = jnp.min(jnp.where(cur == m, lane_f, float(LANES)), axis=-1, keepdims=True)
        vals.append(m)
        idx_acc = jnp.where(lane == k, ik, idx_acc)
        cur = jnp.where(lane_f == ik, -jnp.inf, cur)
    exps = [jnp.exp(v - vals[0]) for v in vals]
    denom = exps[0]
    for e in exps[1:]:
        denom = denom + e
    gate_acc = jnp.zeros((tm, LANES), _F32)
    for k in range(TOP_K):
        gate_acc = jnp.where(lane == k, exps[k] / denom, gate_acc)
    idx_ref[...] = idx_acc.astype(jnp.int32)
    gate_ref[...] = gate_acc


def _out_router(a_out, s_out, x, attn_out_g, w_out_bf16, norm2_g, w_router_pad, b_router_pad):
    t, d = x.shape
    wa = a_out.shape[1]
    ws = s_out.shape[1]
    tm = min(512, t)
    assert t % tm == 0
    row = lambda n: pl.BlockSpec((tm, n), lambda i: (i, 0))
    full = lambda a, b: pl.BlockSpec((a, b), lambda i: (0, 0))
    return pl.pallas_call(
        _out_router_kernel,
        grid=(t // tm,),
        in_specs=[row(wa), row(ws), row(d), full(1, wa), full(wa + ws, d), full(1, d),
                  full(2 * d, LANES), full(1, LANES)],
        out_specs=[row(d), row(d), row(LANES), row(LANES)],
        out_shape=[jax.ShapeDtypeStruct((t, d), _F32), jax.ShapeDtypeStruct((t, d), _F32),
                   jax.ShapeDtypeStruct((t, LANES), jnp.int32), jax.ShapeDtypeStruct((t, LANES), _F32)],
        compiler_params=_cparams(1),
        name="out_proj_router",
    )(a_out, s_out, x, attn_out_g, w_out_bf16, norm2_g, w_router_pad, b_router_pad)


def _membership(idx):
    lane = lax.broadcasted_iota(jnp.int32, idx.shape, 1)
    return [lane == idx[:, k:k + 1] for k in range(TOP_K)]


def _count_kernel(idx_ref, cnt_ref):
    @pl.when(pl.program_id(0) == 0)
    def _():
        cnt_ref[...] = jnp.zeros_like(cnt_ref)

    member = sum(oh.astype(_F32) for oh in _membership(idx_ref[...]))
    cnt_ref[...] += jnp.sum(member, axis=0, keepdims=True)


def _slot_kernel(idx_ref, start_ref, dst_ref, carry_ref):
    @pl.when(pl.program_id(0) == 0)
    def _():
        carry_ref[...] = jnp.zeros_like(carry_ref)

    onehots = _membership(idx_ref[...])
    member = sum(oh.astype(_F32) for oh in onehots)
    tb = member.shape[0]
    earlier = (lax.broadcasted_iota(jnp.int32, (tb, tb), 0) > lax.broadcasted_iota(jnp.int32, (tb, tb), 1))
    before = jnp.dot(earlier.astype(_BF16), member.astype(_BF16), preferred_element_type=_F32)
    slot_e = start_ref[...] + carry_ref[...] + before
    lane = lax.broadcasted_iota(jnp.int32, member.shape, 1)
    dst = jnp.zeros(member.shape, _F32)
    for k in range(TOP_K):
        dk = jnp.sum(jnp.where(onehots[k], slot_e, 0.0), axis=-1, keepdims=True)
        dst = jnp.where(lane == k, dk, dst)
    dst_ref[...] = dst.astype(jnp.int32)
    carry_ref[...] += jnp.sum(member, axis=0, keepdims=True)


def _route_counts(idx_pad):
    t = idx_pad.shape[0]
    tb = min(512, t)
    return pl.pallas_call(
        _count_kernel,
        grid=(t // tb,),
        in_specs=[pl.BlockSpec((tb, LANES), lambda i: (i, 0))],
        out_specs=pl.BlockSpec((1, LANES), lambda i: (0, 0)),
        out_shape=jax.ShapeDtypeStruct((1, LANES), _F32),
        compiler_params=_cparams(1),
        name="route_counts",
    )(idx_pad)


def _route_slots(idx_pad, starts_f32):
    t = idx_pad.shape[0]
    tb = min(512, t)
    return pl.pallas_call(
        _slot_kernel,
        grid=(t // tb,),
        in_specs=[pl.BlockSpec((tb, LANES), lambda i: (i, 0)), pl.BlockSpec((1, LANES), lambda i: (0, 0))],
        out_specs=pl.BlockSpec((tb, LANES), lambda i: (i, 0)),
        out_shape=jax.ShapeDtypeStruct((t, LANES), jnp.int32),
        scratch_shapes=[pltpu.VMEM((1, LANES), _F32)],
        compiler_params=_cparams(1),
        name="route_slots",
    )(idx_pad, starts_f32)


ROUTE_TB = 256


ZERO_ROWS = 256


def _scatter_rows_kernel(pad_beg_ref, pad_mid_ref, pad_end_ref, dst_ref, xn_ref, xs_ref, zero_ref, sem, zsem):
    tb = xn_ref.shape[0]
    first = pl.program_id(0) == 0
    n_regions = pad_beg_ref.shape[0]

    def row_copy(t, d):
        return pltpu.make_async_copy(xn_ref.at[pl.ds(t, 1)], xs_ref.at[pl.ds(d, 1)], sem)

    def issue(t, c):
        for k in range(TOP_K):
            row_copy(t, dst_ref[t * TOP_K + k]).start()
        return c

    def drain(t, c):
        for k in range(TOP_K):
            row_copy(t, dst_ref[t * TOP_K + k]).wait()
        return c

    def clear_padding(start):
        def region(e, c):
            beg, mid, end = pad_beg_ref[e], pad_mid_ref[e], pad_end_ref[e]

            def one_row(r, c2):
                cp = pltpu.make_async_copy(zero_ref.at[pl.ds(0, 1)], xs_ref.at[pl.ds(r, 1)], zsem)
                cp.start() if start else cp.wait()
                return c2

            def one_block(b, c2):
                r0 = pl.multiple_of(mid + b * ZERO_ROWS, ZERO_ROWS)
                cp = pltpu.make_async_copy(zero_ref, xs_ref.at[pl.ds(r0, ZERO_ROWS)], zsem)
                cp.start() if start else cp.wait()
                return c2

            lax.fori_loop(beg, mid, one_row, 0)
            lax.fori_loop(0, (end - mid) // ZERO_ROWS, one_block, 0)
            return c

        lax.fori_loop(0, n_regions, region, 0)

    @pl.when(first)
    def _():
        zero_ref[...] = jnp.zeros_like(zero_ref)
        clear_padding(True)

    lax.fori_loop(0, tb, issue, 0)
    lax.fori_loop(0, tb, drain, 0)

    @pl.when(first)
    def _():
        clear_padding(False)


def _scatter_rows(pad_beg, pad_mid, pad_end, dst_flat, xn, n_slots):
    t, d = xn.shape
    tb = min(ROUTE_TB, t)
    grid_spec = pltpu.PrefetchScalarGridSpec(
        num_scalar_prefetch=3,
        grid=(t // tb,),
        in_specs=[
            pl.BlockSpec((tb * TOP_K,), lambda i, *_: (i,), memory_space=pltpu.SMEM),
            pl.BlockSpec((tb, d), lambda i, *_: (i, 0)),
        ],
        out_specs=pl.BlockSpec(memory_space=pl.ANY),
        scratch_shapes=[pltpu.VMEM((ZERO_ROWS, d), _F32), pltpu.SemaphoreType.DMA(()),
                        pltpu.SemaphoreType.DMA(())],
    )
    return pl.pallas_call(
        _scatter_rows_kernel,
        grid_spec=grid_spec,
        out_shape=jax.ShapeDtypeStruct((n_slots, d), _F32),
        compiler_params=_cparams(1),
        name="scatter_rows",
    )(pad_beg, pad_mid, pad_end, dst_flat, xn)


def _combine_kernel(dst_ref, gate_ref, x2_ref, ys_ref, o_ref, rows_ref, sem):
    tb = x2_ref.shape[0]

    def row_copy(t, k, d):
        return pltpu.make_async_copy(ys_ref.at[pl.ds(d, 1)], rows_ref.at[k, pl.ds(t, 1)], sem)

    def issue(t, c):
        for k in range(TOP_K):
            row_copy(t, k, dst_ref[t * TOP_K + k]).start()
        return c

    def drain(t, c):
        for k in range(TOP_K):
            row_copy(t, k, dst_ref[t * TOP_K + k]).wait()
        return c

    lax.fori_loop(0, tb, issue, 0)
    lax.fori_loop(0, tb, drain, 0)
    acc = x2_ref[...]
    gates = gate_ref[...]
    for k in range(TOP_K):
        acc = acc + gates[:, k:k + 1] * rows_ref[k]
    o_ref[...] = acc


def _combine(dst_flat, gates_pad, x2, ys):
    t, d = x2.shape
    tb = min(ROUTE_TB, t)
    return pl.pallas_call(
        _combine_kernel,
        grid=(t // tb,),
        in_specs=[
            pl.BlockSpec((tb * TOP_K,), lambda i: (i,), memory_space=pltpu.SMEM),
            pl.BlockSpec((tb, LANES), lambda i: (i, 0)),
            pl.BlockSpec((tb, d), lambda i: (i, 0)),
            pl.BlockSpec(memory_space=pl.ANY),
        ],
        out_specs=pl.BlockSpec((tb, d), lambda i: (i, 0)),
        out_shape=jax.ShapeDtypeStruct((t, d), _F32),
        scratch_shapes=[pltpu.VMEM((TOP_K, tb, d), _F32), pltpu.SemaphoreType.DMA(())],
        compiler_params=_cparams(1),
        name="combine_rows",
    )(dst_flat, gates_pad, x2, ys)


def _ffn_kernel(te_ref, tr_ref, tblk_ref, x_ref, wg_ref, wl_ref, bg_ref, bl_ref, wd_ref, bd_ref,
                o_ref, xb_ref, *, sub):
    del te_ref, tblk_ref
    i = pl.program_id(0)
    j = pl.program_id(1)
    n_rows = tr_ref[i]
    tm, d = o_ref.shape

    @pl.when(jnp.logical_and(j == 0, n_rows > 0))
    def _():
        o_ref[...] = jnp.broadcast_to(bd_ref[...], (tm, d))

    @pl.when(jnp.logical_and(j == 0, n_rows == 0))
    def _():
        o_ref[...] = jnp.zeros((tm, d), _F32)

    def sub_block(s, c):
        r0 = pl.multiple_of(s * sub, sub)

        @pl.when(j == 0)
        def _():
            xb_ref[pl.ds(r0, sub), :] = x_ref[pl.ds(r0, sub), :].astype(_BF16)

        xb = xb_ref[pl.ds(r0, sub), :]
        hg = jnp.dot(xb, wg_ref[...], preferred_element_type=_F32) + bg_ref[...]
        hl = jnp.dot(xb, wl_ref[...], preferred_element_type=_F32) + bl_ref[...]
        gate = jnp.minimum(hg, SWIGLU_LIMIT)
        lin = jnp.clip(hl, -SWIGLU_LIMIT, SWIGLU_LIMIT)
        act = gate * (1.0 / (1.0 + jnp.exp(-SWIGLU_ALPHA * gate))) * (lin + 1.0)
        o_ref[pl.ds(r0, sub), :] += jnp.dot(act.astype(_BF16), wd_ref[...], preferred_element_type=_F32)
        return c

    lax.fori_loop(0, (n_rows + sub - 1) // sub, sub_block, 0)


def _expert_ffn(tile_expert, tile_rows, tile_blk, xs, w_gu_bf16, b_gu, w_d_bf16, b_d, *, tm, tn, sub):
    n_slots, d = xs.shape
    n_exp, _, de2 = w_gu_bf16.shape
    de = de2 // 2
    nj = de // tn
    n_tiles = n_slots // tm
    assert de % tn == 0 and n_slots % tm == 0 and tm % sub == 0
    grid_spec = pltpu.PrefetchScalarGridSpec(
        num_scalar_prefetch=3,
        grid=(n_tiles, nj),
        in_specs=[
            pl.BlockSpec((tm, d), lambda i, j, te, tr, tb: (tb[i], 0)),
            pl.BlockSpec((None, d, tn), lambda i, j, te, tr, tb: (te[i], 0, j)),
            pl.BlockSpec((None, d, tn), lambda i, j, te, tr, tb: (te[i], 0, nj + j)),
            pl.BlockSpec((None, 1, tn), lambda i, j, te, tr, tb: (te[i], 0, j)),
            pl.BlockSpec((None, 1, tn), lambda i, j, te, tr, tb: (te[i], 0, nj + j)),
            pl.BlockSpec((None, tn, d), lambda i, j, te, tr, tb: (te[i], j, 0)),
            pl.BlockSpec((None, 1, d), lambda i, j, te, tr, tb: (te[i], 0, 0)),
        ],
        out_specs=pl.BlockSpec((tm, d), lambda i, j, te, tr, tb: (i, 0)),
        scratch_shapes=[pltpu.VMEM((tm, d), _BF16)],
    )
    return pl.pallas_call(
        functools.partial(_ffn_kernel, sub=sub),
        grid_spec=grid_spec,
        out_shape=jax.ShapeDtypeStruct((n_slots, d), _F32),
        compiler_params=_cparams(2),
        name="expert_ffn",
    )(tile_expert, tile_rows, tile_blk, xs, w_gu_bf16, w_gu_bf16,
      b_gu.reshape(n_exp, 1, de2), b_gu.reshape(n_exp, 1, de2), w_d_bf16, b_d.reshape(n_exp, 1, d))


def _tile_plan(counts, *, tm, n_tiles):
    n_exp = counts.shape[0]
    tiles_e = (counts + tm - 1) // tm
    tile_end = jnp.cumsum(tiles_e)
    tile_beg = tile_end - tiles_e
    starts = tile_beg * tm
    n_active = tile_end[-1]
    tid = jnp.minimum(jnp.arange(n_tiles, dtype=jnp.int32), n_active - 1)
    expert = jnp.minimum(jnp.sum(tile_end[None, :] <= tid[:, None], axis=1), n_exp - 1).astype(jnp.int32)
    rows = jnp.clip(counts[expert] - (tid - tile_beg[expert]) * tm, 0, tm)
    rows = jnp.where(jnp.arange(n_tiles) < n_active, rows, 0).astype(jnp.int32)
    pad_beg = jnp.concatenate([starts + counts, (n_active * tm)[None]])
    pad_end = jnp.concatenate([tile_end * tm, jnp.full((1,), n_tiles * tm, jnp.int32)])
    pad_mid = jnp.minimum((pad_beg + ZERO_ROWS - 1) // ZERO_ROWS * ZERO_ROWS, pad_end)
    pads = tuple(p.astype(jnp.int32) for p in (pad_beg, pad_mid, pad_end))
    return starts, expert, rows, tid.astype(jnp.int32), pads


def _layer(x, norm1_g, w_in, q_norm_g, k_norm_g, rpb, sgu_norm_g, w_spatial, b_spatial, attn_out_g,
           sgu_out_g, w_out, norm2_g, w_router, b_router, w_gate_up, b_gate_up, w_down, b_down):
    batch, seq, d = x.shape
    t = batch * seq
    head_dim = q_norm_g.shape[-1]
    n_heads = rpb.shape[0]
    attn_width = n_heads * head_dim
    n_groups, chunk, _ = w_spatial.shape
    sgu_width = sgu_norm_g.shape[-1]
    group_dim = sgu_width // n_groups
    win_rows_max = (rpb.shape[1] + 1) // 2
    win_cols = (rpb.shape[2] + 1) // 2
    rows = seq // GRID_W
    kh = min(win_rows_max, rows)
    n_exp = w_router.shape[-1]
    assert TOP_K <= n_exp <= LANES // 2 and seq % GRID_W == 0 and w_in.shape[1] == 3 * attn_width + 2 * sgu_width

    xf = x.reshape(t, d)
    row = lambda v: v.reshape(1, -1).astype(_F32)

    w_in_b = w_in.astype(_BF16)
    pair = lambda g: jnp.concatenate([g, g]).reshape(1, 1, LANES)
    head_gain = jnp.concatenate([pair(q_norm_g), pair(k_norm_g), jnp.ones((1, 1, LANES), _F32)], axis=0)
    qkv = _norm_proj(xf, row(norm1_g), w_in_b[:, :3 * attn_width], head_gain,
                     slab_out=True, attn_width=attn_width, head_dim=head_dim)
    ug = _norm_proj(xf, row(norm1_g), w_in_b[:, 3 * attn_width:], head_gain,
                    slab_out=False, attn_width=attn_width, head_dim=head_dim)

    bias_tab = _attention_bias_table(rpb, kh=kh, win_cols=win_cols)
    a_out = _attention(qkv, bias_tab, batch=batch, seq=seq, n_heads=n_heads, head_dim=head_dim, kh=kh)

    b_full = jnp.repeat(b_spatial.T, group_dim, axis=1)
    s_out = _sgu(ug, row(sgu_norm_g), w_spatial.astype(_BF16), b_full, row(sgu_out_g),
                 width=sgu_width, chunk=chunk, group_dim=group_dim)

    half = LANES // 2
    w_r_hi = w_router.astype(_BF16)
    w_r_lo = (w_router - w_r_hi.astype(_F32)).astype(_BF16)
    blank = jnp.zeros((d, LANES), _BF16)
    w_router_pad = jnp.concatenate([blank.at[:, :n_exp].set(w_r_hi).at[:, half:half + n_exp].set(w_r_lo),
                                    blank.at[:, :n_exp].set(w_r_hi)], axis=0)
    b_router_pad = jnp.full((1, LANES), NEG_BIG, _F32).at[0, :n_exp].set(b_router)
    x2, xn, idx_pad, gates_pad = _out_router(a_out, s_out, xf, row(attn_out_g), w_out.astype(_BF16),
                                             row(norm2_g), w_router_pad, b_router_pad)

    de = w_down.shape[1]
    tm = min(1024, t)
    tn = min(512, de)
    sub = min(256, tm)
    n_tiles = (t * TOP_K) // tm + n_exp
    counts = _route_counts(idx_pad)[0, :n_exp].astype(jnp.int32)
    assert tm % ZERO_ROWS == 0
    starts, tile_expert, tile_rows, tile_blk, pads = _tile_plan(counts, tm=tm, n_tiles=n_tiles)
    starts_pad = jnp.zeros((1, LANES), _F32).at[0, :n_exp].set(starts.astype(_F32))
    dst_pad = _route_slots(idx_pad, starts_pad)
    dst_flat = dst_pad[:, :TOP_K].reshape(-1)

    xs = _scatter_rows(*pads, dst_flat, xn, n_tiles * tm)
    ys = _expert_ffn(tile_expert, tile_rows, tile_blk, xs, w_gate_up.astype(_BF16), b_gate_up,
                     w_down.astype(_BF16), b_down, tm=tm, tn=tn, sub=sub)
    out = _combine(dst_flat, gates_pad, x2, ys)
    return out.reshape(batch, seq, d)


def kernel(x, norm1_g, w_in, q_norm_g, k_norm_g, rpb, sgu_norm_g, w_spatial, b_spatial, attn_out_g,
           sgu_out_g, w_out, norm2_g, w_router, b_router, w_gate_up, b_gate_up, w_down, b_down):
    depth = norm1_g.shape[0]
    for l in range(depth):
        x = _layer(x, norm1_g[l], w_in[l], q_norm_g[l], k_norm_g[l], rpb[l], sgu_norm_g[l], w_spatial[l],
                   b_spatial[l], attn_out_g[l], sgu_out_g[l], w_out[l], norm2_g[l], w_router[l], b_router[l],
                   w_gate_up[l], b_gate_up[l], w_down[l], b_down[l])
    return x
```

```python
import functools

import jax
import jax.numpy as jnp
from jax import lax
from jax.experimental import pallas as pl
from jax.experimental.pallas import tpu as pltpu

GRID_W = 64
TOP_K = 4
SWIGLU_LIMIT = 7.0
SWIGLU_ALPHA = 1.702
EPS = 1e-6
LANES = 128
NEG_BIG = -1e30
VMEM_LIMIT_BYTES = 56 * 1024 * 1024

_F32 = jnp.float32
_BF16 = jnp.bfloat16


def _cparams(n_axes):
    return pltpu.CompilerParams(
        dimension_semantics=("arbitrary",) * n_axes,
        vmem_limit_bytes=VMEM_LIMIT_BYTES)


def _rms(x):
    return x * lax.rsqrt(jnp.mean(x * x, axis=-1, keepdims=True) + EPS)


def _gelu(x):
    return 0.5 * x * (1.0 + lax.erf(x * 0.7071067811865476))


def _norm_proj_kernel(x_ref, g_ref, w_ref, hg_ref, o_ref, h_ref, *, slab_out, n_norm_tiles, head_dim):
    j = pl.program_id(1)

    @pl.when(j == 0)
    def _():
        h_ref[...] = (_rms(x_ref[...]) * g_ref[...]).astype(_BF16)

    res = jnp.dot(h_ref[...], w_ref[...], preferred_element_type=_F32)
    if not slab_out:
        o_ref[...] = res
        return

    n_slabs = res.shape[1] // LANES
    row_first = jnp.where(lax.broadcasted_iota(jnp.int32, (LANES, LANES), 0) < head_dim, 1.0, 0.0)
    col_first = jnp.where(lax.broadcasted_iota(jnp.int32, (LANES, LANES), 1) < head_dim, 1.0, 0.0)
    same_head = (1.0 - jnp.abs(row_first - col_first)).astype(_BF16)
    for s in range(n_slabs):
        r = res[:, s * LANES:(s + 1) * LANES]

        @pl.when(j < n_norm_tiles)
        def _():
            ss = r * r
            hi = ss.astype(_BF16)
            lo = (ss - hi.astype(_F32)).astype(_BF16)
            head_ss = (jnp.dot(hi, same_head, preferred_element_type=_F32)
                       + jnp.dot(lo, same_head, preferred_element_type=_F32))
            o_ref[s] = r * lax.rsqrt(head_ss / head_dim + EPS) * hg_ref[0]

        @pl.when(j >= n_norm_tiles)
        def _():
            o_ref[s] = r


def _norm_proj(x, g, w_bf16, head_gain, *, slab_out, attn_width, head_dim):
    t, d = x.shape
    n = w_bf16.shape[1]
    tm = min(1024, t)
    tn = min(1024, attn_width if slab_out else n)
    assert t % tm == 0 and n % tn == 0 and tn % LANES == 0
    if slab_out:
        assert attn_width % tn == 0 and 2 * head_dim == LANES
        tiles_per_part = attn_width // tn
        out_shape = jax.ShapeDtypeStruct((n // LANES, t, LANES), _F32)
        out_spec = pl.BlockSpec((tn // LANES, tm, LANES), lambda i, j: (j, i, 0))
        hg_spec = pl.BlockSpec((1, 1, LANES), lambda i, j: (jnp.minimum(j // tiles_per_part, 2), 0, 0))
        n_norm_tiles = 2 * tiles_per_part
    else:
        out_shape = jax.ShapeDtypeStruct((t, n), _F32)
        out_spec = pl.BlockSpec((tm, tn), lambda i, j: (i, j))
        hg_spec = pl.BlockSpec((1, 1, LANES), lambda i, j: (0, 0, 0))
        n_norm_tiles = 0
    kern = functools.partial(_norm_proj_kernel, slab_out=slab_out, n_norm_tiles=n_norm_tiles, head_dim=head_dim)
    return pl.pallas_call(
        kern,
        grid=(t // tm, n // tn),
        in_specs=[
            pl.BlockSpec((tm, d), lambda i, j: (i, 0)),
            pl.BlockSpec((1, d), lambda i, j: (0, 0)),
            pl.BlockSpec((d, tn), lambda i, j: (0, j)),
            hg_spec,
        ],
        out_specs=out_spec,
        out_shape=out_shape,
        scratch_shapes=[pltpu.VMEM((tm, d), _BF16)],
        compiler_params=_cparams(2),
        name="norm_proj_qkv" if slab_out else "norm_proj_ug",
    )(x, g, w_bf16, head_gain)


def _attn_kernel(q_ref, kp_ref, kc_ref, kn_ref, vp_ref, vc_ref, vn_ref, bias_ref, o_ref, kwin, vwin,
                 *, rows, rblk, kh, head_dim, scale):
    rb = pl.program_id(2)
    blk = rblk * GRID_W
    for w, (kr, vr) in enumerate(((kp_ref, vp_ref), (kc_ref, vc_ref), (kn_ref, vn_ref))):
        kwin[w * blk:(w + 1) * blk, :] = kr[...].astype(_BF16)
        vwin[w * blk:(w + 1) * blk, :] = vr[...].astype(_BF16)
    lane = lax.broadcasted_iota(jnp.int32, (1, LANES), 1)
    left = lane < head_dim
    nk = (kh + 2) * GRID_W
    for pair in range(rblk // 2):
        r = rb * rblk + 2 * pair
        rs_a = jnp.clip(r - kh // 2, 0, rows - kh)
        rs_b = jnp.clip(r + 1 - kh // 2, 0, rows - kh)
        kind = (r - rs_a) + (rs_b - rs_a)
        off = pl.multiple_of((rs_a - (rb * rblk - rblk)) * GRID_W, GRID_W)
        ks = kwin[pl.ds(off, nk), :]
        vs = vwin[pl.ds(off, nk), :]
        q2 = q_ref[2 * pair * GRID_W:(2 * pair + 2) * GRID_W, :] * scale
        qa, qb = q2[:GRID_W], q2[GRID_W:]
        qm = jnp.concatenate([jnp.where(left, qa, 0.0), jnp.where(left, 0.0, qa),
                              jnp.where(left, qb, 0.0), jnp.where(left, 0.0, qb)], axis=0).astype(_BF16)
        s = lax.dot_general(qm, ks, (((1,), (1,)), ((), ())), preferred_element_type=_F32)
        s = s + bias_ref[kind]
        p = jnp.exp(s - jnp.max(s, axis=-1, keepdims=True))
        denom = jnp.sum(p, axis=-1, keepdims=True)
        pv = jnp.dot(p.astype(_BF16), vs, preferred_element_type=_F32) / denom
        o_ref[2 * pair * GRID_W:(2 * pair + 1) * GRID_W, :] = jnp.where(
            left, pv[0:GRID_W], pv[GRID_W:2 * GRID_W])
        o_ref[(2 * pair + 1) * GRID_W:(2 * pair + 2) * GRID_W, :] = jnp.where(
            left, pv[2 * GRID_W:3 * GRID_W], pv[3 * GRID_W:4 * GRID_W])


def _attention(qkv, bias_tab, *, batch, seq, n_heads, head_dim, kh):
    n_hp = n_heads // 2
    rows = seq // GRID_W
    rblk = kh
    assert rows % rblk == 0 and rows >= kh and rblk >= 4
    nrb = rows // rblk
    blk = rblk * GRID_W
    t = batch * seq

    def qmap(b, hp, rb):
        return (hp, b * nrb + rb, 0)

    def kvmap(part, delta):
        def f(b, hp, rb):
            return (part * n_hp + hp, b * nrb + jnp.clip(rb + delta, 0, nrb - 1), 0)
        return f

    slab = lambda m: pl.BlockSpec((None, blk, LANES), m)
    kern = functools.partial(_attn_kernel, rows=rows, rblk=rblk, kh=kh, head_dim=head_dim,
                             scale=float(head_dim) ** -0.5)
    return pl.pallas_call(
        kern,
        grid=(batch, n_hp, nrb),
        in_specs=[
            slab(qmap),
            slab(kvmap(1, -1)), slab(kvmap(1, 0)), slab(kvmap(1, 1)),
            slab(kvmap(2, -1)), slab(kvmap(2, 0)), slab(kvmap(2, 1)),
            pl.BlockSpec((kh, None, 4 * GRID_W, (kh + 2) * GRID_W), lambda b, hp, rb: (0, hp, 0, 0)),
        ],
        out_specs=pl.BlockSpec((blk, LANES), lambda b, hp, rb: (b * nrb + rb, hp)),
        out_shape=jax.ShapeDtypeStruct((t, n_hp * LANES), _F32),
        scratch_shapes=[pltpu.VMEM((3 * blk, LANES), _BF16), pltpu.VMEM((3 * blk, LANES), _BF16)],
        compiler_params=_cparams(3),
        name="nbr_attention",
    )(qkv, qkv, qkv, qkv, qkv, qkv, qkv, bias_tab)


def _attention_bias_table(rpb, *, kh, win_cols):
    n_heads, n_rpb_rows, _ = rpb.shape
    win_rows_max = (n_rpb_rows + 1) // 2
    qc = jnp.arange(GRID_W)[:, None]
    kc = jnp.arange(GRID_W)[None, :]
    col_start = jnp.clip(qc - win_cols // 2, 0, GRID_W - win_cols)
    valid = (kc >= col_start) & (kc < col_start + win_cols)
    padded = jnp.pad(rpb.astype(_F32), ((0, 0), (0, 0), (GRID_W, GRID_W)))
    base = GRID_W + win_cols - 1
    cols = jnp.stack([padded[:, :, base - q:base - q + GRID_W] for q in range(GRID_W)], axis=2)
    cols = jnp.where(valid[None, None], cols, NEG_BIG)
    masked = jnp.full((n_heads, GRID_W, GRID_W), NEG_BIG, _F32)
    nkr = kh + 2

    def member(shift, delta):
        tiles = [cols[:, jj - delta - shift + win_rows_max - 1] if 0 <= jj - delta < kh else masked
                 for jj in range(nkr)]
        return jnp.stack(tiles, axis=2).reshape(n_heads, GRID_W, nkr * GRID_W)

    kinds = []
    for kind in range(kh):
        shift_a, delta = (kind // 2) * 2, kind % 2
        both = jnp.stack([member(shift_a, 0), member(shift_a + 1 - delta, delta)], axis=0)
        both = both.reshape(2, n_heads // 2, 2, GRID_W, nkr * GRID_W).transpose(1, 0, 2, 3, 4)
        kinds.append(both.reshape(n_heads // 2, 4 * GRID_W, nkr * GRID_W))
    return jnp.stack(kinds, axis=0)


def _sgu_kernel(u_ref, g_ref, ng_ref, w_ref, b_ref, og_ref, o_ref, *, chunk, group_dim):
    tm, width = u_ref.shape
    gg = _gelu(g_ref[...])
    gn = (_rms(gg) * ng_ref[...]).astype(_BF16)
    lane = lax.broadcasted_iota(jnp.int32, (1, LANES), 1)
    left = lane < group_dim
    for c in range(tm // chunk):
        rs = slice(c * chunk, (c + 1) * chunk)
        for gp in range(width // LANES):
            cs = slice(gp * LANES, (gp + 1) * LANES)
            xg = gn[rs, cs]
            ma = jnp.dot(w_ref[2 * gp], xg, preferred_element_type=_F32)
            mb = jnp.dot(w_ref[2 * gp + 1], xg, preferred_element_type=_F32)
            mixed = jnp.where(left, ma, mb) + b_ref[:, cs]
            s = _gelu(u_ref[rs, cs]) * mixed
            o_ref[rs, cs] = s
    s_all = o_ref[...]
    o_ref[...] = _rms(s_all) * og_ref[...]


def _sgu(ug, norm_g, w_sp_bf16, b_full, out_g, *, width, chunk, group_dim):
    t = ug.shape[0]
    tm = min(512, t)
    assert t % tm == 0 and tm % chunk == 0 and width % LANES == 0 and 2 * group_dim == LANES
    n_groups = w_sp_bf16.shape[0]
    kern = functools.partial(_sgu_kernel, chunk=chunk, group_dim=group_dim)
    return pl.pallas_call(
        kern,
        grid=(t // tm,),
        in_specs=[
            pl.BlockSpec((tm, width), lambda i: (i, 0)),
            pl.BlockSpec((tm, width), lambda i: (i, 1)),
            pl.BlockSpec((1, width), lambda i: (0, 0)),
            pl.BlockSpec((n_groups, chunk, chunk), lambda i: (0, 0, 0)),
            pl.BlockSpec((chunk, width), lambda i: (0, 0)),
            pl.BlockSpec((1, width), lambda i: (0, 0)),
        ],
        out_specs=pl.BlockSpec((tm, width), lambda i: (i, 0)),
        out_shape=jax.ShapeDtypeStruct((t, width), _F32),
        compiler_params=_cparams(1),
        name="spatial_gating",
    )(ug, ug, norm_g, w_sp_bf16, b_full, out_g)


def _out_router_kernel(a_ref, s_ref, x_ref, ag_ref, w_ref, n2g_ref, wr_ref, br_ref,
                       x2_ref, xn_ref, idx_ref, gate_ref):
    an = _rms(a_ref[...]) * ag_ref[...]
    mix = jnp.concatenate([an, s_ref[...]], axis=-1).astype(_BF16)
    x2 = x_ref[...] + jnp.dot(mix, w_ref[...], preferred_element_type=_F32)
    x2_ref[...] = x2
    xn = _rms(x2) * n2g_ref[...]
    xn_ref[...] = xn
    xh = xn.astype(_BF16)
    xl = (xn - xh.astype(_F32)).astype(_BF16)
    parts = jnp.dot(jnp.concatenate([xh, xl], axis=-1), wr_ref[...], preferred_element_type=_F32)
    logits = parts + pltpu.roll(parts, LANES // 2, axis=1) + br_ref[...]
    tm = logits.shape[0]
    lane = lax.broadcasted_iota(jnp.int32, (tm, LANES), 1)
    lane_f = lane.astype(_F32)
    idx_acc = jnp.zeros((tm, LANES), _F32)
    vals = []
    cur = logits
    for k in range(TOP_K):
        m = jnp.max(cur, axis=-1, keepdims=True)
        ik = jnp.min(jnp.where(cur == m, lane_f, float(LANES)), axis=-1, keepdims=True)
        vals.append(m)
        idx_acc = jnp.where(lane == k, ik, idx_acc)
        cur = jnp.where(lane_f == ik, -jnp.inf, cur)
    exps = [jnp.exp(v - vals[0]) for v in vals]
    denom = exps[0]
    for e in exps[1:]:
        denom = denom + e
    gate_acc = jnp.zeros((tm, LANES), _F32)
    for k in range(TOP_K):
        gate_acc = jnp.where(lane == k, exps[k] / denom, gate_acc)
    idx_ref[...] = idx_acc.astype(jnp.int32)
    gate_ref[...] = gate_acc


def _out_router(a_out, s_out, x, attn_out_g, w_out_bf16, norm2_g, w_router_pad, b_router_pad):
    t, d = x.shape
    wa = a_out.shape[1]
    ws = s_out.shape[1]
    tm = min(512, t)
    assert t % tm == 0
    row = lambda n: pl.BlockSpec((tm, n), lambda i: (i, 0))
    full = lambda a, b: pl.BlockSpec((a, b), lambda i: (0, 0))
    return pl.pallas_call(
        _out_router_kernel,
        grid=(t // tm,),
        in_specs=[row(wa), row(ws), row(d), full(1, wa), full(wa + ws, d), full(1, d),
                  full(2 * d, LANES), full(1, LANES)],
        out_specs=[row(d), row(d), row(LANES), row(LANES)],
        out_shape=[jax.ShapeDtypeStruct((t, d), _F32), jax.ShapeDtypeStruct((t, d), _F32),
                   jax.ShapeDtypeStruct((t, LANES), jnp.int32), jax.ShapeDtypeStruct((t, LANES), _F32)],
        compiler_params=_cparams(1),
        name="out_proj_router",
    )(a_out, s_out, x, attn_out_g, w_out_bf16, norm2_g, w_router_pad, b_router_pad)


def _membership(idx):
    lane = lax.broadcasted_iota(jnp.int32, idx.shape, 1)
    return [lane == idx[:, k:k + 1] for k in range(TOP_K)]


def _count_kernel(idx_ref, cnt_ref):
    @pl.when(pl.program_id(0) == 0)
    def _():
        cnt_ref[...] = jnp.zeros_like(cnt_ref)

    member = sum(oh.astype(_F32) for oh in _membership(idx_ref[...]))
    cnt_ref[...] += jnp.sum(member, axis=0, keepdims=True)


def _slot_kernel(idx_ref, start_ref, dst_ref, carry_ref):
    @pl.when(pl.program_id(0) == 0)
    def _():
        carry_ref[...] = jnp.zeros_like(carry_ref)

    onehots = _membership(idx_ref[...])
    member = sum(oh.astype(_F32) for oh in onehots)
    tb = member.shape[0]
    earlier = (lax.broadcasted_iota(jnp.int32, (tb, tb), 0) > lax.broadcasted_iota(jnp.int32, (tb, tb), 1))
    before = jnp.dot(earlier.astype(_BF16), member.astype(_BF16), preferred_element_type=_F32)
    slot_e = start_ref[...] + carry_ref[...] + before
    lane = lax.broadcasted_iota(jnp.int32, member.shape, 1)
    dst = jnp.zeros(member.shape, _F32)
    for k in range(TOP_K):
        dk = jnp.sum(jnp.where(onehots[k], slot_e, 0.0), axis=-1, keepdims=True)
        dst = jnp.where(lane == k, dk, dst)
    dst_ref[...] = dst.astype(jnp.int32)
    carry_ref[...] += jnp.sum(member, axis=0, keepdims=True)


def _route_counts(idx_pad):
    t = idx_pad.shape[0]
    tb = min(512, t)
    return pl.pallas_call(
        _count_kernel,
        grid=(t // tb,),
        in_specs=[pl.BlockSpec((tb, LANES), lambda i: (i, 0))],
        out_specs=pl.BlockSpec((1, LANES), lambda i: (0, 0)),
        out_shape=jax.ShapeDtypeStruct((1, LANES), _F32),
        compiler_params=_cparams(1),
        name="route_counts",
    )(idx_pad)


def _route_slots(idx_pad, starts_f32):
    t = idx_pad.shape[0]
    tb = min(512, t)
    return pl.pallas_call(
        _slot_kernel,
        grid=(t // tb,),
        in_specs=[pl.BlockSpec((tb, LANES), lambda i: (i, 0)), pl.BlockSpec((1, LANES), lambda i: (0, 0))],
        out_specs=pl.BlockSpec((tb, LANES), lambda i: (i, 0)),
        out_shape=jax.ShapeDtypeStruct((t, LANES), jnp.int32),
        scratch_shapes=[pltpu.VMEM((1, LANES), _F32)],
        compiler_params=_cparams(1),
        name="route_slots",
    )(idx_pad, starts_f32)


ROUTE_TB = 256


ZERO_ROWS = 256


def _scatter_rows_kernel(pad_beg_ref, pad_mid_ref, pad_end_ref, dst_ref, xn_ref, xs_ref, zero_ref, sem, zsem):
    tb = xn_ref.shape[0]
    first = pl.program_id(0) == 0
    n_regions = pad_beg_ref.shape[0]

    def issue(t, c):
        for k in range(TOP_K):
            d = dst_ref[t * TOP_K + k]
            pltpu.make_async_copy(xn_ref.at[pl.ds(t, 1)], xs_ref.at[pl.ds(d, 1)], sem).start(priority=k % 2)
        return c

    def drain():
        for _ in range(TOP_K):
            pltpu.make_async_copy(xn_ref, xs_ref.at[pl.ds(0, tb)], sem).wait()

    def clear_padding(start):
        def region(e, c):
            beg, mid, end = pad_beg_ref[e], pad_mid_ref[e], pad_end_ref[e]

            def one_row(r, c2):
                cp = pltpu.make_async_copy(zero_ref.at[pl.ds(0, 1)], xs_ref.at[pl.ds(r, 1)], zsem)
                cp.start() if start else cp.wait()
                return c2

            def one_block(b, c2):
                r0 = pl.multiple_of(mid + b * ZERO_ROWS, ZERO_ROWS)
                cp = pltpu.make_async_copy(zero_ref, xs_ref.at[pl.ds(r0, ZERO_ROWS)], zsem)
                cp.start() if start else cp.wait()
                return c2

            lax.fori_loop(beg, mid, one_row, 0)
            lax.fori_loop(0, (end - mid) // ZERO_ROWS, one_block, 0)
            return c

        lax.fori_loop(0, n_regions, region, 0)

    @pl.when(first)
    def _():
        zero_ref[...] = jnp.zeros_like(zero_ref)
        clear_padding(True)

    lax.fori_loop(0, tb, issue, 0, unroll=4)
    drain()

    @pl.when(first)
    def _():
        clear_padding(False)


def _scatter_rows(pad_beg, pad_mid, pad_end, dst_flat, xn, n_slots):
    t, d = xn.shape
    tb = min(ROUTE_TB, t)
    grid_spec = pltpu.PrefetchScalarGridSpec(
        num_scalar_prefetch=3,
        grid=(t // tb,),
        in_specs=[
            pl.BlockSpec((tb * TOP_K,), lambda i, *_: (i,), memory_space=pltpu.SMEM),
            pl.BlockSpec((tb, d), lambda i, *_: (i, 0)),
        ],
        out_specs=pl.BlockSpec(memory_space=pl.ANY),
        scratch_shapes=[pltpu.VMEM((ZERO_ROWS, d), _F32), pltpu.SemaphoreType.DMA(()),
                        pltpu.SemaphoreType.DMA(())],
    )
    return pl.pallas_call(
        _scatter_rows_kernel,
        grid_spec=grid_spec,
        out_shape=jax.ShapeDtypeStruct((n_slots, d), _F32),
        compiler_params=_cparams(1),
        name="scatter_rows",
    )(pad_beg, pad_mid, pad_end, dst_flat, xn)


def _combine_kernel(dst_ref, gate_ref, x2_ref, ys_ref, o_ref, rows_ref, sem):
    tb = x2_ref.shape[0]

    def issue(t, c):
        for k in range(TOP_K):
            d = dst_ref[t * TOP_K + k]
            pltpu.make_async_copy(ys_ref.at[pl.ds(d, 1)], rows_ref.at[k, pl.ds(t, 1)], sem).start(priority=k % 2)
        return c

    lax.fori_loop(0, tb, issue, 0, unroll=4)
    for k in range(TOP_K):
        pltpu.make_async_copy(ys_ref.at[pl.ds(0, tb)], rows_ref.at[k], sem).wait()
    acc = x2_ref[...]
    gates = gate_ref[...]
    for k in range(TOP_K):
        acc = acc + gates[:, k:k + 1] * rows_ref[k]
    o_ref[...] = acc


def _combine(dst_flat, gates_pad, x2, ys):
    t, d = x2.shape
    tb = min(ROUTE_TB, t)
    return pl.pallas_call(
        _combine_kernel,
        grid=(t // tb,),
        in_specs=[
            pl.BlockSpec((tb * TOP_K,), lambda i: (i,), memory_space=pltpu.SMEM),
            pl.BlockSpec((tb, LANES), lambda i: (i, 0)),
            pl.BlockSpec((tb, d), lambda i: (i, 0)),
            pl.BlockSpec(memory_space=pl.ANY),
        ],
        out_specs=pl.BlockSpec((tb, d), lambda i: (i, 0)),
        out_shape=jax.ShapeDtypeStruct((t, d), _F32),
        scratch_shapes=[pltpu.VMEM((TOP_K, tb, d), _F32), pltpu.SemaphoreType.DMA(())],
        compiler_params=_cparams(1),
        name="combine_rows",
    )(dst_flat, gates_pad, x2, ys)


def _ffn_kernel(te_ref, tr_ref, tblk_ref, x_ref, wg_ref, wl_ref, bg_ref, bl_ref, wd_ref, bd_ref,
                o_ref, xb_ref, *, sub):
    del te_ref, tblk_ref
    i = pl.program_id(0)
    j = pl.program_id(1)
    n_rows = tr_ref[i]
    tm, d = o_ref.shape

    @pl.when(jnp.logical_and(j == 0, n_rows > 0))
    def _():
        o_ref[...] = jnp.broadcast_to(bd_ref[...], (tm, d))

    @pl.when(jnp.logical_and(j == 0, n_rows == 0))
    def _():
        o_ref[...] = jnp.zeros((tm, d), _F32)

    def sub_block(s, c):
        r0 = pl.multiple_of(s * sub, sub)

        @pl.when(j == 0)
        def _():
            xb_ref[pl.ds(r0, sub), :] = x_ref[pl.ds(r0, sub), :].astype(_BF16)

        xb = xb_ref[pl.ds(r0, sub), :]
        hg = jnp.dot(xb, wg_ref[...], preferred_element_type=_F32) + bg_ref[...]
        hl = jnp.dot(xb, wl_ref[...], preferred_element_type=_F32) + bl_ref[...]
        gate = jnp.minimum(hg, SWIGLU_LIMIT)
        lin = jnp.clip(hl, -SWIGLU_LIMIT, SWIGLU_LIMIT)
        act = gate * (1.0 / (1.0 + jnp.exp(-SWIGLU_ALPHA * gate))) * (lin + 1.0)
        o_ref[pl.ds(r0, sub), :] += jnp.dot(act.astype(_BF16), wd_ref[...], preferred_element_type=_F32)
        return c

    lax.fori_loop(0, (n_rows + sub - 1) // sub, sub_block, 0)


def _expert_ffn(tile_expert, tile_rows, tile_blk, xs, w_gu_bf16, b_gu, w_d_bf16, b_d, *, tm, tn, sub):
    n_slots, d = xs.shape
    n_exp, _, de2 = w_gu_bf16.shape
    de = de2 // 2
    nj = de // tn
    n_tiles = n_slots // tm
    assert de % tn == 0 and n_slots % tm == 0 and tm % sub == 0
    grid_spec = pltpu.PrefetchScalarGridSpec(
        num_scalar_prefetch=3,
        grid=(n_tiles, nj),
        in_specs=[
            pl.BlockSpec((tm, d), lambda i, j, te, tr, tb: (tb[i], 0)),
            pl.BlockSpec((None, d, tn), lambda i, j, te, tr, tb: (te[i], 0, j)),
            pl.BlockSpec((None, d, tn), lambda i, j, te, tr, tb: (te[i], 0, nj + j)),
            pl.BlockSpec((None, 1, tn), lambda i, j, te, tr, tb: (te[i], 0, j)),
            pl.BlockSpec((None, 1, tn), lambda i, j, te, tr, tb: (te[i], 0, nj + j)),
            pl.BlockSpec((None, tn, d), lambda i, j, te, tr, tb: (te[i], j, 0)),
            pl.BlockSpec((None, 1, d), lambda i, j, te, tr, tb: (te[i], 0, 0)),
        ],
        out_specs=pl.BlockSpec((tm, d), lambda i, j, te, tr, tb: (i, 0)),
        scratch_shapes=[pltpu.VMEM((tm, d), _BF16)],
    )
    return pl.pallas_call(
        functools.partial(_ffn_kernel, sub=sub),
        grid_spec=grid_spec,
        out_shape=jax.ShapeDtypeStruct((n_slots, d), _F32),
        compiler_params=_cparams(2),
        name="expert_ffn",
    )(tile_expert, tile_rows, tile_blk, xs, w_gu_bf16, w_gu_bf16,
      b_gu.reshape(n_exp, 1, de2), b_gu.reshape(n_exp, 1, de2), w_d_bf16, b_d.reshape(n_exp, 1, d))


def _tile_plan(counts, *, tm, n_tiles):
    n_exp = counts.shape[0]
    tiles_e = (counts + tm - 1) // tm
    tile_end = jnp.cumsum(tiles_e)
    tile_beg = tile_end - tiles_e
    starts = tile_beg * tm
    n_active = tile_end[-1]
    tid = jnp.minimum(jnp.arange(n_tiles, dtype=jnp.int32), n_active - 1)
    expert = jnp.minimum(jnp.sum(tile_end[None, :] <= tid[:, None], axis=1), n_exp - 1).astype(jnp.int32)
    rows = jnp.clip(counts[expert] - (tid - tile_beg[expert]) * tm, 0, tm)
    rows = jnp.where(jnp.arange(n_tiles) < n_active, rows, 0).astype(jnp.int32)
    pad_beg = jnp.concatenate([starts + counts, (n_active * tm)[None]])
    pad_end = jnp.concatenate([tile_end * tm, jnp.full((1,), n_tiles * tm, jnp.int32)])
    pad_mid = jnp.minimum((pad_beg + ZERO_ROWS - 1) // ZERO_ROWS * ZERO_ROWS, pad_end)
    pads = tuple(p.astype(jnp.int32) for p in (pad_beg, pad_mid, pad_end))
    return starts, expert, rows, tid.astype(jnp.int32), pads


def _layer(x, norm1_g, w_in, q_norm_g, k_norm_g, rpb, sgu_norm_g, w_spatial, b_spatial, attn_out_g,
           sgu_out_g, w_out, norm2_g, w_router, b_router, w_gate_up, b_gate_up, w_down, b_down):
    batch, seq, d = x.shape
    t = batch * seq
    head_dim = q_norm_g.shape[-1]
    n_heads = rpb.shape[0]
    attn_width = n_heads * head_dim
    n_groups, chunk, _ = w_spatial.shape
    sgu_width = sgu_norm_g.shape[-1]
    group_dim = sgu_width // n_groups
    win_rows_max = (rpb.shape[1] + 1) // 2
    win_cols = (rpb.shape[2] + 1) // 2
    rows = seq // GRID_W
    kh = min(win_rows_max, rows)
    n_exp = w_router.shape[-1]
    assert TOP_K <= n_exp <= LANES // 2 and seq % GRID_W == 0 and w_in.shape[1] == 3 * attn_width + 2 * sgu_width

    xf = x.reshape(t, d)
    row = lambda v: v.reshape(1, -1).astype(_F32)

    w_in_b = w_in.astype(_BF16)
    pair = lambda g: jnp.concatenate([g, g]).reshape(1, 1, LANES)
    head_gain = jnp.concatenate([pair(q_norm_g), pair(k_norm_g), jnp.ones((1, 1, LANES), _F32)], axis=0)
    qkv = _norm_proj(xf, row(norm1_g), w_in_b[:, :3 * attn_width], head_gain,
                     slab_out=True, attn_width=attn_width, head_dim=head_dim)
    ug = _norm_proj(xf, row(norm1_g), w_in_b[:, 3 * attn_width:], head_gain,
                    slab_out=False, attn_width=attn_width, head_dim=head_dim)

    bias_tab = _attention_bias_table(rpb, kh=kh, win_cols=win_cols)
    a_out = _attention(qkv, bias_tab, batch=batch, seq=seq, n_heads=n_heads, head_dim=head_dim, kh=kh)

    b_full = jnp.repeat(b_spatial.T, group_dim, axis=1)
    s_out = _sgu(ug, row(sgu_norm_g), w_spatial.astype(_BF16), b_full, row(sgu_out_g),
                 width=sgu_width, chunk=chunk, group_dim=group_dim)

    half = LANES // 2
    w_r_hi = w_router.astype(_BF16)
    w_r_lo = (w_router - w_r_hi.astype(_F32)).astype(_BF16)
    blank = jnp.zeros((d, LANES), _BF16)
    w_router_pad = jnp.concatenate([blank.at[:, :n_exp].set(w_r_hi).at[:, half:half + n_exp].set(w_r_lo),
                                    blank.at[:, :n_exp].set(w_r_hi)], axis=0)
    b_router_pad = jnp.full((1, LANES), NEG_BIG, _F32).at[0, :n_exp].set(b_router)
    x2, xn, idx_pad, gates_pad = _out_router(a_out, s_out, xf, row(attn_out_g), w_out.astype(_BF16),
                                             row(norm2_g), w_router_pad, b_router_pad)

    de = w_down.shape[1]
    tm = min(1024, t)
    tn = min(512, de)
    sub = min(256, tm)
    n_tiles = (t * TOP_K) // tm + n_exp
    counts = _route_counts(idx_pad)[0, :n_exp].astype(jnp.int32)
    assert tm % ZERO_ROWS == 0
    starts, tile_expert, tile_rows, tile_blk, pads = _tile_plan(counts, tm=tm, n_tiles=n_tiles)
    starts_pad = jnp.zeros((1, LANES), _F32).at[0, :n_exp].set(starts.astype(_F32))
    dst_pad = _route_slots(idx_pad, starts_pad)
    dst_flat = dst_pad[:, :TOP_K].reshape(-1)

    xs = _scatter_rows(*pads, dst_flat, xn, n_tiles * tm)
    ys = _expert_ffn(tile_expert, tile_rows, tile_blk, xs, w_gate_up.astype(_BF16), b_gate_up,
                     w_down.astype(_BF16), b_down, tm=tm, tn=tn, sub=sub)
    out = _combine(dst_flat, gates_pad, x2, ys)
    return out.reshape(batch, seq, d)


def kernel(x, norm1_g, w_in, q_norm_g, k_norm_g, rpb, sgu_norm_g, w_spatial, b_spatial, attn_out_g,
           sgu_out_g, w_out, norm2_g, w_router, b_router, w_gate_up, b_gate_up, w_down, b_down):
    depth = norm1_g.shape[0]
    for l in range(depth):
        x = _layer(x, norm1_g[l], w_in[l], q_norm_g[l], k_norm_g[l], rpb[l], sgu_norm_g[l], w_spatial[l],
                   b_spatial[l], attn_out_g[l], sgu_out_g[l], w_out[l], norm2_g[l], w_router[l], b_router[l],
                   w_gate_up[l], b_gate_up[l], w_down[l], b_down[l])
    return x
```

```python
import functools

import jax
import jax.numpy as jnp
from jax import lax
from jax.experimental import pallas as pl
from jax.experimental.pallas import tpu as pltpu

GRID_W = 64
TOP_K = 4
SWIGLU_LIMIT = 7.0
SWIGLU_ALPHA = 1.702
EPS = 1e-6
LANES = 128
NEG_BIG = -1e30
VMEM_LIMIT_BYTES = 56 * 1024 * 1024

_F32 = jnp.float32
_BF16 = jnp.bfloat16


def _cparams(n_axes):
    return pltpu.CompilerParams(
        dimension_semantics=("arbitrary",) * n_axes,
        vmem_limit_bytes=VMEM_LIMIT_BYTES)


def _rms(x):
    return x * lax.rsqrt(jnp.mean(x * x, axis=-1, keepdims=True) + EPS)


def _gelu(x):
    return 0.5 * x * (1.0 + lax.erf(x * 0.7071067811865476))


def _norm_proj_kernel(x_ref, g_ref, w_ref, hg_ref, o_ref, h_ref, *, slab_out, n_norm_tiles, head_dim):
    j = pl.program_id(1)

    @pl.when(j == 0)
    def _():
        h_ref[...] = (_rms(x_ref[...]) * g_ref[...]).astype(_BF16)

    if not slab_out:
        o_ref[...] = jnp.dot(h_ref[...], w_ref[...], preferred_element_type=_F32)
        return

    left = lax.broadcasted_iota(jnp.int32, (1, LANES), 1) < head_dim
    normed = j < n_norm_tiles
    tn = w_ref.shape[1]
    piece = min(2 * LANES, tn)
    for p in range(tn // piece):
        res = jnp.dot(h_ref[...], w_ref[:, p * piece:(p + 1) * piece], preferred_element_type=_F32)
        for s in range(piece // LANES):
            r = res[:, s * LANES:(s + 1) * LANES]
            ss = r * r
            ss_first = jnp.sum(jnp.where(left, ss, 0.0), axis=-1, keepdims=True)
            ss_second = jnp.sum(jnp.where(left, 0.0, ss), axis=-1, keepdims=True)
            inv = jnp.where(left, lax.rsqrt(ss_first / head_dim + EPS),
                            lax.rsqrt(ss_second / head_dim + EPS)) * hg_ref[0]
            o_ref[p * (piece // LANES) + s] = r * jnp.where(normed, inv, 1.0)


def _norm_proj(x, g, w_bf16, head_gain, *, slab_out, attn_width, head_dim):
    t, d = x.shape
    n = w_bf16.shape[1]
    tm = min(1024, t)
    tn = min(1024, attn_width if slab_out else n)
    assert t % tm == 0 and n % tn == 0 and tn % LANES == 0
    if slab_out:
        assert attn_width % tn == 0 and 2 * head_dim == LANES
        tiles_per_part = attn_width // tn
        out_shape = jax.ShapeDtypeStruct((n // LANES, t, LANES), _F32)
        out_spec = pl.BlockSpec((tn // LANES, tm, LANES), lambda i, j: (j, i, 0))
        hg_spec = pl.BlockSpec((1, 1, LANES), lambda i, j: (jnp.minimum(j // tiles_per_part, 2), 0, 0))
        n_norm_tiles = 2 * tiles_per_part
    else:
        out_shape = jax.ShapeDtypeStruct((t, n), _F32)
        out_spec = pl.BlockSpec((tm, tn), lambda i, j: (i, j))
        hg_spec = pl.BlockSpec((1, 1, LANES), lambda i, j: (0, 0, 0))
        n_norm_tiles = 0
    kern = functools.partial(_norm_proj_kernel, slab_out=slab_out, n_norm_tiles=n_norm_tiles, head_dim=head_dim)
    return pl.pallas_call(
        kern,
        grid=(t // tm, n // tn),
        in_specs=[
            pl.BlockSpec((tm, d), lambda i, j: (i, 0)),
            pl.BlockSpec((1, d), lambda i, j: (0, 0)),
            pl.BlockSpec((d, tn), lambda i, j: (0, j)),
            hg_spec,
        ],
        out_specs=out_spec,
        out_shape=out_shape,
        scratch_shapes=[pltpu.VMEM((tm, d), _BF16)],
        compiler_params=_cparams(2),
        name="norm_proj_qkv" if slab_out else "norm_proj_ug",
    )(x, g, w_bf16, head_gain)


def _attn_kernel(q_ref, kp_ref, kc_ref, kn_ref, vp_ref, vc_ref, vn_ref, bias_ref, o_ref, kwin, vwin,
                 *, rows, rblk, kh, head_dim, scale):
    rb = pl.program_id(2)
    blk = rblk * GRID_W
    for w, (kr, vr) in enumerate(((kp_ref, vp_ref), (kc_ref, vc_ref), (kn_ref, vn_ref))):
        kwin[w * blk:(w + 1) * blk, :] = kr[...].astype(_BF16)
        vwin[w * blk:(w + 1) * blk, :] = vr[...].astype(_BF16)
    lane = lax.broadcasted_iota(jnp.int32, (1, LANES), 1)
    left = lane < head_dim
    nk = (kh + 2) * GRID_W
    for pair in range(rblk // 2):
        r = rb * rblk + 2 * pair
        rs_a = jnp.clip(r - kh // 2, 0, rows - kh)
        rs_b = jnp.clip(r + 1 - kh // 2, 0, rows - kh)
        kind = (r - rs_a) + (rs_b - rs_a)
        off = pl.multiple_of((rs_a - (rb * rblk - rblk)) * GRID_W, GRID_W)
        ks = kwin[pl.ds(off, nk), :]
        vs = vwin[pl.ds(off, nk), :]
        q2 = q_ref[2 * pair * GRID_W:(2 * pair + 2) * GRID_W, :] * scale
        qa, qb = q2[:GRID_W], q2[GRID_W:]
        qm = jnp.concatenate([jnp.where(left, qa, 0.0), jnp.where(left, 0.0, qa),
                              jnp.where(left, qb, 0.0), jnp.where(left, 0.0, qb)], axis=0).astype(_BF16)
        s = lax.dot_general(qm, ks, (((1,), (1,)), ((), ())), preferred_element_type=_F32)
        s = s + bias_ref[kind]
        p = jnp.exp(s - jnp.max(s, axis=-1, keepdims=True))
        denom = jnp.sum(p, axis=-1, keepdims=True)
        pv = jnp.dot(p.astype(_BF16), vs, preferred_element_type=_F32) / denom
        o_ref[2 * pair * GRID_W:(2 * pair + 1) * GRID_W, :] = jnp.where(
            left, pv[0:GRID_W], pv[GRID_W:2 * GRID_W])
        o_ref[(2 * pair + 1) * GRID_W:(2 * pair + 2) * GRID_W, :] = jnp.where(
            left, pv[2 * GRID_W:3 * GRID_W], pv[3 * GRID_W:4 * GRID_W])


def _attention(qkv, bias_tab, *, batch, seq, n_heads, head_dim, kh):
    n_hp = n_heads // 2
    rows = seq // GRID_W
    rblk = kh
    assert rows % rblk == 0 and rows >= kh and rblk >= 4
    nrb = rows // rblk
    blk = rblk * GRID_W
    t = batch * seq

    def qmap(b, hp, rb):
        return (hp, b * nrb + rb, 0)

    def kvmap(part, delta):
        def f(b, hp, rb):
            return (part * n_hp + hp, b * nrb + jnp.clip(rb + delta, 0, nrb - 1), 0)
        return f

    slab = lambda m: pl.BlockSpec((None, blk, LANES), m)
    kern = functools.partial(_attn_kernel, rows=rows, rblk=rblk, kh=kh, head_dim=head_dim,
                             scale=float(head_dim) ** -0.5)
    return pl.pallas_call(
        kern,
        grid=(batch, n_hp, nrb),
        in_specs=[
            slab(qmap),
            slab(kvmap(1, -1)), slab(kvmap(1, 0)), slab(kvmap(1, 1)),
            slab(kvmap(2, -1)), slab(kvmap(2, 0)), slab(kvmap(2, 1)),
            pl.BlockSpec((kh, None, 4 * GRID_W, (kh + 2) * GRID_W), lambda b, hp, rb: (0, hp, 0, 0)),
        ],
        out_specs=pl.BlockSpec((blk, LANES), lambda b, hp, rb: (b * nrb + rb, hp)),
        out_shape=jax.ShapeDtypeStruct((t, n_hp * LANES), _F32),
        scratch_shapes=[pltpu.VMEM((3 * blk, LANES), _BF16), pltpu.VMEM((3 * blk, LANES), _BF16)],
        compiler_params=_cparams(3),
        name="nbr_attention",
    )(qkv, qkv, qkv, qkv, qkv, qkv, qkv, bias_tab)


def _attention_bias_table(rpb, *, kh, win_cols):
    n_heads, n_rpb_rows, _ = rpb.shape
    win_rows_max = (n_rpb_rows + 1) // 2
    qc = jnp.arange(GRID_W)[:, None]
    kc = jnp.arange(GRID_W)[None, :]
    col_start = jnp.clip(qc - win_cols // 2, 0, GRID_W - win_cols)
    valid = (kc >= col_start) & (kc < col_start + win_cols)
    padded = jnp.pad(rpb.astype(_F32), ((0, 0), (0, 0), (GRID_W, GRID_W)))
    base = GRID_W + win_cols - 1
    cols = jnp.stack([padded[:, :, base - q:base - q + GRID_W] for q in range(GRID_W)], axis=2)
    cols = jnp.where(valid[None, None], cols, NEG_BIG)
    masked = jnp.full((n_heads, GRID_W, GRID_W), NEG_BIG, _F32)
    nkr = kh + 2

    def member(shift, delta):
        tiles = [cols[:, jj - delta - shift + win_rows_max - 1] if 0 <= jj - delta < kh else masked
                 for jj in range(nkr)]
        return jnp.stack(tiles, axis=2).reshape(n_heads, GRID_W, nkr * GRID_W)

    kinds = []
    for kind in range(kh):
        shift_a, delta = (kind // 2) * 2, kind % 2
        both = jnp.stack([member(shift_a, 0), member(shift_a + 1 - delta, delta)], axis=0)
        both = both.reshape(2, n_heads // 2, 2, GRID_W, nkr * GRID_W).transpose(1, 0, 2, 3, 4)
        kinds.append(both.reshape(n_heads // 2, 4 * GRID_W, nkr * GRID_W))
    return jnp.stack(kinds, axis=0)


def _sgu_kernel(u_ref, g_ref, ng_ref, w_ref, b_ref, og_ref, o_ref, *, chunk, group_dim):
    tm, width = u_ref.shape
    gg = _gelu(g_ref[...])
    gn = (_rms(gg) * ng_ref[...]).astype(_BF16)
    lane = lax.broadcasted_iota(jnp.int32, (1, LANES), 1)
    left = lane < group_dim
    for c in range(tm // chunk):
        rs = slice(c * chunk, (c + 1) * chunk)
        for gp in range(width // LANES):
            cs = slice(gp * LANES, (gp + 1) * LANES)
            xg = gn[rs, cs]
            ma = jnp.dot(w_ref[2 * gp], xg, preferred_element_type=_F32)
            mb = jnp.dot(w_ref[2 * gp + 1], xg, preferred_element_type=_F32)
            mixed = jnp.where(left, ma, mb) + b_ref[:, cs]
            s = _gelu(u_ref[rs, cs]) * mixed
            o_ref[rs, cs] = s
    s_all = o_ref[...]
    o_ref[...] = _rms(s_all) * og_ref[...]


def _sgu(ug, norm_g, w_sp_bf16, b_full, out_g, *, width, chunk, group_dim):
    t = ug.shape[0]
    tm = min(512, t)
    assert t % tm == 0 and tm % chunk == 0 and width % LANES == 0 and 2 * group_dim == LANES
    n_groups = w_sp_bf16.shape[0]
    kern = functools.partial(_sgu_kernel, chunk=chunk, group_dim=group_dim)
    return pl.pallas_call(
        kern,
        grid=(t // tm,),
        in_specs=[
            pl.BlockSpec((tm, width), lambda i: (i, 0)),
            pl.BlockSpec((tm, width), lambda i: (i, 1)),
            pl.BlockSpec((1, width), lambda i: (0, 0)),
            pl.BlockSpec((n_groups, chunk, chunk), lambda i: (0, 0, 0)),
            pl.BlockSpec((chunk, width), lambda i: (0, 0)),
            pl.BlockSpec((1, width), lambda i: (0, 0)),
        ],
        out_specs=pl.BlockSpec((tm, width), lambda i: (i, 0)),
        out_shape=jax.ShapeDtypeStruct((t, width), _F32),
        compiler_params=_cparams(1),
        name="spatial_gating",
    )(ug, ug, norm_g, w_sp_bf16, b_full, out_g)


ROUTER_ROW_CHUNK = 128


def _out_router_kernel(a_ref, s_ref, x_ref, ag_ref, w_ref, n2g_ref, wr_ref, br_ref,
                       x2_ref, xn_ref, idx_ref, gate_ref):
    for c in range(x_ref.shape[0] // ROUTER_ROW_CHUNK):
        rs = slice(c * ROUTER_ROW_CHUNK, (c + 1) * ROUTER_ROW_CHUNK)
        an = _rms(a_ref[rs, :]) * ag_ref[...]
        mix = jnp.concatenate([an, s_ref[rs, :]], axis=-1).astype(_BF16)
        x2 = x_ref[rs, :] + jnp.dot(mix, w_ref[...], preferred_element_type=_F32)
        x2_ref[rs, :] = x2
        xn = _rms(x2) * n2g_ref[...]
        xn_ref[rs, :] = xn
        xh = xn.astype(_BF16)
        xl = (xn - xh.astype(_F32)).astype(_BF16)
        parts = jnp.dot(jnp.concatenate([xh, xl], axis=-1), wr_ref[...], preferred_element_type=_F32)
        logits = parts + pltpu.roll(parts, LANES // 2, axis=1) + br_ref[...]
        rc = logits.shape[0]
        lane = lax.broadcasted_iota(jnp.int32, (rc, LANES), 1)
        lane_f = lane.astype(_F32)
        idx_acc = jnp.zeros((rc, LANES), _F32)
        vals = []
        cur = logits
        for k in range(TOP_K):
            m = jnp.max(cur, axis=-1, keepdims=True)
            ik = jnp.min(jnp.where(cur == m, lane_f, float(LANES)), axis=-1, keepdims=True)
            vals.append(m)
            idx_acc = jnp.where(lane == k, ik, idx_acc)
            cur = jnp.where(lane_f == ik, -jnp.inf, cur)
        exps = [jnp.exp(v - vals[0]) for v in vals]
        denom = exps[0]
        for e in exps[1:]:
            denom = denom + e
        gate_acc = jnp.zeros((rc, LANES), _F32)
        for k in range(TOP_K):
            gate_acc = jnp.where(lane == k, exps[k] / denom, gate_acc)
        idx_ref[rs, :] = idx_acc.astype(jnp.int32)
        gate_ref[rs, :] = gate_acc


def _out_router(a_out, s_out, x, attn_out_g, w_out_bf16, norm2_g, w_router_pad, b_router_pad):
    t, d = x.shape
    wa = a_out.shape[1]
    ws = s_out.shape[1]
    tm = min(512, t)
    assert t % tm == 0
    row = lambda n: pl.BlockSpec((tm, n), lambda i: (i, 0))
    full = lambda a, b: pl.BlockSpec((a, b), lambda i: (0, 0))
    return pl.pallas_call(
        _out_router_kernel,
        grid=(t // tm,),
        in_specs=[row(wa), row(ws), row(d), full(1, wa), full(wa + ws, d), full(1, d),
                  full(2 * d, LANES), full(1, LANES)],
        out_specs=[row(d), row(d), row(LANES), row(LANES)],
        out_shape=[jax.ShapeDtypeStruct((t, d), _F32), jax.ShapeDtypeStruct((t, d), _F32),
                   jax.ShapeDtypeStruct((t, LANES), jnp.int32), jax.ShapeDtypeStruct((t, LANES), _F32)],
        compiler_params=_cparams(1),
        name="out_proj_router",
    )(a_out, s_out, x, attn_out_g, w_out_bf16, norm2_g, w_router_pad, b_router_pad)


def _membership(idx):
    lane = lax.broadcasted_iota(jnp.int32, idx.shape, 1)
    return [lane == idx[:, k:k + 1] for k in range(TOP_K)]


def _count_kernel(idx_ref, cnt_ref):
    @pl.when(pl.program_id(0) == 0)
    def _():
        cnt_ref[...] = jnp.zeros_like(cnt_ref)

    member = sum(oh.astype(_F32) for oh in _membership(idx_ref[...]))
    cnt_ref[...] += jnp.sum(member, axis=0, keepdims=True)


def _slot_kernel(idx_ref, start_ref, dst_ref, carry_ref):
    @pl.when(pl.program_id(0) == 0)
    def _():
        carry_ref[...] = jnp.zeros_like(carry_ref)

    onehots = _membership(idx_ref[...])
    member = sum(oh.astype(_F32) for oh in onehots)
    tb = member.shape[0]
    earlier = (lax.broadcasted_iota(jnp.int32, (tb, tb), 0) > lax.broadcasted_iota(jnp.int32, (tb, tb), 1))
    before = jnp.dot(earlier.astype(_BF16), member.astype(_BF16), preferred_element_type=_F32)
    slot_e = start_ref[...] + carry_ref[...] + before
    lane = lax.broadcasted_iota(jnp.int32, member.shape, 1)
    dst = jnp.zeros(member.shape, _F32)
    for k in range(TOP_K):
        dk = jnp.sum(jnp.where(onehots[k], slot_e, 0.0), axis=-1, keepdims=True)
        dst = jnp.where(lane == k, dk, dst)
    dst_ref[...] = dst.astype(jnp.int32)
    carry_ref[...] += jnp.sum(member, axis=0, keepdims=True)


def _route_counts(idx_pad):
    t = idx_pad.shape[0]
    tb = min(512, t)
    return pl.pallas_call(
        _count_kernel,
        grid=(t // tb,),
        in_specs=[pl.BlockSpec((tb, LANES), lambda i: (i, 0))],
        out_specs=pl.BlockSpec((1, LANES), lambda i: (0, 0)),
        out_shape=jax.ShapeDtypeStruct((1, LANES), _F32),
        compiler_params=_cparams(1),
        name="route_counts",
    )(idx_pad)


def _route_slots(idx_pad, starts_f32):
    t = idx_pad.shape[0]
    tb = min(512, t)
    return pl.pallas_call(
        _slot_kernel,
        grid=(t // tb,),
        in_specs=[pl.BlockSpec((tb, LANES), lambda i: (i, 0)), pl.BlockSpec((1, LANES), lambda i: (0, 0))],
        out_specs=pl.BlockSpec((tb, LANES), lambda i: (i, 0)),
        out_shape=jax.ShapeDtypeStruct((t, LANES), jnp.int32),
        scratch_shapes=[pltpu.VMEM((1, LANES), _F32)],
        compiler_params=_cparams(1),
        name="route_slots",
    )(idx_pad, starts_f32)


ROUTE_TB = 256


ZERO_ROWS = 256


def _scatter_rows_kernel(pad_beg_ref, pad_mid_ref, pad_end_ref, dst_ref, xn_ref, xs_ref, zero_ref, sem, zsem):
    tb = xn_ref.shape[0]
    first = pl.program_id(0) == 0
    n_regions = pad_beg_ref.shape[0]

    def issue(t, c):
        for k in range(TOP_K):
            d = dst_ref[t * TOP_K + k]
            pltpu.make_async_copy(xn_ref.at[pl.ds(t, 1)], xs_ref.at[pl.ds(d, 1)], sem).start(priority=k % 2)
        return c

    def drain():
        for _ in range(TOP_K):
            pltpu.make_async_copy(xn_ref, xs_ref.at[pl.ds(0, tb)], sem).wait()

    def clear_padding(start):
        def region(e, c):
            beg, mid, end = pad_beg_ref[e], pad_mid_ref[e], pad_end_ref[e]

            def one_row(r, c2):
                cp = pltpu.make_async_copy(zero_ref.at[pl.ds(0, 1)], xs_ref.at[pl.ds(r, 1)], zsem)
                cp.start() if start else cp.wait()
                return c2

            def one_block(b, c2):
                r0 = pl.multiple_of(mid + b * ZERO_ROWS, ZERO_ROWS)
                cp = pltpu.make_async_copy(zero_ref, xs_ref.at[pl.ds(r0, ZERO_ROWS)], zsem)
                cp.start() if start else cp.wait()
                return c2

            lax.fori_loop(beg, mid, one_row, 0)
            lax.fori_loop(0, (end - mid) // ZERO_ROWS, one_block, 0)
            return c

        lax.fori_loop(0, n_regions, region, 0)

    @pl.when(first)
    def _():
        zero_ref[...] = jnp.zeros_like(zero_ref)
        clear_padding(True)

    lax.fori_loop(0, tb, issue, 0, unroll=4)
    drain()

    @pl.when(first)
    def _():
        clear_padding(False)


def _scatter_rows(pad_beg, pad_mid, pad_end, dst_flat, xn, n_slots):
    t, d = xn.shape
    tb = min(ROUTE_TB, t)
    grid_spec = pltpu.PrefetchScalarGridSpec(
        num_scalar_prefetch=3,
        grid=(t // tb,),
        in_specs=[
            pl.BlockSpec((tb * TOP_K,), lambda i, *_: (i,), memory_space=pltpu.SMEM),
            pl.BlockSpec((tb, d), lambda i, *_: (i, 0)),
        ],
        out_specs=pl.BlockSpec(memory_space=pl.ANY),
        scratch_shapes=[pltpu.VMEM((ZERO_ROWS, d), _F32), pltpu.SemaphoreType.DMA(()),
                        pltpu.SemaphoreType.DMA(())],
    )
    return pl.pallas_call(
        _scatter_rows_kernel,
        grid_spec=grid_spec,
        out_shape=jax.ShapeDtypeStruct((n_slots, d), _F32),
        compiler_params=_cparams(1),
        name="scatter_rows",
    )(pad_beg, pad_mid, pad_end, dst_flat, xn)


def _combine_kernel(dst_ref, gate_ref, x2_ref, ys_ref, o_ref, rows_ref, sem):
    tb = x2_ref.shape[0]

    def issue(t, c):
        for k in range(TOP_K):
            d = dst_ref[t * TOP_K + k]
            pltpu.make_async_copy(ys_ref.at[pl.ds(d, 1)], rows_ref.at[k, pl.ds(t, 1)], sem).start(priority=k % 2)
        return c

    lax.fori_loop(0, tb, issue, 0, unroll=4)
    for k in range(TOP_K):
        pltpu.make_async_copy(ys_ref.at[pl.ds(0, tb)], rows_ref.at[k], sem).wait()
    acc = x2_ref[...]
    gates = gate_ref[...]
    for k in range(TOP_K):
        acc = acc + gates[:, k:k + 1] * rows_ref[k]
    o_ref[...] = acc


def _combine(dst_flat, gates_pad, x2, ys):
    t, d = x2.shape
    tb = min(ROUTE_TB, t)
    return pl.pallas_call(
        _combine_kernel,
        grid=(t // tb,),
        in_specs=[
            pl.BlockSpec((tb * TOP_K,), lambda i: (i,), memory_space=pltpu.SMEM),
            pl.BlockSpec((tb, LANES), lambda i: (i, 0)),
            pl.BlockSpec((tb, d), lambda i: (i, 0)),
            pl.BlockSpec(memory_space=pl.ANY),
        ],
        out_specs=pl.BlockSpec((tb, d), lambda i: (i, 0)),
        out_shape=jax.ShapeDtypeStruct((t, d), _F32),
        scratch_shapes=[pltpu.VMEM((TOP_K, tb, d), _F32), pltpu.SemaphoreType.DMA(())],
        compiler_params=_cparams(1),
        name="combine_rows",
    )(dst_flat, gates_pad, x2, ys)


def _ffn_kernel(te_ref, tr_ref, tblk_ref, x_ref, wg_ref, wl_ref, bg_ref, bl_ref, wd_ref, bd_ref,
                o_ref, xb_ref, *, sub):
    del te_ref, tblk_ref
    i = pl.program_id(0)
    j = pl.program_id(1)
    n_rows = tr_ref[i]
    tm, d = o_ref.shape

    @pl.when(jnp.logical_and(j == 0, n_rows > 0))
    def _():
        o_ref[...] = jnp.broadcast_to(bd_ref[...], (tm, d))

    @pl.when(jnp.logical_and(j == 0, n_rows == 0))
    def _():
        o_ref[...] = jnp.zeros((tm, d), _F32)

    def rows_block(r0, size):
        rs = pl.ds(r0, size)

        @pl.when(j == 0)
        def _():
            xb_ref[rs, :] = x_ref[rs, :].astype(_BF16)

        xb = xb_ref[rs, :]
        hg = jnp.dot(xb, wg_ref[...], preferred_element_type=_F32) + bg_ref[...]
        hl = jnp.dot(xb, wl_ref[...], preferred_element_type=_F32) + bl_ref[...]
        gate = jnp.minimum(hg, SWIGLU_LIMIT)
        lin = jnp.clip(hl, -SWIGLU_LIMIT, SWIGLU_LIMIT)
        act = gate * (1.0 / (1.0 + jnp.exp(-SWIGLU_ALPHA * gate))) * (lin + 1.0)
        o_ref[rs, :] += jnp.dot(act.astype(_BF16), wd_ref[...], preferred_element_type=_F32)

    @pl.when(n_rows == tm)
    def _():
        rows_block(0, tm)

    @pl.when(n_rows < tm)
    def _():
        def sub_block(s, c):
            rows_block(pl.multiple_of(s * sub, sub), sub)
            return c

        lax.fori_loop(0, (n_rows + sub - 1) // sub, sub_block, 0)


def _expert_ffn(tile_expert, tile_rows, tile_blk, xs, w_gu_bf16, b_gu, w_d_bf16, b_d, *, tm, tn, sub):
    n_slots, d = xs.shape
    n_exp, _, de2 = w_gu_bf16.shape
    de = de2 // 2
    nj = de // tn
    n_tiles = n_slots // tm
    assert de % tn == 0 and n_slots % tm == 0 and tm % sub == 0
    grid_spec = pltpu.PrefetchScalarGridSpec(
        num_scalar_prefetch=3,
        grid=(n_tiles, nj),
        in_specs=[
            pl.BlockSpec((tm, d), lambda i, j, te, tr, tb: (tb[i], 0)),
            pl.BlockSpec((None, d, tn), lambda i, j, te, tr, tb: (te[i], 0, j)),
            pl.BlockSpec((None, d, tn), lambda i, j, te, tr, tb: (te[i], 0, nj + j)),
            pl.BlockSpec((None, 1, tn), lambda i, j, te, tr, tb: (te[i], 0, j)),
            pl.BlockSpec((None, 1, tn), lambda i, j, te, tr, tb: (te[i], 0, nj + j)),
            pl.BlockSpec((None, tn, d), lambda i, j, te, tr, tb: (te[i], j, 0)),
            pl.BlockSpec((None, 1, d), lambda i, j, te, tr, tb: (te[i], 0, 0)),
        ],
        out_specs=pl.BlockSpec((tm, d), lambda i, j, te, tr, tb: (i, 0)),
        scratch_shapes=[pltpu.VMEM((tm, d), _BF16)],
    )
    return pl.pallas_call(
        functools.partial(_ffn_kernel, sub=sub),
        grid_spec=grid_spec,
        out_shape=jax.ShapeDtypeStruct((n_slots, d), _F32),
        compiler_params=_cparams(2),
        name="expert_ffn",
    )(tile_expert, tile_rows, tile_blk, xs, w_gu_bf16, w_gu_bf16,
      b_gu.reshape(n_exp, 1, de2), b_gu.reshape(n_exp, 1, de2), w_d_bf16, b_d.reshape(n_exp, 1, d))


def _tile_plan(counts, *, tm, n_tiles):
    n_exp = counts.shape[0]
    tiles_e = (counts + tm - 1) // tm
    tile_end = jnp.cumsum(tiles_e)
    tile_beg = tile_end - tiles_e
    starts = tile_beg * tm
    n_active = tile_end[-1]
    tid = jnp.minimum(jnp.arange(n_tiles, dtype=jnp.int32), n_active - 1)
    expert = jnp.minimum(jnp.sum(tile_end[None, :] <= tid[:, None], axis=1), n_exp - 1).astype(jnp.int32)
    rows = jnp.clip(counts[expert] - (tid - tile_beg[expert]) * tm, 0, tm)
    rows = jnp.where(jnp.arange(n_tiles) < n_active, rows, 0).astype(jnp.int32)
    pad_beg = jnp.concatenate([starts + counts, (n_active * tm)[None]])
    pad_end = jnp.concatenate([tile_end * tm, jnp.full((1,), n_tiles * tm, jnp.int32)])
    pad_mid = jnp.minimum((pad_beg + ZERO_ROWS - 1) // ZERO_ROWS * ZERO_ROWS, pad_end)
    pads = tuple(p.astype(jnp.int32) for p in (pad_beg, pad_mid, pad_end))
    return starts, expert, rows, tid.astype(jnp.int32), pads


def _layer(x, norm1_g, w_in, q_norm_g, k_norm_g, rpb, sgu_norm_g, w_spatial, b_spatial, attn_out_g,
           sgu_out_g, w_out, norm2_g, w_router, b_router, w_gate_up, b_gate_up, w_down, b_down):
    batch, seq, d = x.shape
    t = batch * seq
    head_dim = q_norm_g.shape[-1]
    n_heads = rpb.shape[0]
    attn_width = n_heads * head_dim
    n_groups, chunk, _ = w_spatial.shape
    sgu_width = sgu_norm_g.shape[-1]
    group_dim = sgu_width // n_groups
    win_rows_max = (rpb.shape[1] + 1) // 2
    win_cols = (rpb.shape[2] + 1) // 2
    rows = seq // GRID_W
    kh = min(win_rows_max, rows)
    n_exp = w_router.shape[-1]
    assert TOP_K <= n_exp <= LANES // 2 and seq % GRID_W == 0 and w_in.shape[1] == 3 * attn_width + 2 * sgu_width

    xf = x.reshape(t, d)
    row = lambda v: v.reshape(1, -1).astype(_F32)

    w_in_b = w_in.astype(_BF16)
    pair = lambda g: jnp.concatenate([g, g]).reshape(1, 1, LANES)
    head_gain = jnp.concatenate([pair(q_norm_g), pair(k_norm_g), jnp.ones((1, 1, LANES), _F32)], axis=0)
    qkv = _norm_proj(xf, row(norm1_g), w_in_b[:, :3 * attn_width], head_gain,
                     slab_out=True, attn_width=attn_width, head_dim=head_dim)
    ug = _norm_proj(xf, row(norm1_g), w_in_b[:, 3 * attn_width:], head_gain,
                    slab_out=False, attn_width=attn_width, head_dim=head_dim)

    bias_tab = _attention_bias_table(rpb, kh=kh, win_cols=win_cols)
    a_out = _attention(qkv, bias_tab, batch=batch, seq=seq, n_heads=n_heads, head_dim=head_dim, kh=kh)

    b_full = jnp.repeat(b_spatial.T, group_dim, axis=1)
    s_out = _sgu(ug, row(sgu_norm_g), w_spatial.astype(_BF16), b_full, row(sgu_out_g),
                 width=sgu_width, chunk=chunk, group_dim=group_dim)

    half = LANES // 2
    w_r_hi = w_router.astype(_BF16)
    w_r_lo = (w_router - w_r_hi.astype(_F32)).astype(_BF16)
    blank = jnp.zeros((d, LANES), _BF16)
    w_router_pad = jnp.concatenate([blank.at[:, :n_exp].set(w_r_hi).at[:, half:half + n_exp].set(w_r_lo),
                                    blank.at[:, :n_exp].set(w_r_hi)], axis=0)
    b_router_pad = jnp.full((1, LANES), NEG_BIG, _F32).at[0, :n_exp].set(b_router)
    x2, xn, idx_pad, gates_pad = _out_router(a_out, s_out, xf, row(attn_out_g), w_out.astype(_BF16),
                                             row(norm2_g), w_router_pad, b_router_pad)

    de = w_down.shape[1]
    tm = min(1024, t)
    tn = min(512, de)
    sub = min(256, tm)
    n_tiles = (t * TOP_K) // tm + n_exp
    counts = _route_counts(idx_pad)[0, :n_exp].astype(jnp.int32)
    assert tm % ZERO_ROWS == 0
    starts, tile_expert, tile_rows, tile_blk, pads = _tile_plan(counts, tm=tm, n_tiles=n_tiles)
    starts_pad = jnp.zeros((1, LANES), _F32).at[0, :n_exp].set(starts.astype(_F32))
    dst_pad = _route_slots(idx_pad, starts_pad)
    dst_flat = dst_pad[:, :TOP_K].reshape(-1)

    xs = _scatter_rows(*pads, dst_flat, xn, n_tiles * tm)
    ys = _expert_ffn(tile_expert, tile_rows, tile_blk, xs, w_gate_up.astype(_BF16), b_gate_up,
                     w_down.astype(_BF16), b_down, tm=tm, tn=tn, sub=sub)
    out = _combine(dst_flat, gates_pad, x2, ys)
    return out.reshape(batch, seq, d)


def kernel(x, norm1_g, w_in, q_norm_g, k_norm_g, rpb, sgu_norm_g, w_spatial, b_spatial, attn_out_g,
           sgu_out_g, w_out, norm2_g, w_router, b_router, w_gate_up, b_gate_up, w_down, b_down):
    depth = norm1_g.shape[0]
    for l in range(depth):
        x = _layer(x, norm1_g[l], w_in[l], q_norm_g[l], k_norm_g[l], rpb[l], sgu_norm_g[l], w_spatial[l],
                   b_spatial[l], attn_out_g[l], sgu_out_g[l], w_out[l], norm2_g[l], w_router[l], b_router[l],
                   w_gate_up[l], b_gate_up[l], w_down[l], b_down[l])
    return x
```

```python
import functools

import jax
import jax.numpy as jnp
import numpy as np
from jax import lax
from jax.experimental import pallas as pl
from jax.experimental.pallas import tpu as pltpu

GRID_W = 64
TOP_K = 4
SWIGLU_LIMIT = 7.0
SWIGLU_ALPHA = 1.702
EPS = 1e-6
LANES = 128
NEG_BIG = -1e30
VMEM_LIMIT_BYTES = 56 * 1024 * 1024

_F32 = jnp.float32
_BF16 = jnp.bfloat16


def _cparams(n_axes):
    return pltpu.CompilerParams(
        dimension_semantics=("arbitrary",) * n_axes,
        vmem_limit_bytes=VMEM_LIMIT_BYTES)


def _rms(x):
    return x * lax.rsqrt(jnp.mean(x * x, axis=-1, keepdims=True) + EPS)


def _gelu(x):
    return 0.5 * x * (1.0 + lax.erf(x * 0.7071067811865476))


def _norm_proj_kernel(x_ref, g_ref, w_ref, hg_ref, o_ref, h_ref, *, slab_out, n_norm_tiles, head_dim):
    j = pl.program_id(1)

    @pl.when(j == 0)
    def _():
        h_ref[...] = (_rms(x_ref[...]) * g_ref[...]).astype(_BF16)

    if not slab_out:
        o_ref[...] = jnp.dot(h_ref[...], w_ref[...], preferred_element_type=_F32)
        return

    left = lax.broadcasted_iota(jnp.int32, (1, LANES), 1) < head_dim
    normed = j < n_norm_tiles
    tn = w_ref.shape[1]
    piece = min(2 * LANES, tn)
    for p in range(tn // piece):
        res = jnp.dot(h_ref[...], w_ref[:, p * piece:(p + 1) * piece], preferred_element_type=_F32)
        for s in range(piece // LANES):
            r = res[:, s * LANES:(s + 1) * LANES]
            ss = r * r
            ss_first = jnp.sum(jnp.where(left, ss, 0.0), axis=-1, keepdims=True)
            ss_second = jnp.sum(jnp.where(left, 0.0, ss), axis=-1, keepdims=True)
            inv = jnp.where(left, lax.rsqrt(ss_first / head_dim + EPS),
                            lax.rsqrt(ss_second / head_dim + EPS)) * hg_ref[0]
            o_ref[p * (piece // LANES) + s] = (r * jnp.where(normed, inv, 1.0)).astype(o_ref.dtype)


def _norm_proj(x, g, w_bf16, head_gain, *, slab_out, attn_width, head_dim):
    t, d = x.shape
    n = w_bf16.shape[1]
    tm = min(1024, t)
    tn = min(1024, attn_width if slab_out else n)
    assert t % tm == 0 and n % tn == 0 and tn % LANES == 0
    if slab_out:
        assert attn_width % tn == 0 and 2 * head_dim == LANES
        tiles_per_part = attn_width // tn
        out_shape = jax.ShapeDtypeStruct((n // LANES, t, LANES), _BF16)
        out_spec = pl.BlockSpec((tn // LANES, tm, LANES), lambda i, j: (j, i, 0))
        hg_spec = pl.BlockSpec((1, 1, LANES), lambda i, j: (jnp.minimum(j // tiles_per_part, 2), 0, 0))
        n_norm_tiles = 2 * tiles_per_part
    else:
        out_shape = jax.ShapeDtypeStruct((t, n), _F32)
        out_spec = pl.BlockSpec((tm, tn), lambda i, j: (i, j))
        hg_spec = pl.BlockSpec((1, 1, LANES), lambda i, j: (0, 0, 0))
        n_norm_tiles = 0
    kern = functools.partial(_norm_proj_kernel, slab_out=slab_out, n_norm_tiles=n_norm_tiles, head_dim=head_dim)
    return pl.pallas_call(
        kern,
        grid=(t // tm, n // tn),
        in_specs=[
            pl.BlockSpec((tm, d), lambda i, j: (i, 0)),
            pl.BlockSpec((1, d), lambda i, j: (0, 0)),
            pl.BlockSpec((d, tn), lambda i, j: (0, j)),
            hg_spec,
        ],
        out_specs=out_spec,
        out_shape=out_shape,
        scratch_shapes=[pltpu.VMEM((tm, d), _BF16)],
        compiler_params=_cparams(2),
        name="norm_proj_qkv" if slab_out else "norm_proj_ug",
    )(x, g, w_bf16, head_gain)


def _attn_kernel(q_ref, kp_ref, kc_ref, kn_ref, vp_ref, vc_ref, vn_ref, bias_ref, o_ref, kwin, vwin,
                 *, rows, rblk, kh, head_dim, scale):
    rb = pl.program_id(2)
    blk = rblk * GRID_W
    for w, (kr, vr) in enumerate(((kp_ref, vp_ref), (kc_ref, vc_ref), (kn_ref, vn_ref))):
        kwin[w * blk:(w + 1) * blk, :] = kr[...]
        vwin[w * blk:(w + 1) * blk, :] = vr[...]
    lane = lax.broadcasted_iota(jnp.int32, (1, LANES), 1)
    left = lane < head_dim
    nk = (kh + 2) * GRID_W
    scores, offsets = [], []
    for pair in range(rblk // 2):
        r = rb * rblk + 2 * pair
        rs_a = jnp.clip(r - kh // 2, 0, rows - kh)
        rs_b = jnp.clip(r + 1 - kh // 2, 0, rows - kh)
        kind = (r - rs_a) + (rs_b - rs_a)
        off = pl.multiple_of((rs_a - (rb * rblk - rblk)) * GRID_W, GRID_W)
        q2 = q_ref[2 * pair * GRID_W:(2 * pair + 2) * GRID_W, :] * scale
        qa, qb = q2[:GRID_W], q2[GRID_W:]
        qm = jnp.concatenate([jnp.where(left, qa, 0.0), jnp.where(left, qb, 0.0),
                              jnp.where(left, 0.0, qa), jnp.where(left, 0.0, qb)], axis=0).astype(_BF16)
        s = lax.dot_general(qm, kwin[pl.ds(off, nk), :], (((1,), (1,)), ((), ())), preferred_element_type=_F32)
        scores.append(s + bias_ref[kind])
        offsets.append(off)
    for pair in range(rblk // 2):
        s = scores[pair]
        p = jnp.exp(s - jnp.max(s, axis=-1, keepdims=True))
        denom = jnp.sum(p, axis=-1, keepdims=True)
        pv = jnp.dot(p.astype(_BF16), vwin[pl.ds(offsets[pair], nk), :], preferred_element_type=_F32) / denom
        o_ref[2 * pair * GRID_W:(2 * pair + 2) * GRID_W, :] = jnp.where(
            left, pv[0:2 * GRID_W], pv[2 * GRID_W:4 * GRID_W])


def _attention(qkv, bias_tab, *, batch, seq, n_heads, head_dim, kh):
    n_hp = n_heads // 2
    rows = seq // GRID_W
    rblk = kh
    assert rows % rblk == 0 and rows >= kh and rblk >= 4
    nrb = rows // rblk
    blk = rblk * GRID_W
    t = batch * seq

    def qmap(b, hp, rb):
        return (hp, b * nrb + rb, 0)

    def kvmap(part, delta):
        def f(b, hp, rb):
            return (part * n_hp + hp, b * nrb + jnp.clip(rb + delta, 0, nrb - 1), 0)
        return f

    slab = lambda m: pl.BlockSpec((None, blk, LANES), m)
    kern = functools.partial(_attn_kernel, rows=rows, rblk=rblk, kh=kh, head_dim=head_dim,
                             scale=float(head_dim) ** -0.5)
    return pl.pallas_call(
        kern,
        grid=(batch, n_hp, nrb),
        in_specs=[
            slab(qmap),
            slab(kvmap(1, -1)), slab(kvmap(1, 0)), slab(kvmap(1, 1)),
            slab(kvmap(2, -1)), slab(kvmap(2, 0)), slab(kvmap(2, 1)),
            pl.BlockSpec((kh, None, 4 * GRID_W, (kh + 2) * GRID_W), lambda b, hp, rb: (0, hp, 0, 0)),
        ],
        out_specs=pl.BlockSpec((blk, LANES), lambda b, hp, rb: (b * nrb + rb, hp)),
        out_shape=jax.ShapeDtypeStruct((t, n_hp * LANES), _F32),
        scratch_shapes=[pltpu.VMEM((3 * blk, LANES), _BF16), pltpu.VMEM((3 * blk, LANES), _BF16)],
        compiler_params=_cparams(3),
        name="nbr_attention",
    )(qkv, qkv, qkv, qkv, qkv, qkv, qkv, bias_tab)


def _attention_bias_table(rpb, *, kh, win_cols):
    n_heads, n_rpb_rows, n_rpb_cols = rpb.shape
    win_rows_max = (n_rpb_rows + 1) // 2
    nkr = kh + 2
    qc = np.arange(GRID_W)[:, None]
    kc = np.arange(GRID_W)[None, :]
    col_start = np.clip(qc - win_cols // 2, 0, GRID_W - win_cols)
    col_valid = (kc >= col_start) & (kc < col_start + win_cols)
    col_sel = np.zeros((GRID_W, GRID_W, n_rpb_cols), np.float32)
    qi, ki = np.nonzero(col_valid)
    col_sel[qi, ki, ki - qi + win_cols - 1] = 1.0
    row_sel = np.zeros((kh, 2, nkr, n_rpb_rows), np.float32)
    row_valid = np.zeros((kh, 2, nkr), bool)
    for kind in range(kh):
        shift_a, delta = (kind // 2) * 2, kind % 2
        for member, (shift, dlt) in enumerate(((shift_a, 0), (shift_a + 1 - delta, delta))):
            for jj in range(nkr):
                if 0 <= jj - dlt < kh:
                    row_sel[kind, member, jj, jj - dlt - shift + win_rows_max - 1] = 1.0
                    row_valid[kind, member, jj] = True
    exact = lax.Precision.HIGHEST
    cols = jnp.einsum("hrc,qkc->rhqk", rpb.astype(_F32), col_sel, precision=exact)
    tab = jnp.einsum("ymjr,rhqk->yhmqjk", row_sel, cols, precision=exact)
    keep = row_valid[:, None, :, None, :, None] & col_valid[None, None, None, :, None, :]
    tab = jnp.where(keep, tab, NEG_BIG)
    return tab.reshape(kh, n_heads // 2, 4 * GRID_W, nkr * GRID_W)


def _sgu_kernel(u_ref, g_ref, ng_ref, w_ref, b_ref, og_ref, o_ref, *, chunk, group_dim):
    tm, width = u_ref.shape
    gg = _gelu(g_ref[...])
    gn = (_rms(gg) * ng_ref[...]).astype(_BF16)
    lane = lax.broadcasted_iota(jnp.int32, (1, LANES), 1)
    left = lane < group_dim
    for c in range(tm // chunk):
        rs = slice(c * chunk, (c + 1) * chunk)
        for gp in range(width // LANES):
            cs = slice(gp * LANES, (gp + 1) * LANES)
            xg = gn[rs, cs]
            ma = jnp.dot(w_ref[2 * gp], xg, preferred_element_type=_F32)
            mb = jnp.dot(w_ref[2 * gp + 1], xg, preferred_element_type=_F32)
            mixed = jnp.where(left, ma, mb) + b_ref[:, cs]
            s = _gelu(u_ref[rs, cs]) * mixed
            o_ref[rs, cs] = s
    s_all = o_ref[...]
    o_ref[...] = _rms(s_all) * og_ref[...]


def _sgu(ug, norm_g, w_sp_bf16, b_full, out_g, *, width, chunk, group_dim):
    t = ug.shape[0]
    tm = min(512, t)
    assert t % tm == 0 and tm % chunk == 0 and width % LANES == 0 and 2 * group_dim == LANES
    n_groups = w_sp_bf16.shape[0]
    kern = functools.partial(_sgu_kernel, chunk=chunk, group_dim=group_dim)
    return pl.pallas_call(
        kern,
        grid=(t // tm,),
        in_specs=[
            pl.BlockSpec((tm, width), lambda i: (i, 0)),
            pl.BlockSpec((tm, width), lambda i: (i, 1)),
            pl.BlockSpec((1, width), lambda i: (0, 0)),
            pl.BlockSpec((n_groups, chunk, chunk), lambda i: (0, 0, 0)),
            pl.BlockSpec((chunk, width), lambda i: (0, 0)),
            pl.BlockSpec((1, width), lambda i: (0, 0)),
        ],
        out_specs=pl.BlockSpec((tm, width), lambda i: (i, 0)),
        out_shape=jax.ShapeDtypeStruct((t, width), _F32),
        compiler_params=_cparams(1),
        name="spatial_gating",
    )(ug, ug, norm_g, w_sp_bf16, b_full, out_g)


ROUTER_ROW_CHUNK = 128


def _out_router_kernel(a_ref, s_ref, x_ref, ag_ref, w_ref, n2g_ref, wr_ref, br_ref,
                       x2_ref, xn_ref, idx_ref, gate_ref):
    chunks = [slice(c * ROUTER_ROW_CHUNK, (c + 1) * ROUTER_ROW_CHUNK)
              for c in range(x_ref.shape[0] // ROUTER_ROW_CHUNK)]
    for rs in chunks:
        an = _rms(a_ref[rs, :]) * ag_ref[...]
        mix = jnp.concatenate([an, s_ref[rs, :]], axis=-1).astype(_BF16)
        x2_ref[rs, :] = x_ref[rs, :] + jnp.dot(mix, w_ref[...], preferred_element_type=_F32)
    all_logits = []
    for rs in chunks:
        xn = _rms(x2_ref[rs, :]) * n2g_ref[...]
        xn_ref[rs, :] = xn
        xh = xn.astype(_BF16)
        xl = (xn - xh.astype(_F32)).astype(_BF16)
        parts = jnp.dot(jnp.concatenate([xh, xl], axis=-1), wr_ref[...], preferred_element_type=_F32)
        all_logits.append(parts + pltpu.roll(parts, LANES // 2, axis=1) + br_ref[...])
    for rs, logits in zip(chunks, all_logits):
        rc = logits.shape[0]
        lane = lax.broadcasted_iota(jnp.int32, (rc, LANES), 1)
        lane_f = lane.astype(_F32)
        idx_acc = jnp.zeros((rc, LANES), _F32)
        vals = []
        cur = logits
        for k in range(TOP_K):
            m = jnp.max(cur, axis=-1, keepdims=True)
            ik = jnp.min(jnp.where(cur == m, lane_f, float(LANES)), axis=-1, keepdims=True)
            vals.append(m)
            idx_acc = jnp.where(lane == k, ik, idx_acc)
            cur = jnp.where(lane_f == ik, -jnp.inf, cur)
        exps = [jnp.exp(v - vals[0]) for v in vals]
        denom = exps[0]
        for e in exps[1:]:
            denom = denom + e
        gate_acc = jnp.zeros((rc, LANES), _F32)
        for k in range(TOP_K):
            gate_acc = jnp.where(lane == k, exps[k] / denom, gate_acc)
        idx_ref[rs, :] = idx_acc.astype(jnp.int32)
        gate_ref[rs, :] = gate_acc


def _out_router(a_out, s_out, x, attn_out_g, w_out_bf16, norm2_g, w_router_pad, b_router_pad):
    t, d = x.shape
    wa = a_out.shape[1]
    ws = s_out.shape[1]
    tm = min(512, t)
    assert t % tm == 0
    row = lambda n: pl.BlockSpec((tm, n), lambda i: (i, 0))
    full = lambda a, b: pl.BlockSpec((a, b), lambda i: (0, 0))
    return pl.pallas_call(
        _out_router_kernel,
        grid=(t // tm,),
        in_specs=[row(wa), row(ws), row(d), full(1, wa), full(wa + ws, d), full(1, d),
                  full(2 * d, LANES), full(1, LANES)],
        out_specs=[row(d), row(d), row(LANES), row(LANES)],
        out_shape=[jax.ShapeDtypeStruct((t, d), _F32), jax.ShapeDtypeStruct((t, d), _F32),
                   jax.ShapeDtypeStruct((t, LANES), jnp.int32), jax.ShapeDtypeStruct((t, LANES), _F32)],
        compiler_params=_cparams(1),
        name="out_proj_router",
    )(a_out, s_out, x, attn_out_g, w_out_bf16, norm2_g, w_router_pad, b_router_pad)


def _membership(idx):
    lane = lax.broadcasted_iota(jnp.int32, idx.shape, 1)
    return [lane == idx[:, k:k + 1] for k in range(TOP_K)]


def _count_kernel(idx_ref, cnt_ref):
    @pl.when(pl.program_id(0) == 0)
    def _():
        cnt_ref[...] = jnp.zeros_like(cnt_ref)

    member = sum(oh.astype(_F32) for oh in _membership(idx_ref[...]))
    cnt_ref[...] += jnp.sum(member, axis=0, keepdims=True)


def _slot_kernel(idx_ref, start_ref, dst_ref, carry_ref):
    @pl.when(pl.program_id(0) == 0)
    def _():
        carry_ref[...] = jnp.zeros_like(carry_ref)

    onehots = _membership(idx_ref[...])
    member = sum(oh.astype(_F32) for oh in onehots)
    tb = member.shape[0]
    earlier = (lax.broadcasted_iota(jnp.int32, (tb, tb), 0) > lax.broadcasted_iota(jnp.int32, (tb, tb), 1))
    before = jnp.dot(earlier.astype(_BF16), member.astype(_BF16), preferred_element_type=_F32)
    slot_e = start_ref[...] + carry_ref[...] + before
    lane = lax.broadcasted_iota(jnp.int32, member.shape, 1)
    dst = jnp.zeros(member.shape, _F32)
    for k in range(TOP_K):
        dk = jnp.sum(jnp.where(onehots[k], slot_e, 0.0), axis=-1, keepdims=True)
        dst = jnp.where(lane == k, dk, dst)
    dst_ref[...] = dst.astype(jnp.int32)
    carry_ref[...] += jnp.sum(member, axis=0, keepdims=True)


def _route_counts(idx_pad):
    t = idx_pad.shape[0]
    tb = min(512, t)
    return pl.pallas_call(
        _count_kernel,
        grid=(t // tb,),
        in_specs=[pl.BlockSpec((tb, LANES), lambda i: (i, 0))],
        out_specs=pl.BlockSpec((1, LANES), lambda i: (0, 0)),
        out_shape=jax.ShapeDtypeStruct((1, LANES), _F32),
        compiler_params=_cparams(1),
        name="route_counts",
    )(idx_pad)


def _route_slots(idx_pad, starts_f32):
    t = idx_pad.shape[0]
    tb = min(512, t)
    return pl.pallas_call(
        _slot_kernel,
        grid=(t // tb,),
        in_specs=[pl.BlockSpec((tb, LANES), lambda i: (i, 0)), pl.BlockSpec((1, LANES), lambda i: (0, 0))],
        out_specs=pl.BlockSpec((tb, LANES), lambda i: (i, 0)),
        out_shape=jax.ShapeDtypeStruct((t, LANES), jnp.int32),
        scratch_shapes=[pltpu.VMEM((1, LANES), _F32)],
        compiler_params=_cparams(1),
        name="route_slots",
    )(idx_pad, starts_f32)


ROUTE_TB = 256


ZERO_ROWS = 256


def _scatter_rows_kernel(pad_beg_ref, pad_mid_ref, pad_end_ref, dst_ref, xn_ref, xs_ref, zero_ref, sem, zsem):
    tb = xn_ref.shape[0]
    first = pl.program_id(0) == 0
    n_regions = pad_beg_ref.shape[0]

    def issue(t, c):
        for k in range(TOP_K):
            d = dst_ref[t * TOP_K + k]
            pltpu.make_async_copy(xn_ref.at[pl.ds(t, 1)], xs_ref.at[pl.ds(d, 1)], sem).start(priority=k % 2)
        return c

    def drain():
        for _ in range(TOP_K):
            pltpu.make_async_copy(xn_ref, xs_ref.at[pl.ds(0, tb)], sem).wait()

    def clear_padding(start):
        def region(e, c):
            beg, mid, end = pad_beg_ref[e], pad_mid_ref[e], pad_end_ref[e]

            def one_row(r, c2):
                cp = pltpu.make_async_copy(zero_ref.at[pl.ds(0, 1)], xs_ref.at[pl.ds(r, 1)], zsem)
                cp.start() if start else cp.wait()
                return c2

            def one_block(b, c2):
                r0 = pl.multiple_of(mid + b * ZERO_ROWS, ZERO_ROWS)
                cp = pltpu.make_async_copy(zero_ref, xs_ref.at[pl.ds(r0, ZERO_ROWS)], zsem)
                cp.start() if start else cp.wait()
                return c2

            lax.fori_loop(beg, mid, one_row, 0)
            lax.fori_loop(0, (end - mid) // ZERO_ROWS, one_block, 0)
            return c

        lax.fori_loop(0, n_regions, region, 0)

    @pl.when(first)
    def _():
        zero_ref[...] = jnp.zeros_like(zero_ref)
        clear_padding(True)

    lax.fori_loop(0, tb, issue, 0, unroll=4)
    drain()

    @pl.when(first)
    def _():
        clear_padding(False)


def _scatter_rows(pad_beg, pad_mid, pad_end, dst_flat, xn, n_slots):
    t, d = xn.shape
    tb = min(ROUTE_TB, t)
    grid_spec = pltpu.PrefetchScalarGridSpec(
        num_scalar_prefetch=3,
        grid=(t // tb,),
        in_specs=[
            pl.BlockSpec((tb * TOP_K,), lambda i, *_: (i,), memory_space=pltpu.SMEM),
            pl.BlockSpec((tb, d), lambda i, *_: (i, 0)),
        ],
        out_specs=pl.BlockSpec(memory_space=pl.ANY),
        scratch_shapes=[pltpu.VMEM((ZERO_ROWS, d), _F32), pltpu.SemaphoreType.DMA(()),
                        pltpu.SemaphoreType.DMA(())],
    )
    return pl.pallas_call(
        _scatter_rows_kernel,
        grid_spec=grid_spec,
        out_shape=jax.ShapeDtypeStruct((n_slots, d), _F32),
        compiler_params=_cparams(1),
        name="scatter_rows",
    )(pad_beg, pad_mid, pad_end, dst_flat, xn)


def _combine_kernel(dst_ref, gate_ref, x2_ref, ys_ref, o_ref, rows_ref, sem):
    tb = x2_ref.shape[0]

    def issue(t, c):
        for k in range(TOP_K):
            d = dst_ref[t * TOP_K + k]
            pltpu.make_async_copy(ys_ref.at[pl.ds(d, 1)], rows_ref.at[k, pl.ds(t, 1)], sem).start(priority=k % 2)
        return c

    lax.fori_loop(0, tb, issue, 0, unroll=4)
    for k in range(TOP_K):
        pltpu.make_async_copy(ys_ref.at[pl.ds(0, tb)], rows_ref.at[k], sem).wait()
    acc = x2_ref[...]
    gates = gate_ref[...]
    for k in range(TOP_K):
        acc = acc + gates[:, k:k + 1] * rows_ref[k]
    o_ref[...] = acc


def _combine(dst_flat, gates_pad, x2, ys):
    t, d = x2.shape
    tb = min(ROUTE_TB, t)
    return pl.pallas_call(
        _combine_kernel,
        grid=(t // tb,),
        in_specs=[
            pl.BlockSpec((tb * TOP_K,), lambda i: (i,), memory_space=pltpu.SMEM),
            pl.BlockSpec((tb, LANES), lambda i: (i, 0)),
            pl.BlockSpec((tb, d), lambda i: (i, 0)),
            pl.BlockSpec(memory_space=pl.ANY),
        ],
        out_specs=pl.BlockSpec((tb, d), lambda i: (i, 0)),
        out_shape=jax.ShapeDtypeStruct((t, d), _F32),
        scratch_shapes=[pltpu.VMEM((TOP_K, tb, d), _F32), pltpu.SemaphoreType.DMA(())],
        compiler_params=_cparams(1),
        name="combine_rows",
    )(dst_flat, gates_pad, x2, ys)


def _ffn_kernel(te_ref, tr_ref, tblk_ref, x_ref, wg_ref, wl_ref, bg_ref, bl_ref, wd_ref, bd_ref,
                o_ref, xb_ref, *, sub):
    del te_ref, tblk_ref
    i = pl.program_id(0)
    j = pl.program_id(1)
    n_rows = tr_ref[i]
    tm, d = o_ref.shape

    @pl.when(jnp.logical_and(j == 0, n_rows > 0))
    def _():
        o_ref[...] = jnp.broadcast_to(bd_ref[...], (tm, d))

    @pl.when(jnp.logical_and(j == 0, n_rows == 0))
    def _():
        o_ref[...] = jnp.zeros((tm, d), _F32)

    def rows_block(r0, size):
        rs = pl.ds(r0, size)

        @pl.when(j == 0)
        def _():
            xb_ref[rs, :] = x_ref[rs, :].astype(_BF16)

        xb = xb_ref[rs, :]
        hg = jnp.dot(xb, wg_ref[...], preferred_element_type=_F32) + bg_ref[...]
        hl = jnp.dot(xb, wl_ref[...], preferred_element_type=_F32) + bl_ref[...]
        gate = jnp.minimum(hg, SWIGLU_LIMIT)
        lin = jnp.clip(hl, -SWIGLU_LIMIT, SWIGLU_LIMIT)
        act = gate * (1.0 / (1.0 + jnp.exp(-SWIGLU_ALPHA * gate))) * (lin + 1.0)
        o_ref[rs, :] += jnp.dot(act.astype(_BF16), wd_ref[...], preferred_element_type=_F32)

    @pl.when(n_rows == tm)
    def _():
        rows_block(0, tm)

    @pl.when(n_rows < tm)
    def _():
        def sub_block(s, c):
            rows_block(pl.multiple_of(s * sub, sub), sub)
            return c

        lax.fori_loop(0, (n_rows + sub - 1) // sub, sub_block, 0)


def _expert_ffn(tile_expert, tile_rows, tile_blk, xs, w_gu_bf16, b_gu, w_d_bf16, b_d, *, tm, tn, sub):
    n_slots, d = xs.shape
    n_exp, _, de2 = w_gu_bf16.shape
    de = de2 // 2
    nj = de // tn
    n_tiles = n_slots // tm
    assert de % tn == 0 and n_slots % tm == 0 and tm % sub == 0
    grid_spec = pltpu.PrefetchScalarGridSpec(
        num_scalar_prefetch=3,
        grid=(n_tiles, nj),
        in_specs=[
            pl.BlockSpec((tm, d), lambda i, j, te, tr, tb: (tb[i], 0)),
            pl.BlockSpec((None, d, tn), lambda i, j, te, tr, tb: (te[i], 0, j)),
            pl.BlockSpec((None, d, tn), lambda i, j, te, tr, tb: (te[i], 0, nj + j)),
            pl.BlockSpec((None, 1, tn), lambda i, j, te, tr, tb: (te[i], 0, j)),
            pl.BlockSpec((None, 1, tn), lambda i, j, te, tr, tb: (te[i], 0, nj + j)),
            pl.BlockSpec((None, tn, d), lambda i, j, te, tr, tb: (te[i], j, 0)),
            pl.BlockSpec((None, 1, d), lambda i, j, te, tr, tb: (te[i], 0, 0)),
        ],
        out_specs=pl.BlockSpec((tm, d), lambda i, j, te, tr, tb: (i, 0)),
        scratch_shapes=[pltpu.VMEM((tm, d), _BF16)],
    )
    return pl.pallas_call(
        functools.partial(_ffn_kernel, sub=sub),
        grid_spec=grid_spec,
        out_shape=jax.ShapeDtypeStruct((n_slots, d), _F32),
        compiler_params=_cparams(2),
        name="expert_ffn",
    )(tile_expert, tile_rows, tile_blk, xs, w_gu_bf16, w_gu_bf16,
      b_gu.reshape(n_exp, 1, de2), b_gu.reshape(n_exp, 1, de2), w_d_bf16, b_d.reshape(n_exp, 1, d))


def _tile_plan(counts, *, tm, n_tiles):
    n_exp = counts.shape[0]
    tiles_e = (counts + tm - 1) // tm
    tile_end = jnp.cumsum(tiles_e)
    tile_beg = tile_end - tiles_e
    starts = tile_beg * tm
    n_active = tile_end[-1]
    tid = jnp.minimum(jnp.arange(n_tiles, dtype=jnp.int32), n_active - 1)
    expert = jnp.minimum(jnp.sum(tile_end[None, :] <= tid[:, None], axis=1), n_exp - 1).astype(jnp.int32)
    rows = jnp.clip(counts[expert] - (tid - tile_beg[expert]) * tm, 0, tm)
    rows = jnp.where(jnp.arange(n_tiles) < n_active, rows, 0).astype(jnp.int32)
    pad_beg = jnp.concatenate([starts + counts, (n_active * tm)[None]])
    pad_end = jnp.concatenate([tile_end * tm, jnp.full((1,), n_tiles * tm, jnp.int32)])
    pad_mid = jnp.minimum((pad_beg + ZERO_ROWS - 1) // ZERO_ROWS * ZERO_ROWS, pad_end)
    pads = tuple(p.astype(jnp.int32) for p in (pad_beg, pad_mid, pad_end))
    return starts, expert, rows, tid.astype(jnp.int32), pads


def _layer(x, norm1_g, w_in, q_norm_g, k_norm_g, rpb, sgu_norm_g, w_spatial, b_spatial, attn_out_g,
           sgu_out_g, w_out, norm2_g, w_router, b_router, w_gate_up, b_gate_up, w_down, b_down):
    batch, seq, d = x.shape
    t = batch * seq
    head_dim = q_norm_g.shape[-1]
    n_heads = rpb.shape[0]
    attn_width = n_heads * head_dim
    n_groups, chunk, _ = w_spatial.shape
    sgu_width = sgu_norm_g.shape[-1]
    group_dim = sgu_width // n_groups
    win_rows_max = (rpb.shape[1] + 1) // 2
    win_cols = (rpb.shape[2] + 1) // 2
    rows = seq // GRID_W
    kh = min(win_rows_max, rows)
    n_exp = w_router.shape[-1]
    assert TOP_K <= n_exp <= LANES // 2 and seq % GRID_W == 0 and w_in.shape[1] == 3 * attn_width + 2 * sgu_width

    xf = x.reshape(t, d)
    row = lambda v: v.reshape(1, -1).astype(_F32)

    w_in_b = w_in.astype(_BF16)
    pair = lambda g: jnp.concatenate([g, g]).reshape(1, 1, LANES)
    head_gain = jnp.concatenate([pair(q_norm_g), pair(k_norm_g), jnp.ones((1, 1, LANES), _F32)], axis=0)
    qkv = _norm_proj(xf, row(norm1_g), w_in_b[:, :3 * attn_width], head_gain,
                     slab_out=True, attn_width=attn_width, head_dim=head_dim)
    ug = _norm_proj(xf, row(norm1_g), w_in_b[:, 3 * attn_width:], head_gain,
                    slab_out=False, attn_width=attn_width, head_dim=head_dim)

    bias_tab = _attention_bias_table(rpb, kh=kh, win_cols=win_cols)
    a_out = _attention(qkv, bias_tab, batch=batch, seq=seq, n_heads=n_heads, head_dim=head_dim, kh=kh)

    b_full = jnp.repeat(b_spatial.T, group_dim, axis=1)
    s_out = _sgu(ug, row(sgu_norm_g), w_spatial.astype(_BF16), b_full, row(sgu_out_g),
                 width=sgu_width, chunk=chunk, group_dim=group_dim)

    half = LANES // 2
    w_r_hi = w_router.astype(_BF16)
    w_r_lo = (w_router - w_r_hi.astype(_F32)).astype(_BF16)
    blank = jnp.zeros((d, LANES), _BF16)
    w_router_pad = jnp.concatenate([blank.at[:, :n_exp].set(w_r_hi).at[:, half:half + n_exp].set(w_r_lo),
                                    blank.at[:, :n_exp].set(w_r_hi)], axis=0)
    b_router_pad = jnp.full((1, LANES), NEG_BIG, _F32).at[0, :n_exp].set(b_router)
    x2, xn, idx_pad, gates_pad = _out_router(a_out, s_out, xf, row(attn_out_g), w_out.astype(_BF16),
                                             row(norm2_g), w_router_pad, b_router_pad)

    de = w_down.shape[1]
    tm = min(1024, t)
    tn = min(512, de)
    sub = min(256, tm)
    n_tiles = (t * TOP_K) // tm + n_exp
    counts = _route_counts(idx_pad)[0, :n_exp].astype(jnp.int32)
    assert tm % ZERO_ROWS == 0
    starts, tile_expert, tile_rows, tile_blk, pads = _tile_plan(counts, tm=tm, n_tiles=n_tiles)
    starts_pad = jnp.zeros((1, LANES), _F32).at[0, :n_exp].set(starts.astype(_F32))
    dst_pad = _route_slots(idx_pad, starts_pad)
    dst_flat = dst_pad[:, :TOP_K].reshape(-1)

    xs = _scatter_rows(*pads, dst_flat, xn, n_tiles * tm)
    ys = _expert_ffn(tile_expert, tile_rows, tile_blk, xs, w_gate_up.astype(_BF16), b_gate_up,
                     w_down.astype(_BF16), b_down, tm=tm, tn=tn, sub=sub)
    out = _combine(dst_flat, gates_pad, x2, ys)
    return out.reshape(batch, seq, d)


def kernel(x, norm1_g, w_in, q_norm_g, k_norm_g, rpb, sgu_norm_g, w_spatial, b_spatial, attn_out_g,
           sgu_out_g, w_out, norm2_g, w_router, b_router, w_gate_up, b_gate_up, w_down, b_down):
    depth = norm1_g.shape[0]
    for l in range(depth):
        x = _layer(x, norm1_g[l], w_in[l], q_norm_g[l], k_norm_g[l], rpb[l], sgu_norm_g[l], w_spatial[l],
                   b_spatial[l], attn_out_g[l], sgu_out_g[l], w_out[l], norm2_g[l], w_router[l], b_router[l],
                   w_gate_up[l], b_gate_up[l], w_down[l], b_down[l])
    return x
```

```python
import functools

import jax
import jax.numpy as jnp
import numpy as np
from jax import lax
from jax.experimental import pallas as pl
from jax.experimental.pallas import tpu as pltpu

GRID_W = 64
TOP_K = 4
SWIGLU_LIMIT = 7.0
SWIGLU_ALPHA = 1.702
EPS = 1e-6
LANES = 128
NEG_BIG = -1e30
VMEM_LIMIT_BYTES = 56 * 1024 * 1024

_F32 = jnp.float32
_BF16 = jnp.bfloat16


def _cparams(n_axes):
    return pltpu.CompilerParams(
        dimension_semantics=("arbitrary",) * n_axes,
        vmem_limit_bytes=VMEM_LIMIT_BYTES)


def _rms(x):
    return x * lax.rsqrt(jnp.mean(x * x, axis=-1, keepdims=True) + EPS)


def _gelu(x):
    return 0.5 * x * (1.0 + lax.erf(x * 0.7071067811865476))


def _norm_proj_kernel(x_ref, g_ref, w_ref, hg_ref, o_ref, h_ref, *, slab_out, n_norm_tiles, head_dim):
    j = pl.program_id(1)

    @pl.when(j == 0)
    def _():
        h_ref[...] = (_rms(x_ref[...]) * g_ref[...]).astype(_BF16)

    if not slab_out:
        o_ref[...] = jnp.dot(h_ref[...], w_ref[...], preferred_element_type=_F32)
        return

    left = lax.broadcasted_iota(jnp.int32, (1, LANES), 1) < head_dim
    normed = j < n_norm_tiles
    tn = w_ref.shape[1]
    piece = min(2 * LANES, tn)
    for p in range(tn // piece):
        res = jnp.dot(h_ref[...], w_ref[:, p * piece:(p + 1) * piece], preferred_element_type=_F32)
        for s in range(piece // LANES):
            r = res[:, s * LANES:(s + 1) * LANES]
            ss = r * r
            ss_first = jnp.sum(jnp.where(left, ss, 0.0), axis=-1, keepdims=True)
            ss_second = jnp.sum(jnp.where(left, 0.0, ss), axis=-1, keepdims=True)
            inv = jnp.where(left, lax.rsqrt(ss_first / head_dim + EPS),
                            lax.rsqrt(ss_second / head_dim + EPS)) * hg_ref[0]
            o_ref[p * (piece // LANES) + s] = (r * jnp.where(normed, inv, 1.0)).astype(o_ref.dtype)


def _norm_proj(x, g, w_bf16, head_gain, *, slab_out, attn_width, head_dim):
    t, d = x.shape
    n = w_bf16.shape[1]
    tm = min(1024, t)
    tn = min(1024, attn_width if slab_out else n)
    assert t % tm == 0 and n % tn == 0 and tn % LANES == 0
    if slab_out:
        assert attn_width % tn == 0 and 2 * head_dim == LANES
        tiles_per_part = attn_width // tn
        out_shape = jax.ShapeDtypeStruct((n // LANES, t, LANES), _BF16)
        out_spec = pl.BlockSpec((tn // LANES, tm, LANES), lambda i, j: (j, i, 0))
        hg_spec = pl.BlockSpec((1, 1, LANES), lambda i, j: (jnp.minimum(j // tiles_per_part, 2), 0, 0))
        n_norm_tiles = 2 * tiles_per_part
    else:
        out_shape = jax.ShapeDtypeStruct((t, n), _F32)
        out_spec = pl.BlockSpec((tm, tn), lambda i, j: (i, j))
        hg_spec = pl.BlockSpec((1, 1, LANES), lambda i, j: (0, 0, 0))
        n_norm_tiles = 0
    kern = functools.partial(_norm_proj_kernel, slab_out=slab_out, n_norm_tiles=n_norm_tiles, head_dim=head_dim)
    return pl.pallas_call(
        kern,
        grid=(t // tm, n // tn),
        in_specs=[
            pl.BlockSpec((tm, d), lambda i, j: (i, 0)),
            pl.BlockSpec((1, d), lambda i, j: (0, 0)),
            pl.BlockSpec((d, tn), lambda i, j: (0, j)),
            hg_spec,
        ],
        out_specs=out_spec,
        out_shape=out_shape,
        scratch_shapes=[pltpu.VMEM((tm, d), _BF16)],
        compiler_params=_cparams(2),
        name="norm_proj_qkv" if slab_out else "norm_proj_ug",
    )(x, g, w_bf16, head_gain)


def _attn_kernel(q_ref, kp_ref, kc_ref, kn_ref, vp_ref, vc_ref, vn_ref, bias_ref, o_ref, kwin, vwin,
                 *, rows, rblk, kh, head_dim, scale):
    rb = pl.program_id(2)
    blk = rblk * GRID_W
    for w, (kr, vr) in enumerate(((kp_ref, vp_ref), (kc_ref, vc_ref), (kn_ref, vn_ref))):
        kwin[w * blk:(w + 1) * blk, :] = kr[...]
        vwin[w * blk:(w + 1) * blk, :] = vr[...]
    lane = lax.broadcasted_iota(jnp.int32, (1, LANES), 1)
    left = lane < head_dim
    nk = (kh + 2) * GRID_W
    scores, offsets = [], []
    for pair in range(rblk // 2):
        r = rb * rblk + 2 * pair
        rs_a = jnp.clip(r - kh // 2, 0, rows - kh)
        rs_b = jnp.clip(r + 1 - kh // 2, 0, rows - kh)
        kind = _pair_kind_index(r - rs_a, rs_b - rs_a, kh)
        off = pl.multiple_of((rs_a - (rb * rblk - rblk)) * GRID_W, GRID_W)
        q2 = q_ref[2 * pair * GRID_W:(2 * pair + 2) * GRID_W, :] * scale
        qa, qb = q2[:GRID_W], q2[GRID_W:]
        qm = jnp.concatenate([jnp.where(left, qa, 0.0), jnp.where(left, qb, 0.0),
                              jnp.where(left, 0.0, qa), jnp.where(left, 0.0, qb)], axis=0).astype(_BF16)
        s = lax.dot_general(qm, kwin[pl.ds(off, nk), :], (((1,), (1,)), ((), ())), preferred_element_type=_F32)
        scores.append(s + bias_ref[kind])
        offsets.append(off)
    for pair in range(rblk // 2):
        s = scores[pair]
        p = jnp.exp(s - jnp.max(s, axis=-1, keepdims=True))
        denom = jnp.sum(p, axis=-1, keepdims=True)
        pv = jnp.dot(p.astype(_BF16), vwin[pl.ds(offsets[pair], nk), :], preferred_element_type=_F32) / denom
        o_ref[2 * pair * GRID_W:(2 * pair + 2) * GRID_W, :] = jnp.where(
            left, pv[0:2 * GRID_W], pv[2 * GRID_W:4 * GRID_W])


def _attention(qkv, bias_tab, *, batch, seq, n_heads, head_dim, kh):
    n_hp = n_heads // 2
    rows = seq // GRID_W
    rblk = kh
    assert rows % rblk == 0 and rows >= kh and rblk >= 4
    nrb = rows // rblk
    blk = rblk * GRID_W
    t = batch * seq

    def qmap(b, hp, rb):
        return (hp, b * nrb + rb, 0)

    def kvmap(part, delta):
        def f(b, hp, rb):
            return (part * n_hp + hp, b * nrb + jnp.clip(rb + delta, 0, nrb - 1), 0)
        return f

    slab = lambda m: pl.BlockSpec((None, blk, LANES), m)
    kern = functools.partial(_attn_kernel, rows=rows, rblk=rblk, kh=kh, head_dim=head_dim,
                             scale=float(head_dim) ** -0.5)
    return pl.pallas_call(
        kern,
        grid=(batch, n_hp, nrb),
        in_specs=[
            slab(qmap),
            slab(kvmap(1, -1)), slab(kvmap(1, 0)), slab(kvmap(1, 1)),
            slab(kvmap(2, -1)), slab(kvmap(2, 0)), slab(kvmap(2, 1)),
            pl.BlockSpec((len(_pair_kinds(kh)), None, 4 * GRID_W, (kh + 2) * GRID_W),
                         lambda b, hp, rb: (0, hp, 0, 0)),
        ],
        out_specs=pl.BlockSpec((blk, LANES), lambda b, hp, rb: (b * nrb + rb, hp)),
        out_shape=jax.ShapeDtypeStruct((t, n_hp * LANES), _F32),
        scratch_shapes=[pltpu.VMEM((3 * blk, LANES), _BF16), pltpu.VMEM((3 * blk, LANES), _BF16)],
        compiler_params=_cparams(3),
        name="nbr_attention",
    )(qkv, qkv, qkv, qkv, qkv, qkv, qkv, bias_tab)


def _pair_kinds(kh):
    assert kh % 4 == 0
    return sorted([(s, 0) for s in range(0, kh, 2)] + [(kh // 2, 1)], key=lambda sd: sd[0] + sd[1])


def _pair_kind_index(shift_a, delta, kh):
    code = shift_a + delta
    return code // 2 + (code > kh // 2).astype(jnp.int32)


def _attention_bias_table(rpb, *, kh, win_cols):
    n_heads, n_rpb_rows, n_rpb_cols = rpb.shape
    win_rows_max = (n_rpb_rows + 1) // 2
    nkr = kh + 2
    qc = np.arange(GRID_W)[:, None]
    kc = np.arange(GRID_W)[None, :]
    col_start = np.clip(qc - win_cols // 2, 0, GRID_W - win_cols)
    col_valid = (kc >= col_start) & (kc < col_start + win_cols)
    col_sel = np.zeros((GRID_W, GRID_W, n_rpb_cols), np.float32)
    qi, ki = np.nonzero(col_valid)
    col_sel[qi, ki, ki - qi + win_cols - 1] = 1.0
    kinds = _pair_kinds(kh)
    row_sel = np.zeros((len(kinds), 2, nkr, n_rpb_rows), np.float32)
    row_valid = np.zeros((len(kinds), 2, nkr), bool)
    for kind, (shift_a, delta) in enumerate(kinds):
        for member, (shift, dlt) in enumerate(((shift_a, 0), (shift_a + 1 - delta, delta))):
            for jj in range(nkr):
                if 0 <= jj - dlt < kh:
                    row_sel[kind, member, jj, jj - dlt - shift + win_rows_max - 1] = 1.0
                    row_valid[kind, member, jj] = True
    exact = lax.Precision.HIGHEST
    cols = jnp.einsum("hrc,qkc->rhqk", rpb.astype(_F32), col_sel, precision=exact)
    tab = jnp.einsum("ymjr,rhqk->yhmqjk", row_sel, cols, precision=exact)
    keep = row_valid[:, None, :, None, :, None] & col_valid[None, None, None, :, None, :]
    tab = jnp.where(keep, tab, NEG_BIG)
    return tab.reshape(len(kinds), n_heads // 2, 4 * GRID_W, nkr * GRID_W)


def _sgu_kernel(u_ref, g_ref, ng_ref, w_ref, b_ref, og_ref, o_ref, *, chunk, group_dim):
    tm, width = u_ref.shape
    gg = _gelu(g_ref[...])
    gn = (_rms(gg) * ng_ref[...]).astype(_BF16)
    lane = lax.broadcasted_iota(jnp.int32, (1, LANES), 1)
    left = lane < group_dim
    for c in range(tm // chunk):
        rs = slice(c * chunk, (c + 1) * chunk)
        for gp in range(width // LANES):
            cs = slice(gp * LANES, (gp + 1) * LANES)
            xg = gn[rs, cs]
            ma = jnp.dot(w_ref[2 * gp], xg, preferred_element_type=_F32)
            mb = jnp.dot(w_ref[2 * gp + 1], xg, preferred_element_type=_F32)
            mixed = jnp.where(left, ma, mb) + b_ref[:, cs]
            s = _gelu(u_ref[rs, cs]) * mixed
            o_ref[rs, cs] = s
    s_all = o_ref[...]
    o_ref[...] = _rms(s_all) * og_ref[...]


def _sgu(ug, norm_g, w_sp_bf16, b_full, out_g, *, width, chunk, group_dim):
    t = ug.shape[0]
    tm = min(512, t)
    assert t % tm == 0 and tm % chunk == 0 and width % LANES == 0 and 2 * group_dim == LANES
    n_groups = w_sp_bf16.shape[0]
    kern = functools.partial(_sgu_kernel, chunk=chunk, group_dim=group_dim)
    return pl.pallas_call(
        kern,
        grid=(t // tm,),
        in_specs=[
            pl.BlockSpec((tm, width), lambda i: (i, 0)),
            pl.BlockSpec((tm, width), lambda i: (i, 1)),
            pl.BlockSpec((1, width), lambda i: (0, 0)),
            pl.BlockSpec((n_groups, chunk, chunk), lambda i: (0, 0, 0)),
            pl.BlockSpec((chunk, width), lambda i: (0, 0)),
            pl.BlockSpec((1, width), lambda i: (0, 0)),
        ],
        out_specs=pl.BlockSpec((tm, width), lambda i: (i, 0)),
        out_shape=jax.ShapeDtypeStruct((t, width), _F32),
        compiler_params=_cparams(1),
        name="spatial_gating",
    )(ug, ug, norm_g, w_sp_bf16, b_full, out_g)


ROUTER_ROW_CHUNK = 128


def _packed_words(d):
    assert d % (2 * LANES) == 0
    return d // (2 * LANES)


def _store_packed_rows(ref, row0, x):
    rows, d = x.shape
    words = _packed_words(d)
    for c in range(words):
        lo = x[:, c * LANES:(c + 1) * LANES].astype(_BF16).astype(_F32)
        hi = x[:, (c + words) * LANES:(c + words + 1) * LANES].astype(_BF16).astype(_F32)
        word = (lax.bitcast_convert_type(hi, jnp.uint32)
                | lax.shift_right_logical(lax.bitcast_convert_type(lo, jnp.uint32), jnp.uint32(16)))
        ref[pl.ds(row0 * words + c, rows, stride=words), :] = word


def _load_packed_rows(ref, row0, rows, d):
    words = _packed_words(d)
    lows, highs = [], []
    for c in range(words):
        word = ref[pl.ds(row0 * words + c, rows, stride=words), :]
        lows.append(lax.bitcast_convert_type(lax.shift_left(word, jnp.uint32(16)), _F32).astype(_BF16))
        highs.append(lax.bitcast_convert_type(word & jnp.uint32(0xFFFF0000), _F32).astype(_BF16))
    return jnp.concatenate(lows + highs, axis=-1)


def _out_router_kernel(a_ref, s_ref, x_ref, ag_ref, w_ref, n2g_ref, wr_ref, br_ref,
                       x2_ref, xn_ref, idx_ref, gate_ref):
    chunks = [slice(c * ROUTER_ROW_CHUNK, (c + 1) * ROUTER_ROW_CHUNK)
              for c in range(x_ref.shape[0] // ROUTER_ROW_CHUNK)]
    for rs in chunks:
        an = _rms(a_ref[rs, :]) * ag_ref[...]
        mix = jnp.concatenate([an, s_ref[rs, :]], axis=-1).astype(_BF16)
        x2_ref[rs, :] = x_ref[rs, :] + jnp.dot(mix, w_ref[...], preferred_element_type=_F32)
    all_logits = []
    for rs in chunks:
        xn = _rms(x2_ref[rs, :]) * n2g_ref[...]
        _store_packed_rows(xn_ref, rs.start, xn)
        xh = xn.astype(_BF16)
        xl = (xn - xh.astype(_F32)).astype(_BF16)
        parts = jnp.dot(jnp.concatenate([xh, xl], axis=-1), wr_ref[...], preferred_element_type=_F32)
        all_logits.append(parts + pltpu.roll(parts, LANES // 2, axis=1) + br_ref[...])
    for rs, logits in zip(chunks, all_logits):
        rc = logits.shape[0]
        lane = lax.broadcasted_iota(jnp.int32, (rc, LANES), 1)
        lane_f = lane.astype(_F32)
        idx_acc = jnp.zeros((rc, LANES), _F32)
        vals = []
        cur = logits
        for k in range(TOP_K):
            m = jnp.max(cur, axis=-1, keepdims=True)
            ik = jnp.min(jnp.where(cur == m, lane_f, float(LANES)), axis=-1, keepdims=True)
            vals.append(m)
            idx_acc = jnp.where(lane == k, ik, idx_acc)
            cur = jnp.where(lane_f == ik, -jnp.inf, cur)
        exps = [jnp.exp(v - vals[0]) for v in vals]
        denom = exps[0]
        for e in exps[1:]:
            denom = denom + e
        gate_acc = jnp.zeros((rc, LANES), _F32)
        for k in range(TOP_K):
            gate_acc = jnp.where(lane == k, exps[k] / denom, gate_acc)
        idx_ref[rs, :] = idx_acc.astype(jnp.int32)
        gate_ref[rs, :] = gate_acc


def _out_router(a_out, s_out, x, attn_out_g, w_out_bf16, norm2_g, w_router_pad, b_router_pad):
    t, d = x.shape
    wa = a_out.shape[1]
    ws = s_out.shape[1]
    tm = min(512, t)
    assert t % tm == 0
    words = _packed_words(d)
    row = lambda n: pl.BlockSpec((tm, n), lambda i: (i, 0))
    full = lambda a, b: pl.BlockSpec((a, b), lambda i: (0, 0))
    return pl.pallas_call(
        _out_router_kernel,
        grid=(t // tm,),
        in_specs=[row(wa), row(ws), row(d), full(1, wa), full(wa + ws, d), full(1, d),
                  full(2 * d, LANES), full(1, LANES)],
        out_specs=[row(d), pl.BlockSpec((tm * words, LANES), lambda i: (i, 0)), row(LANES), row(LANES)],
        out_shape=[jax.ShapeDtypeStruct((t, d), _F32), jax.ShapeDtypeStruct((t * words, LANES), jnp.uint32),
                   jax.ShapeDtypeStruct((t, LANES), jnp.int32), jax.ShapeDtypeStruct((t, LANES), _F32)],
        compiler_params=_cparams(1),
        name="out_proj_router",
    )(a_out, s_out, x, attn_out_g, w_out_bf16, norm2_g, w_router_pad, b_router_pad)


def _membership(idx):
    lane = lax.broadcasted_iota(jnp.int32, idx.shape, 1)
    return [lane == idx[:, k:k + 1] for k in range(TOP_K)]


def _count_kernel(idx_ref, cnt_ref):
    @pl.when(pl.program_id(0) == 0)
    def _():
        cnt_ref[...] = jnp.zeros_like(cnt_ref)

    member = sum(oh.astype(_F32) for oh in _membership(idx_ref[...]))
    cnt_ref[...] += jnp.sum(member, axis=0, keepdims=True)


def _slot_kernel(idx_ref, start_ref, dst_ref, carry_ref):
    @pl.when(pl.program_id(0) == 0)
    def _():
        carry_ref[...] = jnp.zeros_like(carry_ref)

    onehots = _membership(idx_ref[...])
    member = sum(oh.astype(_F32) for oh in onehots)
    tb = member.shape[0]
    earlier = (lax.broadcasted_iota(jnp.int32, (tb, tb), 0) > lax.broadcasted_iota(jnp.int32, (tb, tb), 1))
    before = jnp.dot(earlier.astype(_BF16), member.astype(_BF16), preferred_element_type=_F32)
    slot_e = start_ref[...] + carry_ref[...] + before
    lane = lax.broadcasted_iota(jnp.int32, member.shape, 1)
    dst = jnp.zeros(member.shape, _F32)
    for k in range(TOP_K):
        dk = jnp.sum(jnp.where(onehots[k], slot_e, 0.0), axis=-1, keepdims=True)
        dst = jnp.where(lane == k, dk, dst)
    dst_ref[...] = dst.astype(jnp.int32)
    carry_ref[...] += jnp.sum(member, axis=0, keepdims=True)


def _route_counts(idx_pad):
    t = idx_pad.shape[0]
    tb = min(512, t)
    return pl.pallas_call(
        _count_kernel,
        grid=(t // tb,),
        in_specs=[pl.BlockSpec((tb, LANES), lambda i: (i, 0))],
        out_specs=pl.BlockSpec((1, LANES), lambda i: (0, 0)),
        out_shape=jax.ShapeDtypeStruct((1, LANES), _F32),
        compiler_params=_cparams(1),
        name="route_counts",
    )(idx_pad)


def _route_slots(idx_pad, starts_f32):
    t = idx_pad.shape[0]
    tb = min(512, t)
    return pl.pallas_call(
        _slot_kernel,
        grid=(t // tb,),
        in_specs=[pl.BlockSpec((tb, LANES), lambda i: (i, 0)), pl.BlockSpec((1, LANES), lambda i: (0, 0))],
        out_specs=pl.BlockSpec((tb, LANES), lambda i: (i, 0)),
        out_shape=jax.ShapeDtypeStruct((t, LANES), jnp.int32),
        scratch_shapes=[pltpu.VMEM((1, LANES), _F32)],
        compiler_params=_cparams(1),
        name="route_slots",
    )(idx_pad, starts_f32)


ROUTE_TB = 256


ZERO_ROWS = 256


def _scatter_rows_kernel(pad_beg_ref, pad_mid_ref, pad_end_ref, dst_ref, xn_ref, xs_ref, zero_ref, sem, zsem,
                         *, words):
    tb = xn_ref.shape[0] // words
    first = pl.program_id(0) == 0
    n_regions = pad_beg_ref.shape[0]

    def slot(ref, r, n=1):
        start = r * words
        if not isinstance(start, int):
            start = pl.multiple_of(start, words)
        return ref.at[pl.ds(start, n * words)]

    def issue(t, c):
        for k in range(TOP_K):
            d = dst_ref[t * TOP_K + k]
            pltpu.make_async_copy(slot(xn_ref, t), slot(xs_ref, d), sem).start(priority=k % 2)
        return c

    def drain():
        for _ in range(TOP_K):
            pltpu.make_async_copy(xn_ref, slot(xs_ref, 0, tb), sem).wait()

    def clear_padding(start):
        def region(e, c):
            beg, mid, end = pad_beg_ref[e], pad_mid_ref[e], pad_end_ref[e]

            def one_row(r, c2):
                cp = pltpu.make_async_copy(slot(zero_ref, 0), slot(xs_ref, r), zsem)
                cp.start() if start else cp.wait()
                return c2

            def one_block(b, c2):
                r0 = pl.multiple_of(mid + b * ZERO_ROWS, ZERO_ROWS)
                cp = pltpu.make_async_copy(zero_ref, slot(xs_ref, r0, ZERO_ROWS), zsem)
                cp.start() if start else cp.wait()
                return c2

            lax.fori_loop(beg, mid, one_row, 0)
            lax.fori_loop(0, (end - mid) // ZERO_ROWS, one_block, 0)
            return c

        lax.fori_loop(0, n_regions, region, 0)

    @pl.when(first)
    def _():
        zero_ref[...] = jnp.zeros_like(zero_ref)
        clear_padding(True)

    lax.fori_loop(0, tb, issue, 0, unroll=4)
    drain()

    @pl.when(first)
    def _():
        clear_padding(False)


def _scatter_rows(pad_beg, pad_mid, pad_end, dst_flat, xn_packed, n_slots, *, words):
    t = xn_packed.shape[0] // words
    tb = min(ROUTE_TB, t)
    grid_spec = pltpu.PrefetchScalarGridSpec(
        num_scalar_prefetch=3,
        grid=(t // tb,),
        in_specs=[
            pl.BlockSpec((tb * TOP_K,), lambda i, *_: (i,), memory_space=pltpu.SMEM),
            pl.BlockSpec((tb * words, LANES), lambda i, *_: (i, 0)),
        ],
        out_specs=pl.BlockSpec(memory_space=pl.ANY),
        scratch_shapes=[pltpu.VMEM((ZERO_ROWS * words, LANES), jnp.uint32), pltpu.SemaphoreType.DMA(()),
                        pltpu.SemaphoreType.DMA(())],
    )
    return pl.pallas_call(
        functools.partial(_scatter_rows_kernel, words=words),
        grid_spec=grid_spec,
        out_shape=jax.ShapeDtypeStruct((n_slots * words, LANES), jnp.uint32),
        compiler_params=_cparams(1),
        name="scatter_rows",
    )(pad_beg, pad_mid, pad_end, dst_flat, xn_packed)


def _combine_kernel(dst_ref, gate_ref, x2_ref, ys_ref, o_ref, rows_ref, sem):
    tb = x2_ref.shape[0]

    def issue(t, c):
        for k in range(TOP_K):
            d = dst_ref[t * TOP_K + k]
            pltpu.make_async_copy(ys_ref.at[pl.ds(d, 1)], rows_ref.at[k, pl.ds(t, 1)], sem).start(priority=k % 2)
        return c

    lax.fori_loop(0, tb, issue, 0, unroll=4)
    for k in range(TOP_K):
        pltpu.make_async_copy(ys_ref.at[pl.ds(0, tb)], rows_ref.at[k], sem).wait()
    acc = x2_ref[...]
    gates = gate_ref[...]
    for k in range(TOP_K):
        acc = acc + gates[:, k:k + 1] * rows_ref[k]
    o_ref[...] = acc


def _combine(dst_flat, gates_pad, x2, ys):
    t, d = x2.shape
    tb = min(ROUTE_TB, t)
    return pl.pallas_call(
        _combine_kernel,
        grid=(t // tb,),
        in_specs=[
            pl.BlockSpec((tb * TOP_K,), lambda i: (i,), memory_space=pltpu.SMEM),
            pl.BlockSpec((tb, LANES), lambda i: (i, 0)),
            pl.BlockSpec((tb, d), lambda i: (i, 0)),
            pl.BlockSpec(memory_space=pl.ANY),
        ],
        out_specs=pl.BlockSpec((tb, d), lambda i: (i, 0)),
        out_shape=jax.ShapeDtypeStruct((t, d), _F32),
        scratch_shapes=[pltpu.VMEM((TOP_K, tb, d), _F32), pltpu.SemaphoreType.DMA(())],
        compiler_params=_cparams(1),
        name="combine_rows",
    )(dst_flat, gates_pad, x2, ys)


def _ffn_kernel(te_ref, tr_ref, tblk_ref, x_ref, wg_ref, wl_ref, bg_ref, bl_ref, wd_ref, bd_ref,
                o_ref, xb_ref, *, sub):
    del te_ref, tblk_ref
    i = pl.program_id(0)
    j = pl.program_id(1)
    n_rows = tr_ref[i]
    tm, d = o_ref.shape

    @pl.when(jnp.logical_and(j == 0, n_rows > 0))
    def _():
        o_ref[...] = jnp.broadcast_to(bd_ref[...], (tm, d))

    @pl.when(jnp.logical_and(j == 0, n_rows == 0))
    def _():
        o_ref[...] = jnp.zeros((tm, d), _F32)

    def rows_block(r0, size):
        rs = pl.ds(r0, size)

        @pl.when(j == 0)
        def _():
            xb_ref[rs, :] = _load_packed_rows(x_ref, r0, size, d)

        xb = xb_ref[rs, :]
        hg = jnp.dot(xb, wg_ref[...], preferred_element_type=_F32) + bg_ref[...]
        hl = jnp.dot(xb, wl_ref[...], preferred_element_type=_F32) + bl_ref[...]
        gate = jnp.minimum(hg, SWIGLU_LIMIT)
        lin = jnp.clip(hl, -SWIGLU_LIMIT, SWIGLU_LIMIT)
        act = gate * (1.0 / (1.0 + jnp.exp(-SWIGLU_ALPHA * gate))) * (lin + 1.0)
        o_ref[rs, :] += jnp.dot(act.astype(_BF16), wd_ref[...], preferred_element_type=_F32)

    @pl.when(n_rows == tm)
    def _():
        rows_block(0, tm)

    @pl.when(n_rows < tm)
    def _():
        def sub_block(s, c):
            rows_block(pl.multiple_of(s * sub, sub), sub)
            return c

        lax.fori_loop(0, (n_rows + sub - 1) // sub, sub_block, 0)


def _expert_ffn(tile_expert, tile_rows, tile_blk, xs, w_gu_bf16, b_gu, w_d_bf16, b_d, *, tm, tn, sub):
    n_exp, d, de2 = w_gu_bf16.shape
    words = _packed_words(d)
    n_slots = xs.shape[0] // words
    de = de2 // 2
    nj = de // tn
    n_tiles = n_slots // tm
    assert de % tn == 0 and n_slots % tm == 0 and tm % sub == 0
    grid_spec = pltpu.PrefetchScalarGridSpec(
        num_scalar_prefetch=3,
        grid=(n_tiles, nj),
        in_specs=[
            pl.BlockSpec((tm * words, LANES), lambda i, j, te, tr, tb: (tb[i], 0)),
            pl.BlockSpec((None, d, tn), lambda i, j, te, tr, tb: (te[i], 0, j)),
            pl.BlockSpec((None, d, tn), lambda i, j, te, tr, tb: (te[i], 0, nj + j)),
            pl.BlockSpec((None, 1, tn), lambda i, j, te, tr, tb: (te[i], 0, j)),
            pl.BlockSpec((None, 1, tn), lambda i, j, te, tr, tb: (te[i], 0, nj + j)),
            pl.BlockSpec((None, tn, d), lambda i, j, te, tr, tb: (te[i], j, 0)),
            pl.BlockSpec((None, 1, d), lambda i, j, te, tr, tb: (te[i], 0, 0)),
        ],
        out_specs=pl.BlockSpec((tm, d), lambda i, j, te, tr, tb: (i, 0)),
        scratch_shapes=[pltpu.VMEM((tm, d), _BF16)],
    )
    return pl.pallas_call(
        functools.partial(_ffn_kernel, sub=sub),
        grid_spec=grid_spec,
        out_shape=jax.ShapeDtypeStruct((n_slots, d), _F32),
        compiler_params=_cparams(2),
        name="expert_ffn",
    )(tile_expert, tile_rows, tile_blk, xs, w_gu_bf16, w_gu_bf16,
      b_gu.reshape(n_exp, 1, de2), b_gu.reshape(n_exp, 1, de2), w_d_bf16, b_d.reshape(n_exp, 1, d))


def _tile_plan(counts, *, tm, n_tiles):
    n_exp = counts.shape[0]
    tiles_e = (counts + tm - 1) // tm
    tile_end = jnp.cumsum(tiles_e)
    tile_beg = tile_end - tiles_e
    starts = tile_beg * tm
    n_active = tile_end[-1]
    tid = jnp.minimum(jnp.arange(n_tiles, dtype=jnp.int32), n_active - 1)
    expert = jnp.minimum(jnp.sum(tile_end[None, :] <= tid[:, None], axis=1), n_exp - 1).astype(jnp.int32)
    rows = jnp.clip(counts[expert] - (tid - tile_beg[expert]) * tm, 0, tm)
    rows = jnp.where(jnp.arange(n_tiles) < n_active, rows, 0).astype(jnp.int32)
    pad_beg = jnp.concatenate([starts + counts, (n_active * tm)[None]])
    pad_end = jnp.concatenate([tile_end * tm, jnp.full((1,), n_tiles * tm, jnp.int32)])
    pad_mid = jnp.minimum((pad_beg + ZERO_ROWS - 1) // ZERO_ROWS * ZERO_ROWS, pad_end)
    pads = tuple(p.astype(jnp.int32) for p in (pad_beg, pad_mid, pad_end))
    return starts, expert, rows, tid.astype(jnp.int32), pads


def _layer(x, norm1_g, w_in, q_norm_g, k_norm_g, rpb, sgu_norm_g, w_spatial, b_spatial, attn_out_g,
           sgu_out_g, w_out, norm2_g, w_router, b_router, w_gate_up, b_gate_up, w_down, b_down):
    batch, seq, d = x.shape
    t = batch * seq
    head_dim = q_norm_g.shape[-1]
    n_heads = rpb.shape[0]
    attn_width = n_heads * head_dim
    n_groups, chunk, _ = w_spatial.shape
    sgu_width = sgu_norm_g.shape[-1]
    group_dim = sgu_width // n_groups
    win_rows_max = (rpb.shape[1] + 1) // 2
    win_cols = (rpb.shape[2] + 1) // 2
    rows = seq // GRID_W
    kh = min(win_rows_max, rows)
    n_exp = w_router.shape[-1]
    assert TOP_K <= n_exp <= LANES // 2 and seq % GRID_W == 0 and w_in.shape[1] == 3 * attn_width + 2 * sgu_width

    xf = x.reshape(t, d)
    row = lambda v: v.reshape(1, -1).astype(_F32)

    w_in_b = w_in.astype(_BF16)
    pair = lambda g: jnp.concatenate([g, g]).reshape(1, 1, LANES)
    head_gain = jnp.concatenate([pair(q_norm_g), pair(k_norm_g), jnp.ones((1, 1, LANES), _F32)], axis=0)
    qkv = _norm_proj(xf, row(norm1_g), w_in_b[:, :3 * attn_width], head_gain,
                     slab_out=True, attn_width=attn_width, head_dim=head_dim)
    ug = _norm_proj(xf, row(norm1_g), w_in_b[:, 3 * attn_width:], head_gain,
                    slab_out=False, attn_width=attn_width, head_dim=head_dim)

    bias_tab = _attention_bias_table(rpb, kh=kh, win_cols=win_cols)
    a_out = _attention(qkv, bias_tab, batch=batch, seq=seq, n_heads=n_heads, head_dim=head_dim, kh=kh)

    b_full = jnp.repeat(b_spatial.T, group_dim, axis=1)
    s_out = _sgu(ug, row(sgu_norm_g), w_spatial.astype(_BF16), b_full, row(sgu_out_g),
                 width=sgu_width, chunk=chunk, group_dim=group_dim)

    half = LANES // 2
    w_r_hi = w_router.astype(_BF16)
    w_r_lo = (w_router - w_r_hi.astype(_F32)).astype(_BF16)
    blank = jnp.zeros((d, LANES), _BF16)
    w_router_pad = jnp.concatenate([blank.at[:, :n_exp].set(w_r_hi).at[:, half:half + n_exp].set(w_r_lo),
                                    blank.at[:, :n_exp].set(w_r_hi)], axis=0)
    b_router_pad = jnp.full((1, LANES), NEG_BIG, _F32).at[0, :n_exp].set(b_router)
    x2, xn, idx_pad, gates_pad = _out_router(a_out, s_out, xf, row(attn_out_g), w_out.astype(_BF16),
                                             row(norm2_g), w_router_pad, b_router_pad)

    de = w_down.shape[1]
    tm = min(1024, t)
    tn = min(512, de)
    sub = min(256, tm)
    n_tiles = (t * TOP_K) // tm + n_exp
    counts = _route_counts(idx_pad)[0, :n_exp].astype(jnp.int32)
    assert tm % ZERO_ROWS == 0
    starts, tile_expert, tile_rows, tile_blk, pads = _tile_plan(counts, tm=tm, n_tiles=n_tiles)
    starts_pad = jnp.zeros((1, LANES), _F32).at[0, :n_exp].set(starts.astype(_F32))
    dst_pad = _route_slots(idx_pad, starts_pad)
    dst_flat = dst_pad[:, :TOP_K].reshape(-1)

    xs = _scatter_rows(*pads, dst_flat, xn, n_tiles * tm, words=_packed_words(d))
    ys = _expert_ffn(tile_expert, tile_rows, tile_blk, xs, w_gate_up.astype(_BF16), b_gate_up,
                     w_down.astype(_BF16), b_down, tm=tm, tn=tn, sub=sub)
    out = _combine(dst_flat, gates_pad, x2, ys)
    return out.reshape(batch, seq, d)


def kernel(x, norm1_g, w_in, q_norm_g, k_norm_g, rpb, sgu_norm_g, w_spatial, b_spatial, attn_out_g,
           sgu_out_g, w_out, norm2_g, w_router, b_router, w_gate_up, b_gate_up, w_down, b_down):
    depth = norm1_g.shape[0]
    for l in range(depth):
        x = _layer(x, norm1_g[l], w_in[l], q_norm_g[l], k_norm_g[l], rpb[l], sgu_norm_g[l], w_spatial[l],
                   b_spatial[l], attn_out_g[l], sgu_out_g[l], w_out[l], norm2_g[l], w_router[l], b_router[l],
                   w_gate_up[l], b_gate_up[l], w_down[l], b_down[l])
    return x
```

```python
import functools

import jax
import jax.numpy as jnp
import numpy as np
from jax import lax
from jax.experimental import pallas as pl
from jax.experimental.pallas import tpu as pltpu

GRID_W = 64
TOP_K = 4
SWIGLU_LIMIT = 7.0
SWIGLU_ALPHA = 1.702
EPS = 1e-6
LANES = 128
NEG_BIG = -1e30
VMEM_LIMIT_BYTES = 56 * 1024 * 1024

_F32 = jnp.float32
_BF16 = jnp.bfloat16


def _cparams(n_axes):
    return pltpu.CompilerParams(
        dimension_semantics=("arbitrary",) * n_axes,
        vmem_limit_bytes=VMEM_LIMIT_BYTES)


def _rms(x):
    return x * lax.rsqrt(jnp.mean(x * x, axis=-1, keepdims=True) + EPS)


def _gelu(x):
    return 0.5 * x * (1.0 + lax.erf(x * 0.7071067811865476))


def _norm_proj_kernel(x_ref, g_ref, w_ref, hg_ref, o_ref, h_ref, *, slab_out, n_norm_tiles, head_dim):
    j = pl.program_id(1)

    @pl.when(j == 0)
    def _():
        h_ref[...] = (_rms(x_ref[...]) * g_ref[...]).astype(_BF16)

    if not slab_out:
        o_ref[...] = jnp.dot(h_ref[...], w_ref[...], preferred_element_type=_F32)
        return

    left = lax.broadcasted_iota(jnp.int32, (1, LANES), 1) < head_dim
    normed = j < n_norm_tiles
    tn = w_ref.shape[1]
    piece = min(2 * LANES, tn)
    for p in range(tn // piece):
        res = jnp.dot(h_ref[...], w_ref[:, p * piece:(p + 1) * piece], preferred_element_type=_F32)
        for s in range(piece // LANES):
            r = res[:, s * LANES:(s + 1) * LANES]
            ss = r * r
            ss_first = jnp.sum(jnp.where(left, ss, 0.0), axis=-1, keepdims=True)
            ss_second = jnp.sum(jnp.where(left, 0.0, ss), axis=-1, keepdims=True)
            inv = jnp.where(left, lax.rsqrt(ss_first / head_dim + EPS),
                            lax.rsqrt(ss_second / head_dim + EPS)) * hg_ref[0]
            o_ref[p * (piece // LANES) + s] = (r * jnp.where(normed, inv, 1.0)).astype(o_ref.dtype)


def _norm_proj(x, g, w_bf16, head_gain, *, slab_out, attn_width, head_dim):
    t, d = x.shape
    n = w_bf16.shape[1]
    tm = min(1024, t)
    tn = min(1024, attn_width if slab_out else n)
    assert t % tm == 0 and n % tn == 0 and tn % LANES == 0
    if slab_out:
        assert attn_width % tn == 0 and 2 * head_dim == LANES
        tiles_per_part = attn_width // tn
        out_shape = jax.ShapeDtypeStruct((n // LANES, t, LANES), _BF16)
        out_spec = pl.BlockSpec((tn // LANES, tm, LANES), lambda i, j: (j, i, 0))
        hg_spec = pl.BlockSpec((1, 1, LANES), lambda i, j: (jnp.minimum(j // tiles_per_part, 2), 0, 0))
        n_norm_tiles = 2 * tiles_per_part
    else:
        out_shape = jax.ShapeDtypeStruct((t, n), _F32)
        out_spec = pl.BlockSpec((tm, tn), lambda i, j: (i, j))
        hg_spec = pl.BlockSpec((1, 1, LANES), lambda i, j: (0, 0, 0))
        n_norm_tiles = 0
    kern = functools.partial(_norm_proj_kernel, slab_out=slab_out, n_norm_tiles=n_norm_tiles, head_dim=head_dim)
    return pl.pallas_call(
        kern,
        grid=(t // tm, n // tn),
        in_specs=[
            pl.BlockSpec((tm, d), lambda i, j: (i, 0)),
            pl.BlockSpec((1, d), lambda i, j: (0, 0)),
            pl.BlockSpec((d, tn), lambda i, j: (0, j)),
            hg_spec,
        ],
        out_specs=out_spec,
        out_shape=out_shape,
        scratch_shapes=[pltpu.VMEM((tm, d), _BF16)],
        compiler_params=_cparams(2),
        name="norm_proj_qkv" if slab_out else "norm_proj_ug",
    )(x, g, w_bf16, head_gain)


def _attn_kernel(q_ref, kp_ref, kc_ref, kn_ref, vp_ref, vc_ref, vn_ref, bias_ref, o_ref, kwin, vwin,
                 *, rows, rblk, kh, head_dim, scale):
    rb = pl.program_id(2)
    blk = rblk * GRID_W
    for w, (kr, vr) in enumerate(((kp_ref, vp_ref), (kc_ref, vc_ref), (kn_ref, vn_ref))):
        kwin[w * blk:(w + 1) * blk, :] = kr[...]
        vwin[w * blk:(w + 1) * blk, :] = vr[...]
    lane = lax.broadcasted_iota(jnp.int32, (1, LANES), 1)
    left = lane < head_dim
    nk = (kh + 2) * GRID_W
    scores, offsets = [], []
    for pair in range(rblk // 2):
        r = rb * rblk + 2 * pair
        rs_a = jnp.clip(r - kh // 2, 0, rows - kh)
        rs_b = jnp.clip(r + 1 - kh // 2, 0, rows - kh)
        kind = _pair_kind_index(r - rs_a, rs_b - rs_a, kh)
        off = pl.multiple_of((rs_a - (rb * rblk - rblk)) * GRID_W, GRID_W)
        q2 = q_ref[2 * pair * GRID_W:(2 * pair + 2) * GRID_W, :] * scale
        qa, qb = q2[:GRID_W], q2[GRID_W:]
        qm = jnp.concatenate([jnp.where(left, qa, 0.0), jnp.where(left, qb, 0.0),
                              jnp.where(left, 0.0, qa), jnp.where(left, 0.0, qb)], axis=0).astype(_BF16)
        s = lax.dot_general(qm, kwin[pl.ds(off, nk), :], (((1,), (1,)), ((), ())), preferred_element_type=_F32)
        scores.append(s + bias_ref[kind])
        offsets.append(off)
    for pair in range(rblk // 2):
        s = scores[pair]
        p = jnp.exp(s - jnp.max(s, axis=-1, keepdims=True))
        denom = jnp.sum(p, axis=-1, keepdims=True)
        pv = jnp.dot(p.astype(_BF16), vwin[pl.ds(offsets[pair], nk), :], preferred_element_type=_F32) / denom
        o_ref[2 * pair * GRID_W:(2 * pair + 2) * GRID_W, :] = jnp.where(
            left, pv[0:2 * GRID_W], pv[2 * GRID_W:4 * GRID_W])


def _attention(qkv, bias_tab, *, batch, seq, n_heads, head_dim, kh):
    n_hp = n_heads // 2
    rows = seq // GRID_W
    rblk = kh
    assert rows % rblk == 0 and rows >= kh and rblk >= 4
    nrb = rows // rblk
    blk = rblk * GRID_W
    t = batch * seq

    def qmap(b, hp, rb):
        return (hp, b * nrb + rb, 0)

    def kvmap(part, delta):
        def f(b, hp, rb):
            return (part * n_hp + hp, b * nrb + jnp.clip(rb + delta, 0, nrb - 1), 0)
        return f

    slab = lambda m: pl.BlockSpec((None, blk, LANES), m)
    kern = functools.partial(_attn_kernel, rows=rows, rblk=rblk, kh=kh, head_dim=head_dim,
                             scale=float(head_dim) ** -0.5)
    return pl.pallas_call(
        kern,
        grid=(batch, n_hp, nrb),
        in_specs=[
            slab(qmap),
            slab(kvmap(1, -1)), slab(kvmap(1, 0)), slab(kvmap(1, 1)),
            slab(kvmap(2, -1)), slab(kvmap(2, 0)), slab(kvmap(2, 1)),
            pl.BlockSpec((len(_pair_kinds(kh)), None, 4 * GRID_W, (kh + 2) * GRID_W),
                         lambda b, hp, rb: (0, hp, 0, 0)),
        ],
        out_specs=pl.BlockSpec((blk, LANES), lambda b, hp, rb: (b * nrb + rb, hp)),
        out_shape=jax.ShapeDtypeStruct((t, n_hp * LANES), _F32),
        scratch_shapes=[pltpu.VMEM((3 * blk, LANES), _BF16), pltpu.VMEM((3 * blk, LANES), _BF16)],
        compiler_params=_cparams(3),
        name="nbr_attention",
    )(qkv, qkv, qkv, qkv, qkv, qkv, qkv, bias_tab)


def _pair_kinds(kh):
    assert kh % 4 == 0
    return sorted([(s, 0) for s in range(0, kh, 2)] + [(kh // 2, 1)], key=lambda sd: sd[0] + sd[1])


def _pair_kind_index(shift_a, delta, kh):
    code = shift_a + delta
    return code // 2 + (code > kh // 2).astype(jnp.int32)


def _attention_bias_table(rpb, *, kh, win_cols):
    n_heads, n_rpb_rows, n_rpb_cols = rpb.shape
    win_rows_max = (n_rpb_rows + 1) // 2
    nkr = kh + 2
    qc = np.arange(GRID_W)[:, None]
    kc = np.arange(GRID_W)[None, :]
    col_start = np.clip(qc - win_cols // 2, 0, GRID_W - win_cols)
    col_valid = (kc >= col_start) & (kc < col_start + win_cols)
    col_sel = np.zeros((GRID_W, GRID_W, n_rpb_cols), np.float32)
    qi, ki = np.nonzero(col_valid)
    col_sel[qi, ki, ki - qi + win_cols - 1] = 1.0
    kinds = _pair_kinds(kh)
    row_sel = np.zeros((len(kinds), 2, nkr, n_rpb_rows), np.float32)
    row_valid = np.zeros((len(kinds), 2, nkr), bool)
    for kind, (shift_a, delta) in enumerate(kinds):
        for member, (shift, dlt) in enumerate(((shift_a, 0), (shift_a + 1 - delta, delta))):
            for jj in range(nkr):
                if 0 <= jj - dlt < kh:
                    row_sel[kind, member, jj, jj - dlt - shift + win_rows_max - 1] = 1.0
                    row_valid[kind, member, jj] = True
    exact = lax.Precision.HIGHEST
    cols = jnp.einsum("hrc,qkc->rhqk", rpb.astype(_F32), col_sel, precision=exact)
    tab = jnp.einsum("ymjr,rhqk->yhmqjk", row_sel, cols, precision=exact)
    keep = row_valid[:, None, :, None, :, None] & col_valid[None, None, None, :, None, :]
    tab = jnp.where(keep, tab, NEG_BIG)
    return tab.reshape(len(kinds), n_heads // 2, 4 * GRID_W, nkr * GRID_W)


def _sgu_kernel(u_ref, g_ref, ng_ref, w_ref, b_ref, og_ref, o_ref, *, chunk, group_dim):
    tm, width = u_ref.shape
    gg = _gelu(g_ref[...])
    gn = (_rms(gg) * ng_ref[...]).astype(_BF16)
    lane = lax.broadcasted_iota(jnp.int32, (1, LANES), 1)
    left = lane < group_dim
    for c in range(tm // chunk):
        rs = slice(c * chunk, (c + 1) * chunk)
        for gp in range(width // LANES):
            cs = slice(gp * LANES, (gp + 1) * LANES)
            xg = gn[rs, cs]
            ma = jnp.dot(w_ref[2 * gp], xg, preferred_element_type=_F32)
            mb = jnp.dot(w_ref[2 * gp + 1], xg, preferred_element_type=_F32)
            mixed = jnp.where(left, ma, mb) + b_ref[:, cs]
            s = _gelu(u_ref[rs, cs]) * mixed
            o_ref[rs, cs] = s
    s_all = o_ref[...]
    o_ref[...] = _rms(s_all) * og_ref[...]


def _sgu(ug, norm_g, w_sp_bf16, b_full, out_g, *, width, chunk, group_dim):
    t = ug.shape[0]
    tm = min(512, t)
    assert t % tm == 0 and tm % chunk == 0 and width % LANES == 0 and 2 * group_dim == LANES
    n_groups = w_sp_bf16.shape[0]
    kern = functools.partial(_sgu_kernel, chunk=chunk, group_dim=group_dim)
    return pl.pallas_call(
        kern,
        grid=(t // tm,),
        in_specs=[
            pl.BlockSpec((tm, width), lambda i: (i, 0)),
            pl.BlockSpec((tm, width), lambda i: (i, 1)),
            pl.BlockSpec((1, width), lambda i: (0, 0)),
            pl.BlockSpec((n_groups, chunk, chunk), lambda i: (0, 0, 0)),
            pl.BlockSpec((chunk, width), lambda i: (0, 0)),
            pl.BlockSpec((1, width), lambda i: (0, 0)),
        ],
        out_specs=pl.BlockSpec((tm, width), lambda i: (i, 0)),
        out_shape=jax.ShapeDtypeStruct((t, width), _F32),
        compiler_params=_cparams(1),
        name="spatial_gating",
    )(ug, ug, norm_g, w_sp_bf16, b_full, out_g)


ROUTER_ROW_CHUNK = 128


def _packed_words(d):
    assert d % (2 * LANES) == 0
    return d // (2 * LANES)


def _store_packed_rows(ref, row0, x):
    rows, d = x.shape
    words = _packed_words(d)
    for c in range(words):
        lo = x[:, c * LANES:(c + 1) * LANES].astype(_BF16).astype(_F32)
        hi = x[:, (c + words) * LANES:(c + words + 1) * LANES].astype(_BF16).astype(_F32)
        word = (lax.bitcast_convert_type(hi, jnp.uint32)
                | lax.shift_right_logical(lax.bitcast_convert_type(lo, jnp.uint32), jnp.uint32(16)))
        ref[pl.ds(row0 * words + c, rows, stride=words), :] = word


def _load_packed_rows(ref, row0, rows, d, dtype):
    words = _packed_words(d)
    lows, highs = [], []
    for c in range(words):
        word = ref[pl.ds(row0 * words + c, rows, stride=words), :]
        lows.append(lax.bitcast_convert_type(lax.shift_left(word, jnp.uint32(16)), _F32).astype(dtype))
        highs.append(lax.bitcast_convert_type(word & jnp.uint32(0xFFFF0000), _F32).astype(dtype))
    return jnp.concatenate(lows + highs, axis=-1)


def _out_router_kernel(a_ref, s_ref, x_ref, ag_ref, w_ref, n2g_ref, wr_ref, br_ref,
                       x2_ref, xn_ref, idx_ref, gate_ref):
    chunks = [slice(c * ROUTER_ROW_CHUNK, (c + 1) * ROUTER_ROW_CHUNK)
              for c in range(x_ref.shape[0] // ROUTER_ROW_CHUNK)]
    for rs in chunks:
        an = _rms(a_ref[rs, :]) * ag_ref[...]
        mix = jnp.concatenate([an, s_ref[rs, :]], axis=-1).astype(_BF16)
        x2_ref[rs, :] = x_ref[rs, :] + jnp.dot(mix, w_ref[...], preferred_element_type=_F32)
    all_logits = []
    for rs in chunks:
        xn = _rms(x2_ref[rs, :]) * n2g_ref[...]
        _store_packed_rows(xn_ref, rs.start, xn)
        xh = xn.astype(_BF16)
        xl = (xn - xh.astype(_F32)).astype(_BF16)
        parts = jnp.dot(jnp.concatenate([xh, xl], axis=-1), wr_ref[...], preferred_element_type=_F32)
        all_logits.append(parts + pltpu.roll(parts, LANES // 2, axis=1) + br_ref[...])
    for rs, logits in zip(chunks, all_logits):
        rc = logits.shape[0]
        lane = lax.broadcasted_iota(jnp.int32, (rc, LANES), 1)
        lane_f = lane.astype(_F32)
        idx_acc = jnp.zeros((rc, LANES), _F32)
        vals = []
        cur = logits
        for k in range(TOP_K):
            m = jnp.max(cur, axis=-1, keepdims=True)
            ik = jnp.min(jnp.where(cur == m, lane_f, float(LANES)), axis=-1, keepdims=True)
            vals.append(m)
            idx_acc = jnp.where(lane == k, ik, idx_acc)
            cur = jnp.where(lane_f == ik, -jnp.inf, cur)
        exps = [jnp.exp(v - vals[0]) for v in vals]
        denom = exps[0]
        for e in exps[1:]:
            denom = denom + e
        gate_acc = jnp.zeros((rc, LANES), _F32)
        for k in range(TOP_K):
            gate_acc = jnp.where(lane == k, exps[k] / denom, gate_acc)
        idx_ref[rs, :] = idx_acc.astype(jnp.int32)
        gate_ref[rs, :] = gate_acc


def _out_router(a_out, s_out, x, attn_out_g, w_out_bf16, norm2_g, w_router_pad, b_router_pad):
    t, d = x.shape
    wa = a_out.shape[1]
    ws = s_out.shape[1]
    tm = min(512, t)
    assert t % tm == 0
    words = _packed_words(d)
    row = lambda n: pl.BlockSpec((tm, n), lambda i: (i, 0))
    full = lambda a, b: pl.BlockSpec((a, b), lambda i: (0, 0))
    return pl.pallas_call(
        _out_router_kernel,
        grid=(t // tm,),
        in_specs=[row(wa), row(ws), row(d), full(1, wa), full(wa + ws, d), full(1, d),
                  full(2 * d, LANES), full(1, LANES)],
        out_specs=[row(d), pl.BlockSpec((tm * words, LANES), lambda i: (i, 0)), row(LANES), row(LANES)],
        out_shape=[jax.ShapeDtypeStruct((t, d), _F32), jax.ShapeDtypeStruct((t * words, LANES), jnp.uint32),
                   jax.ShapeDtypeStruct((t, LANES), jnp.int32), jax.ShapeDtypeStruct((t, LANES), _F32)],
        compiler_params=_cparams(1),
        name="out_proj_router",
    )(a_out, s_out, x, attn_out_g, w_out_bf16, norm2_g, w_router_pad, b_router_pad)


def _membership(idx):
    lane = lax.broadcasted_iota(jnp.int32, idx.shape, 1)
    return [lane == idx[:, k:k + 1] for k in range(TOP_K)]


def _count_kernel(idx_ref, cnt_ref):
    @pl.when(pl.program_id(0) == 0)
    def _():
        cnt_ref[...] = jnp.zeros_like(cnt_ref)

    member = sum(oh.astype(_F32) for oh in _membership(idx_ref[...]))
    cnt_ref[...] += jnp.sum(member, axis=0, keepdims=True)


def _slot_kernel(idx_ref, start_ref, dst_ref, carry_ref):
    @pl.when(pl.program_id(0) == 0)
    def _():
        carry_ref[...] = jnp.zeros_like(carry_ref)

    onehots = _membership(idx_ref[...])
    member = sum(oh.astype(_F32) for oh in onehots)
    tb = member.shape[0]
    earlier = (lax.broadcasted_iota(jnp.int32, (tb, tb), 0) > lax.broadcasted_iota(jnp.int32, (tb, tb), 1))
    before = jnp.dot(earlier.astype(_BF16), member.astype(_BF16), preferred_element_type=_F32)
    slot_e = start_ref[...] + carry_ref[...] + before
    lane = lax.broadcasted_iota(jnp.int32, member.shape, 1)
    dst = jnp.zeros(member.shape, _F32)
    for k in range(TOP_K):
        dk = jnp.sum(jnp.where(onehots[k], slot_e, 0.0), axis=-1, keepdims=True)
        dst = jnp.where(lane == k, dk, dst)
    dst_ref[...] = dst.astype(jnp.int32)
    carry_ref[...] += jnp.sum(member, axis=0, keepdims=True)


def _route_counts(idx_pad):
    t = idx_pad.shape[0]
    tb = min(512, t)
    return pl.pallas_call(
        _count_kernel,
        grid=(t // tb,),
        in_specs=[pl.BlockSpec((tb, LANES), lambda i: (i, 0))],
        out_specs=pl.BlockSpec((1, LANES), lambda i: (0, 0)),
        out_shape=jax.ShapeDtypeStruct((1, LANES), _F32),
        compiler_params=_cparams(1),
        name="route_counts",
    )(idx_pad)


def _route_slots(idx_pad, starts_f32):
    t = idx_pad.shape[0]
    tb = min(512, t)
    return pl.pallas_call(
        _slot_kernel,
        grid=(t // tb,),
        in_specs=[pl.BlockSpec((tb, LANES), lambda i: (i, 0)), pl.BlockSpec((1, LANES), lambda i: (0, 0))],
        out_specs=pl.BlockSpec((tb, LANES), lambda i: (i, 0)),
        out_shape=jax.ShapeDtypeStruct((t, LANES), jnp.int32),
        scratch_shapes=[pltpu.VMEM((1, LANES), _F32)],
        compiler_params=_cparams(1),
        name="route_slots",
    )(idx_pad, starts_f32)


ROUTE_TB = 256


ZERO_ROWS = 256


def _scatter_rows_kernel(pad_beg_ref, pad_mid_ref, pad_end_ref, dst_ref, xn_ref, xs_ref, zero_ref, sem, zsem,
                         *, words):
    tb = xn_ref.shape[0] // words
    first = pl.program_id(0) == 0
    n_regions = pad_beg_ref.shape[0]

    def slot(ref, r, n=1):
        start = r * words
        if not isinstance(start, int):
            start = pl.multiple_of(start, words)
        return ref.at[pl.ds(start, n * words)]

    def issue(t, c):
        for k in range(TOP_K):
            d = dst_ref[t * TOP_K + k]
            pltpu.make_async_copy(slot(xn_ref, t), slot(xs_ref, d), sem).start(priority=k % 2)
        return c

    def drain():
        for _ in range(TOP_K):
            pltpu.make_async_copy(xn_ref, slot(xs_ref, 0, tb), sem).wait()

    def clear_padding(start):
        def region(e, c):
            beg, mid, end = pad_beg_ref[e], pad_mid_ref[e], pad_end_ref[e]

            def one_row(r, c2):
                cp = pltpu.make_async_copy(slot(zero_ref, 0), slot(xs_ref, r), zsem)
                cp.start() if start else cp.wait()
                return c2

            def one_block(b, c2):
                r0 = pl.multiple_of(mid + b * ZERO_ROWS, ZERO_ROWS)
                cp = pltpu.make_async_copy(zero_ref, slot(xs_ref, r0, ZERO_ROWS), zsem)
                cp.start() if start else cp.wait()
                return c2

            lax.fori_loop(beg, mid, one_row, 0)
            lax.fori_loop(0, (end - mid) // ZERO_ROWS, one_block, 0)
            return c

        lax.fori_loop(0, n_regions, region, 0)

    @pl.when(first)
    def _():
        zero_ref[...] = jnp.zeros_like(zero_ref)
        clear_padding(True)

    lax.fori_loop(0, tb, issue, 0, unroll=4)
    drain()

    @pl.when(first)
    def _():
        clear_padding(False)


def _scatter_rows(pad_beg, pad_mid, pad_end, dst_flat, xn_packed, n_slots, *, words):
    t = xn_packed.shape[0] // words
    tb = min(ROUTE_TB, t)
    grid_spec = pltpu.PrefetchScalarGridSpec(
        num_scalar_prefetch=3,
        grid=(t // tb,),
        in_specs=[
            pl.BlockSpec((tb * TOP_K,), lambda i, *_: (i,), memory_space=pltpu.SMEM),
            pl.BlockSpec((tb * words, LANES), lambda i, *_: (i, 0)),
        ],
        out_specs=pl.BlockSpec(memory_space=pl.ANY),
        scratch_shapes=[pltpu.VMEM((ZERO_ROWS * words, LANES), jnp.uint32), pltpu.SemaphoreType.DMA(()),
                        pltpu.SemaphoreType.DMA(())],
    )
    return pl.pallas_call(
        functools.partial(_scatter_rows_kernel, words=words),
        grid_spec=grid_spec,
        out_shape=jax.ShapeDtypeStruct((n_slots * words, LANES), jnp.uint32),
        compiler_params=_cparams(1),
        name="scatter_rows",
    )(pad_beg, pad_mid, pad_end, dst_flat, xn_packed)


def _combine_kernel(dst_ref, gate_ref, x2_ref, ys_ref, o_ref, rows_ref, sem, *, words):
    tb, d = x2_ref.shape

    def issue(t, c):
        for k in range(TOP_K):
            src = pl.multiple_of(dst_ref[t * TOP_K + k] * words, words)
            pltpu.make_async_copy(ys_ref.at[pl.ds(src, words)],
                                  rows_ref.at[k, pl.ds(pl.multiple_of(t * words, words), words)],
                                  sem).start(priority=k % 2)
        return c

    lax.fori_loop(0, tb, issue, 0, unroll=4)
    for k in range(TOP_K):
        pltpu.make_async_copy(ys_ref.at[pl.ds(0, tb * words)], rows_ref.at[k], sem).wait()
    acc = x2_ref[...]
    gates = gate_ref[...]
    for k in range(TOP_K):
        acc = acc + gates[:, k:k + 1] * _load_packed_rows(rows_ref.at[k], 0, tb, d, _F32)
    o_ref[...] = acc


def _combine(dst_flat, gates_pad, x2, ys):
    t, d = x2.shape
    tb = min(ROUTE_TB, t)
    words = _packed_words(d)
    return pl.pallas_call(
        functools.partial(_combine_kernel, words=words),
        grid=(t // tb,),
        in_specs=[
            pl.BlockSpec((tb * TOP_K,), lambda i: (i,), memory_space=pltpu.SMEM),
            pl.BlockSpec((tb, LANES), lambda i: (i, 0)),
            pl.BlockSpec((tb, d), lambda i: (i, 0)),
            pl.BlockSpec(memory_space=pl.ANY),
        ],
        out_specs=pl.BlockSpec((tb, d), lambda i: (i, 0)),
        out_shape=jax.ShapeDtypeStruct((t, d), _F32),
        scratch_shapes=[pltpu.VMEM((TOP_K, tb * words, LANES), jnp.uint32), pltpu.SemaphoreType.DMA(())],
        compiler_params=_cparams(1),
        name="combine_rows",
    )(dst_flat, gates_pad, x2, ys)


def _ffn_kernel(te_ref, tr_ref, tblk_ref, x_ref, wg_ref, wl_ref, bg_ref, bl_ref, wd_ref, bd_ref,
                o_ref, xb_ref, acc_ref, *, sub):
    del te_ref, tblk_ref
    i = pl.program_id(0)
    j = pl.program_id(1)
    nj = pl.num_programs(1)
    n_rows = tr_ref[i]
    tm, d = acc_ref.shape

    @pl.when(jnp.logical_and(j == 0, n_rows > 0))
    def _():
        acc_ref[...] = jnp.broadcast_to(bd_ref[...], (tm, d))

    @pl.when(jnp.logical_and(j == 0, n_rows < tm))
    def _():
        o_ref[...] = jnp.zeros(o_ref.shape, o_ref.dtype)

    def rows_block(r0, size, last):
        rs = pl.ds(r0, size)

        @pl.when(j == 0)
        def _():
            xb_ref[rs, :] = _load_packed_rows(x_ref, r0, size, d, _BF16)

        xb = xb_ref[rs, :]
        hg = jnp.dot(xb, wg_ref[...], preferred_element_type=_F32) + bg_ref[...]
        hl = jnp.dot(xb, wl_ref[...], preferred_element_type=_F32) + bl_ref[...]
        gate = jnp.minimum(hg, SWIGLU_LIMIT)
        lin = jnp.clip(hl, -SWIGLU_LIMIT, SWIGLU_LIMIT)
        act = gate * (1.0 / (1.0 + jnp.exp(-SWIGLU_ALPHA * gate))) * (lin + 1.0)
        total = acc_ref[rs, :] + jnp.dot(act.astype(_BF16), wd_ref[...], preferred_element_type=_F32)
        if last:
            _store_packed_rows(o_ref, r0, total)
        else:
            acc_ref[rs, :] = total

    def tile(last):
        @pl.when(n_rows == tm)
        def _():
            rows_block(0, tm, last)

        @pl.when(n_rows < tm)
        def _():
            def sub_block(s, c):
                rows_block(pl.multiple_of(s * sub, sub), sub, last)
                return c

            lax.fori_loop(0, (n_rows + sub - 1) // sub, sub_block, 0)

    @pl.when(j < nj - 1)
    def _():
        tile(False)

    @pl.when(j == nj - 1)
    def _():
        tile(True)


def _expert_ffn(tile_expert, tile_rows, tile_blk, xs, w_gu_bf16, b_gu, w_d_bf16, b_d, *, tm, tn, sub):
    n_exp, d, de2 = w_gu_bf16.shape
    words = _packed_words(d)
    n_slots = xs.shape[0] // words
    de = de2 // 2
    nj = de // tn
    n_tiles = n_slots // tm
    assert de % tn == 0 and n_slots % tm == 0 and tm % sub == 0
    grid_spec = pltpu.PrefetchScalarGridSpec(
        num_scalar_prefetch=3,
        grid=(n_tiles, nj),
        in_specs=[
            pl.BlockSpec((tm * words, LANES), lambda i, j, te, tr, tb: (tb[i], 0)),
            pl.BlockSpec((None, d, tn), lambda i, j, te, tr, tb: (te[i], 0, j)),
            pl.BlockSpec((None, d, tn), lambda i, j, te, tr, tb: (te[i], 0, nj + j)),
            pl.BlockSpec((None, 1, tn), lambda i, j, te, tr, tb: (te[i], 0, j)),
            pl.BlockSpec((None, 1, tn), lambda i, j, te, tr, tb: (te[i], 0, nj + j)),
            pl.BlockSpec((None, tn, d), lambda i, j, te, tr, tb: (te[i], j, 0)),
            pl.BlockSpec((None, 1, d), lambda i, j, te, tr, tb: (te[i], 0, 0)),
        ],
        out_specs=pl.BlockSpec((tm * words, LANES), lambda i, j, te, tr, tb: (i, 0)),
        scratch_shapes=[pltpu.VMEM((tm, d), _BF16), pltpu.VMEM((tm, d), _F32)],
    )
    return pl.pallas_call(
        functools.partial(_ffn_kernel, sub=sub),
        grid_spec=grid_spec,
        out_shape=jax.ShapeDtypeStruct((n_slots * words, LANES), jnp.uint32),
        compiler_params=_cparams(2),
        name="expert_ffn",
    )(tile_expert, tile_rows, tile_blk, xs, w_gu_bf16, w_gu_bf16,
      b_gu.reshape(n_exp, 1, de2), b_gu.reshape(n_exp, 1, de2), w_d_bf16, b_d.reshape(n_exp, 1, d))


def _tile_plan(counts, *, tm, n_tiles):
    n_exp = counts.shape[0]
    tiles_e = (counts + tm - 1) // tm
    tile_end = jnp.cumsum(tiles_e)
    tile_beg = tile_end - tiles_e
    starts = tile_beg * tm
    n_active = tile_end[-1]
    tid = jnp.minimum(jnp.arange(n_tiles, dtype=jnp.int32), n_active - 1)
    expert = jnp.minimum(jnp.sum(tile_end[None, :] <= tid[:, None], axis=1), n_exp - 1).astype(jnp.int32)
    rows = jnp.clip(counts[expert] - (tid - tile_beg[expert]) * tm, 0, tm)
    rows = jnp.where(jnp.arange(n_tiles) < n_active, rows, 0).astype(jnp.int32)
    pad_beg = jnp.concatenate([starts + counts, (n_active * tm)[None]])
    pad_end = jnp.concatenate([tile_end * tm, jnp.full((1,), n_tiles * tm, jnp.int32)])
    pad_mid = jnp.minimum((pad_beg + ZERO_ROWS - 1) // ZERO_ROWS * ZERO_ROWS, pad_end)
    pads = tuple(p.astype(jnp.int32) for p in (pad_beg, pad_mid, pad_end))
    return starts, expert, rows, tid.astype(jnp.int32), pads


def _layer(x, norm1_g, w_in, q_norm_g, k_norm_g, rpb, sgu_norm_g, w_spatial, b_spatial, attn_out_g,
           sgu_out_g, w_out, norm2_g, w_router, b_router, w_gate_up, b_gate_up, w_down, b_down):
    batch, seq, d = x.shape
    t = batch * seq
    head_dim = q_norm_g.shape[-1]
    n_heads = rpb.shape[0]
    attn_width = n_heads * head_dim
    n_groups, chunk, _ = w_spatial.shape
    sgu_width = sgu_norm_g.shape[-1]
    group_dim = sgu_width // n_groups
    win_rows_max = (rpb.shape[1] + 1) // 2
    win_cols = (rpb.shape[2] + 1) // 2
    rows = seq // GRID_W
    kh = min(win_rows_max, rows)
    n_exp = w_router.shape[-1]
    assert TOP_K <= n_exp <= LANES // 2 and seq % GRID_W == 0 and w_in.shape[1] == 3 * attn_width + 2 * sgu_width

    xf = x.reshape(t, d)
    row = lambda v: v.reshape(1, -1).astype(_F32)

    w_in_b = w_in.astype(_BF16)
    pair = lambda g: jnp.concatenate([g, g]).reshape(1, 1, LANES)
    head_gain = jnp.concatenate([pair(q_norm_g), pair(k_norm_g), jnp.ones((1, 1, LANES), _F32)], axis=0)
    qkv = _norm_proj(xf, row(norm1_g), w_in_b[:, :3 * attn_width], head_gain,
                     slab_out=True, attn_width=attn_width, head_dim=head_dim)
    ug = _norm_proj(xf, row(norm1_g), w_in_b[:, 3 * attn_width:], head_gain,
                    slab_out=False, attn_width=attn_width, head_dim=head_dim)

    bias_tab = _attention_bias_table(rpb, kh=kh, win_cols=win_cols)
    a_out = _attention(qkv, bias_tab, batch=batch, seq=seq, n_heads=n_heads, head_dim=head_dim, kh=kh)

    b_full = jnp.repeat(b_spatial.T, group_dim, axis=1)
    s_out = _sgu(ug, row(sgu_norm_g), w_spatial.astype(_BF16), b_full, row(sgu_out_g),
                 width=sgu_width, chunk=chunk, group_dim=group_dim)

    half = LANES // 2
    w_r_hi = w_router.astype(_BF16)
    w_r_lo = (w_router - w_r_hi.astype(_F32)).astype(_BF16)
    blank = jnp.zeros((d, LANES), _BF16)
    w_router_pad = jnp.concatenate([blank.at[:, :n_exp].set(w_r_hi).at[:, half:half + n_exp].set(w_r_lo),
                                    blank.at[:, :n_exp].set(w_r_hi)], axis=0)
    b_router_pad = jnp.full((1, LANES), NEG_BIG, _F32).at[0, :n_exp].set(b_router)
    x2, xn, idx_pad, gates_pad = _out_router(a_out, s_out, xf, row(attn_out_g), w_out.astype(_BF16),
                                             row(norm2_g), w_router_pad, b_router_pad)

    de = w_down.shape[1]
    tm = min(1024, t)
    tn = min(512, de)
    sub = min(256, tm)
    n_tiles = (t * TOP_K) // tm + n_exp
    counts = _route_counts(idx_pad)[0, :n_exp].astype(jnp.int32)
    assert tm % ZERO_ROWS == 0
    starts, tile_expert, tile_rows, tile_blk, pads = _tile_plan(counts, tm=tm, n_tiles=n_tiles)
    starts_pad = jnp.zeros((1, LANES), _F32).at[0, :n_exp].set(starts.astype(_F32))
    dst_pad = _route_slots(idx_pad, starts_pad)
    dst_flat = dst_pad[:, :TOP_K].reshape(-1)

    xs = _scatter_rows(*pads, dst_flat, xn, n_tiles * tm, words=_packed_words(d))
    ys = _expert_ffn(tile_expert, tile_rows, tile_blk, xs, w_gate_up.astype(_BF16), b_gate_up,
                     w_down.astype(_BF16), b_down, tm=tm, tn=tn, sub=sub)
    out = _combine(dst_flat, gates_pad, x2, ys)
    return out.reshape(batch, seq, d)


def kernel(x, norm1_g, w_in, q_norm_g, k_norm_g, rpb, sgu_norm_g, w_spatial, b_spatial, attn_out_g,
           sgu_out_g, w_out, norm2_g, w_router, b_router, w_gate_up, b_gate_up, w_down, b_down):
    depth = norm1_g.shape[0]
    for l in range(depth):
        x = _layer(x, norm1_g[l], w_in[l], q_norm_g[l], k_norm_g[l], rpb[l], sgu_norm_g[l], w_spatial[l],
                   b_spatial[l], attn_out_g[l], sgu_out_g[l], w_out[l], norm2_g[l], w_router[l], b_router[l],
                   w_gate_up[l], b_gate_up[l], w_down[l], b_down[l])
    return x
```

```python
import functools

import jax
import jax.numpy as jnp
import numpy as np
from jax import lax
from jax.experimental import pallas as pl
from jax.experimental.pallas import tpu as pltpu

GRID_W = 64
TOP_K = 4
SWIGLU_LIMIT = 7.0
SWIGLU_ALPHA = 1.702
EPS = 1e-6
LANES = 128
NEG_BIG = -1e30
VMEM_LIMIT_BYTES = 56 * 1024 * 1024

_F32 = jnp.float32
_BF16 = jnp.bfloat16


def _cparams(n_axes):
    return pltpu.CompilerParams(
        dimension_semantics=("arbitrary",) * n_axes,
        vmem_limit_bytes=VMEM_LIMIT_BYTES)


def _rms(x):
    return x * lax.rsqrt(jnp.mean(x * x, axis=-1, keepdims=True) + EPS)


def _gelu(x):
    return 0.5 * x * (1.0 + lax.erf(x * 0.7071067811865476))


def _norm_proj_kernel(x_ref, g_ref, w_ref, hg_ref, o_ref, h_ref, *, slab_out, n_norm_tiles, head_dim):
    j = pl.program_id(1)

    @pl.when(j == 0)
    def _():
        h_ref[...] = (_rms(x_ref[...]) * g_ref[...]).astype(_BF16)

    if not slab_out:
        o_ref[...] = jnp.dot(h_ref[...], w_ref[...], preferred_element_type=_F32)
        return

    left = lax.broadcasted_iota(jnp.int32, (1, LANES), 1) < head_dim
    normed = j < n_norm_tiles
    tn = w_ref.shape[1]
    piece = min(2 * LANES, tn)
    for p in range(tn // piece):
        res = jnp.dot(h_ref[...], w_ref[:, p * piece:(p + 1) * piece], preferred_element_type=_F32)
        for s in range(piece // LANES):
            r = res[:, s * LANES:(s + 1) * LANES]
            ss = r * r
            ss_first = jnp.sum(jnp.where(left, ss, 0.0), axis=-1, keepdims=True)
            ss_second = jnp.sum(jnp.where(left, 0.0, ss), axis=-1, keepdims=True)
            inv = jnp.where(left, lax.rsqrt(ss_first / head_dim + EPS),
                            lax.rsqrt(ss_second / head_dim + EPS)) * hg_ref[0]
            o_ref[p * (piece // LANES) + s] = (r * jnp.where(normed, inv, 1.0)).astype(o_ref.dtype)


def _norm_proj(x, g, w_bf16, head_gain, *, slab_out, attn_width, head_dim):
    t, d = x.shape
    n = w_bf16.shape[1]
    tm = min(1024, t)
    tn = min(1024, attn_width if slab_out else n)
    assert t % tm == 0 and n % tn == 0 and tn % LANES == 0
    if slab_out:
        assert attn_width % tn == 0 and 2 * head_dim == LANES
        tiles_per_part = attn_width // tn
        out_shape = jax.ShapeDtypeStruct((n // LANES, t, LANES), _BF16)
        out_spec = pl.BlockSpec((tn // LANES, tm, LANES), lambda i, j: (j, i, 0))
        hg_spec = pl.BlockSpec((1, 1, LANES), lambda i, j: (jnp.minimum(j // tiles_per_part, 2), 0, 0))
        n_norm_tiles = 2 * tiles_per_part
    else:
        out_shape = jax.ShapeDtypeStruct((t, n), _F32)
        out_spec = pl.BlockSpec((tm, tn), lambda i, j: (i, j))
        hg_spec = pl.BlockSpec((1, 1, LANES), lambda i, j: (0, 0, 0))
        n_norm_tiles = 0
    kern = functools.partial(_norm_proj_kernel, slab_out=slab_out, n_norm_tiles=n_norm_tiles, head_dim=head_dim)
    return pl.pallas_call(
        kern,
        grid=(t // tm, n // tn),
        in_specs=[
            pl.BlockSpec((tm, d), lambda i, j: (i, 0)),
            pl.BlockSpec((1, d), lambda i, j: (0, 0)),
            pl.BlockSpec((d, tn), lambda i, j: (0, j)),
            hg_spec,
        ],
        out_specs=out_spec,
        out_shape=out_shape,
        scratch_shapes=[pltpu.VMEM((tm, d), _BF16)],
        compiler_params=_cparams(2),
        name="norm_proj_qkv" if slab_out else "norm_proj_ug",
    )(x, g, w_bf16, head_gain)


def _attn_kernel(q_ref, kp_ref, kc_ref, kn_ref, vp_ref, vc_ref, vn_ref, bias_ref, *rest,
                 rows, rblk, kh, head_dim, scale, n_cast):
    cast_in, (o_ref,), cast_out, (kwin, vwin) = (rest[:n_cast], rest[n_cast:n_cast + 1],
                                                rest[n_cast + 1:2 * n_cast + 1], rest[2 * n_cast + 1:])
    for src, dst in zip(cast_in, cast_out):
        dst[...] = src[...].astype(dst.dtype)
    rb = pl.program_id(2)
    blk = rblk * GRID_W
    for w, (kr, vr) in enumerate(((kp_ref, vp_ref), (kc_ref, vc_ref), (kn_ref, vn_ref))):
        kwin[w * blk:(w + 1) * blk, :] = kr[...]
        vwin[w * blk:(w + 1) * blk, :] = vr[...]
    lane = lax.broadcasted_iota(jnp.int32, (1, LANES), 1)
    left = lane < head_dim
    nk = (kh + 2) * GRID_W
    scores, offsets = [], []
    for pair in range(rblk // 2):
        r = rb * rblk + 2 * pair
        rs_a = jnp.clip(r - kh // 2, 0, rows - kh)
        rs_b = jnp.clip(r + 1 - kh // 2, 0, rows - kh)
        kind = _pair_kind_index(r - rs_a, rs_b - rs_a, kh)
        off = pl.multiple_of((rs_a - (rb * rblk - rblk)) * GRID_W, GRID_W)
        q2 = q_ref[2 * pair * GRID_W:(2 * pair + 2) * GRID_W, :] * scale
        qa, qb = q2[:GRID_W], q2[GRID_W:]
        qm = jnp.concatenate([jnp.where(left, qa, 0.0), jnp.where(left, qb, 0.0),
                              jnp.where(left, 0.0, qa), jnp.where(left, 0.0, qb)], axis=0).astype(_BF16)
        s = lax.dot_general(qm, kwin[pl.ds(off, nk), :], (((1,), (1,)), ((), ())), preferred_element_type=_F32)
        scores.append(s + bias_ref[kind])
        offsets.append(off)
    for pair in range(rblk // 2):
        s = scores[pair]
        p = jnp.exp(s - jnp.max(s, axis=-1, keepdims=True))
        denom = jnp.sum(p, axis=-1, keepdims=True)
        pv = jnp.dot(p.astype(_BF16), vwin[pl.ds(offsets[pair], nk), :], preferred_element_type=_F32) / denom
        o_ref[2 * pair * GRID_W:(2 * pair + 2) * GRID_W, :] = jnp.where(
            left, pv[0:2 * GRID_W], pv[2 * GRID_W:4 * GRID_W])


def _attention_steps(batch, seq, n_heads, kh):
    return batch * (n_heads // 2) * (seq // GRID_W // kh)


def _attention(qkv, bias_tab, cast_srcs, *, batch, seq, n_heads, head_dim, kh):
    n_hp = n_heads // 2
    rows = seq // GRID_W
    rblk = kh
    assert rows % rblk == 0 and rows >= kh and rblk >= 4
    nrb = rows // rblk
    blk = rblk * GRID_W
    t = batch * seq
    n_steps = batch * n_hp * nrb
    cast_specs = []
    for src in cast_srcs:
        assert src.shape[0] % n_steps == 0 and (src.shape[0] // n_steps) % 16 == 0
        cast_specs.append(pl.BlockSpec((src.shape[0] // n_steps, src.shape[1]),
                                       lambda b, hp, rb: ((b * n_hp + hp) * nrb + rb, 0)))

    def qmap(b, hp, rb):
        return (hp, b * nrb + rb, 0)

    def kvmap(part, delta):
        def f(b, hp, rb):
            return (part * n_hp + hp, b * nrb + jnp.clip(rb + delta, 0, nrb - 1), 0)
        return f

    slab = lambda m: pl.BlockSpec((None, blk, LANES), m)
    kern = functools.partial(_attn_kernel, rows=rows, rblk=rblk, kh=kh, head_dim=head_dim,
                             scale=float(head_dim) ** -0.5, n_cast=len(cast_srcs))
    outs = pl.pallas_call(
        kern,
        grid=(batch, n_hp, nrb),
        in_specs=[
            slab(qmap),
            slab(kvmap(1, -1)), slab(kvmap(1, 0)), slab(kvmap(1, 1)),
            slab(kvmap(2, -1)), slab(kvmap(2, 0)), slab(kvmap(2, 1)),
            pl.BlockSpec((len(_pair_kinds(kh)), None, 4 * GRID_W, (kh + 2) * GRID_W),
                         lambda b, hp, rb: (0, hp, 0, 0)),
        ] + cast_specs,
        out_specs=[pl.BlockSpec((blk, LANES), lambda b, hp, rb: (b * nrb + rb, hp))] + cast_specs,
        out_shape=[jax.ShapeDtypeStruct((t, n_hp * LANES), _F32)]
                  + [jax.ShapeDtypeStruct(src.shape, _BF16) for src in cast_srcs],
        scratch_shapes=[pltpu.VMEM((3 * blk, LANES), _BF16), pltpu.VMEM((3 * blk, LANES), _BF16)],
        compiler_params=_cparams(3),
        name="nbr_attention",
    )(qkv, qkv, qkv, qkv, qkv, qkv, qkv, bias_tab, *cast_srcs)
    return outs[0], outs[1:]


def _pair_kinds(kh):
    assert kh % 4 == 0
    return sorted([(s, 0) for s in range(0, kh, 2)] + [(kh // 2, 1)], key=lambda sd: sd[0] + sd[1])


def _pair_kind_index(shift_a, delta, kh):
    code = shift_a + delta
    return code // 2 + (code > kh // 2).astype(jnp.int32)


def _attention_bias_table(rpb, *, kh, win_cols):
    n_heads, n_rpb_rows, n_rpb_cols = rpb.shape
    win_rows_max = (n_rpb_rows + 1) // 2
    nkr = kh + 2
    qc = np.arange(GRID_W)[:, None]
    kc = np.arange(GRID_W)[None, :]
    col_start = np.clip(qc - win_cols // 2, 0, GRID_W - win_cols)
    col_valid = (kc >= col_start) & (kc < col_start + win_cols)
    col_sel = np.zeros((GRID_W, GRID_W, n_rpb_cols), np.float32)
    qi, ki = np.nonzero(col_valid)
    col_sel[qi, ki, ki - qi + win_cols - 1] = 1.0
    kinds = _pair_kinds(kh)
    row_sel = np.zeros((len(kinds), 2, nkr, n_rpb_rows), np.float32)
    row_valid = np.zeros((len(kinds), 2, nkr), bool)
    for kind, (shift_a, delta) in enumerate(kinds):
        for member, (shift, dlt) in enumerate(((shift_a, 0), (shift_a + 1 - delta, delta))):
            for jj in range(nkr):
                if 0 <= jj - dlt < kh:
                    row_sel[kind, member, jj, jj - dlt - shift + win_rows_max - 1] = 1.0
                    row_valid[kind, member, jj] = True
    exact = lax.Precision.HIGHEST
    cols = jnp.einsum("hrc,qkc->rhqk", rpb.astype(_F32), col_sel, precision=exact)
    tab = jnp.einsum("ymjr,rhqk->yhmqjk", row_sel, cols, precision=exact)
    keep = row_valid[:, None, :, None, :, None] & col_valid[None, None, None, :, None, :]
    tab = jnp.where(keep, tab, NEG_BIG)
    return tab.reshape(len(kinds), n_heads // 2, 4 * GRID_W, nkr * GRID_W)


def _sgu_kernel(u_ref, g_ref, ng_ref, w_ref, b_ref, og_ref, o_ref, *, chunk, group_dim):
    tm, width = u_ref.shape
    gg = _gelu(g_ref[...])
    gn = (_rms(gg) * ng_ref[...]).astype(_BF16)
    lane = lax.broadcasted_iota(jnp.int32, (1, LANES), 1)
    left = lane < group_dim
    for c in range(tm // chunk):
        rs = slice(c * chunk, (c + 1) * chunk)
        for gp in range(width // LANES):
            cs = slice(gp * LANES, (gp + 1) * LANES)
            xg = gn[rs, cs]
            ma = jnp.dot(w_ref[2 * gp], xg, preferred_element_type=_F32)
            mb = jnp.dot(w_ref[2 * gp + 1], xg, preferred_element_type=_F32)
            mixed = jnp.where(left, ma, mb) + b_ref[:, cs]
            s = _gelu(u_ref[rs, cs]) * mixed
            o_ref[rs, cs] = s
    s_all = o_ref[...]
    o_ref[...] = _rms(s_all) * og_ref[...]


def _sgu(ug, norm_g, w_sp_bf16, b_full, out_g, *, width, chunk, group_dim):
    t = ug.shape[0]
    tm = min(512, t)
    assert t % tm == 0 and tm % chunk == 0 and width % LANES == 0 and 2 * group_dim == LANES
    n_groups = w_sp_bf16.shape[0]
    kern = functools.partial(_sgu_kernel, chunk=chunk, group_dim=group_dim)
    return pl.pallas_call(
        kern,
        grid=(t // tm,),
        in_specs=[
            pl.BlockSpec((tm, width), lambda i: (i, 0)),
            pl.BlockSpec((tm, width), lambda i: (i, 1)),
            pl.BlockSpec((1, width), lambda i: (0, 0)),
            pl.BlockSpec((n_groups, chunk, chunk), lambda i: (0, 0, 0)),
            pl.BlockSpec((chunk, width), lambda i: (0, 0)),
            pl.BlockSpec((1, width), lambda i: (0, 0)),
        ],
        out_specs=pl.BlockSpec((tm, width), lambda i: (i, 0)),
        out_shape=jax.ShapeDtypeStruct((t, width), _F32),
        compiler_params=_cparams(1),
        name="spatial_gating",
    )(ug, ug, norm_g, w_sp_bf16, b_full, out_g)


ROUTER_ROW_CHUNK = 128


def _out_router_kernel(a_ref, s_ref, x_ref, ag_ref, w_ref, n2g_ref, wr_ref, br_ref,
                       x2_ref, xn_ref, idx_ref, gate_ref):
    chunks = [slice(c * ROUTER_ROW_CHUNK, (c + 1) * ROUTER_ROW_CHUNK)
              for c in range(x_ref.shape[0] // ROUTER_ROW_CHUNK)]
    for rs in chunks:
        an = _rms(a_ref[rs, :]) * ag_ref[...]
        mix = jnp.concatenate([an, s_ref[rs, :]], axis=-1).astype(_BF16)
        x2_ref[rs, :] = x_ref[rs, :] + jnp.dot(mix, w_ref[...], preferred_element_type=_F32)
    all_logits = []
    for rs in chunks:
        xn = _rms(x2_ref[rs, :]) * n2g_ref[...]
        xn_ref[rs, :] = xn
        xh = xn.astype(_BF16)
        xl = (xn - xh.astype(_F32)).astype(_BF16)
        parts = jnp.dot(jnp.concatenate([xh, xl], axis=-1), wr_ref[...], preferred_element_type=_F32)
        all_logits.append(parts + pltpu.roll(parts, LANES // 2, axis=1) + br_ref[...])
    for rs, logits in zip(chunks, all_logits):
        rc = logits.shape[0]
        lane = lax.broadcasted_iota(jnp.int32, (rc, LANES), 1)
        lane_f = lane.astype(_F32)
        idx_acc = jnp.zeros((rc, LANES), _F32)
        vals = []
        cur = logits
        for k in range(TOP_K):
            m = jnp.max(cur, axis=-1, keepdims=True)
            ik = jnp.min(jnp.where(cur == m, lane_f, float(LANES)), axis=-1, keepdims=True)
            vals.append(m)
            idx_acc = jnp.where(lane == k, ik, idx_acc)
            cur = jnp.where(lane_f == ik, -jnp.inf, cur)
        exps = [jnp.exp(v - vals[0]) for v in vals]
        denom = exps[0]
        for e in exps[1:]:
            denom = denom + e
        gate_acc = jnp.zeros((rc, LANES), _F32)
        for k in range(TOP_K):
            gate_acc = jnp.where(lane == k, exps[k] / denom, gate_acc)
        idx_ref[rs, :] = idx_acc.astype(jnp.int32)
        gate_ref[rs, :] = gate_acc


def _out_router(a_out, s_out, x, attn_out_g, w_out_bf16, norm2_g, w_router_pad, b_router_pad):
    t, d = x.shape
    wa = a_out.shape[1]
    ws = s_out.shape[1]
    tm = min(512, t)
    assert t % tm == 0
    row = lambda n: pl.BlockSpec((tm, n), lambda i: (i, 0))
    full = lambda a, b: pl.BlockSpec((a, b), lambda i: (0, 0))
    return pl.pallas_call(
        _out_router_kernel,
        grid=(t // tm,),
        in_specs=[row(wa), row(ws), row(d), full(1, wa), full(wa + ws, d), full(1, d),
                  full(2 * d, LANES), full(1, LANES)],
        out_specs=[row(d), row(d), row(LANES), row(LANES)],
        out_shape=[jax.ShapeDtypeStruct((t, d), _F32), jax.ShapeDtypeStruct((t, d), _F32),
                   jax.ShapeDtypeStruct((t, LANES), jnp.int32), jax.ShapeDtypeStruct((t, LANES), _F32)],
        compiler_params=_cparams(1),
        name="out_proj_router",
    )(a_out, s_out, x, attn_out_g, w_out_bf16, norm2_g, w_router_pad, b_router_pad)


def _membership(idx):
    lane = lax.broadcasted_iota(jnp.int32, idx.shape, 1)
    return [lane == idx[:, k:k + 1] for k in range(TOP_K)]


def _count_kernel(idx_ref, cnt_ref):
    @pl.when(pl.program_id(0) == 0)
    def _():
        cnt_ref[...] = jnp.zeros_like(cnt_ref)

    member = sum(oh.astype(_F32) for oh in _membership(idx_ref[...]))
    cnt_ref[...] += jnp.sum(member, axis=0, keepdims=True)


def _slot_kernel(idx_ref, start_ref, dst_ref, carry_ref):
    @pl.when(pl.program_id(0) == 0)
    def _():
        carry_ref[...] = jnp.zeros_like(carry_ref)

    onehots = _membership(idx_ref[...])
    member = sum(oh.astype(_F32) for oh in onehots)
    tb = member.shape[0]
    earlier = (lax.broadcasted_iota(jnp.int32, (tb, tb), 0) > lax.broadcasted_iota(jnp.int32, (tb, tb), 1))
    before = jnp.dot(earlier.astype(_BF16), member.astype(_BF16), preferred_element_type=_F32)
    slot_e = start_ref[...] + carry_ref[...] + before
    lane = lax.broadcasted_iota(jnp.int32, member.shape, 1)
    dst = jnp.zeros(member.shape, _F32)
    for k in range(TOP_K):
        dk = jnp.sum(jnp.where(onehots[k], slot_e, 0.0), axis=-1, keepdims=True)
        dst = jnp.where(lane == k, dk, dst)
    dst_ref[...] = dst.astype(jnp.int32)
    carry_ref[...] += jnp.sum(member, axis=0, keepdims=True)


def _route_counts(idx_pad):
    t = idx_pad.shape[0]
    tb = min(512, t)
    return pl.pallas_call(
        _count_kernel,
        grid=(t // tb,),
        in_specs=[pl.BlockSpec((tb, LANES), lambda i: (i, 0))],
        out_specs=pl.BlockSpec((1, LANES), lambda i: (0, 0)),
        out_shape=jax.ShapeDtypeStruct((1, LANES), _F32),
        compiler_params=_cparams(1),
        name="route_counts",
    )(idx_pad)


def _route_slots(idx_pad, starts_f32):
    t = idx_pad.shape[0]
    tb = min(512, t)
    return pl.pallas_call(
        _slot_kernel,
        grid=(t // tb,),
        in_specs=[pl.BlockSpec((tb, LANES), lambda i: (i, 0)), pl.BlockSpec((1, LANES), lambda i: (0, 0))],
        out_specs=pl.BlockSpec((tb, LANES), lambda i: (i, 0)),
        out_shape=jax.ShapeDtypeStruct((t, LANES), jnp.int32),
        scratch_shapes=[pltpu.VMEM((1, LANES), _F32)],
        compiler_params=_cparams(1),
        name="route_slots",
    )(idx_pad, starts_f32)


ROUTE_TB = 256


ZERO_ROWS = 256


def _scatter_rows_kernel(pad_beg_ref, pad_mid_ref, pad_end_ref, dst_ref, xn_ref, xs_ref, zero_ref, sem, zsem):
    tb = xn_ref.shape[0]
    first = pl.program_id(0) == 0
    n_regions = pad_beg_ref.shape[0]

    def slot(ref, r, n=1):
        return ref.at[pl.ds(r, n)]

    def issue(t, c):
        for k in range(TOP_K):
            d = dst_ref[t * TOP_K + k]
            pltpu.make_async_copy(slot(xn_ref, t), slot(xs_ref, d), sem).start(priority=k % 2)
        return c

    def drain():
        for _ in range(TOP_K):
            pltpu.make_async_copy(xn_ref, slot(xs_ref, 0, tb), sem).wait()

    def clear_padding(start):
        def region(e, c):
            beg, mid, end = pad_beg_ref[e], pad_mid_ref[e], pad_end_ref[e]

            def one_row(r, c2):
                cp = pltpu.make_async_copy(slot(zero_ref, 0), slot(xs_ref, r), zsem)
                cp.start() if start else cp.wait()
                return c2

            def one_block(b, c2):
                r0 = pl.multiple_of(mid + b * ZERO_ROWS, ZERO_ROWS)
                cp = pltpu.make_async_copy(zero_ref, slot(xs_ref, r0, ZERO_ROWS), zsem)
                cp.start() if start else cp.wait()
                return c2

            lax.fori_loop(beg, mid, one_row, 0)
            lax.fori_loop(0, (end - mid) // ZERO_ROWS, one_block, 0)
            return c

        lax.fori_loop(0, n_regions, region, 0)

    @pl.when(first)
    def _():
        zero_ref[...] = jnp.zeros_like(zero_ref)
        clear_padding(True)

    lax.fori_loop(0, tb, issue, 0, unroll=4)
    drain()

    @pl.when(first)
    def _():
        clear_padding(False)


def _scatter_rows(pad_beg, pad_mid, pad_end, dst_flat, xn, n_slots):
    t, d = xn.shape
    tb = min(ROUTE_TB, t)
    grid_spec = pltpu.PrefetchScalarGridSpec(
        num_scalar_prefetch=3,
        grid=(t // tb,),
        in_specs=[
            pl.BlockSpec((tb * TOP_K,), lambda i, *_: (i,), memory_space=pltpu.SMEM),
            pl.BlockSpec((tb, d), lambda i, *_: (i, 0)),
        ],
        out_specs=pl.BlockSpec(memory_space=pl.ANY),
        scratch_shapes=[pltpu.VMEM((ZERO_ROWS, d), xn.dtype), pltpu.SemaphoreType.DMA(()),
                        pltpu.SemaphoreType.DMA(())],
    )
    return pl.pallas_call(
        _scatter_rows_kernel,
        grid_spec=grid_spec,
        out_shape=jax.ShapeDtypeStruct((n_slots, d), xn.dtype),
        compiler_params=_cparams(1),
        name="scatter_rows",
    )(pad_beg, pad_mid, pad_end, dst_flat, xn)


def _combine_kernel(dst_ref, gate_ref, x2_ref, ys_ref, o_ref, rows_ref, sems):
    tb = x2_ref.shape[0]

    def issue(t, c, sem):
        for k in range(TOP_K):
            src = dst_ref[t * TOP_K + k]
            pltpu.make_async_copy(ys_ref.at[pl.ds(src, 1)], rows_ref.at[k, pl.ds(t, 1)], sem).start(priority=k % 2)
        return c

    half = tb // 2
    for part in range(2):
        lax.fori_loop(part * half, (part + 1) * half, functools.partial(issue, sem=sems.at[part]), 0, unroll=4)
    for part in range(2):
        t0 = part * half
        for k in range(TOP_K):
            pltpu.make_async_copy(ys_ref.at[pl.ds(0, half)], rows_ref.at[k, pl.ds(t0, half)], sems.at[part]).wait()
        acc = x2_ref[t0:t0 + half, :]
        gates = gate_ref[t0:t0 + half, :]
        for k in range(TOP_K):
            acc = acc + gates[:, k:k + 1] * rows_ref[k, t0:t0 + half, :]
        o_ref[t0:t0 + half, :] = acc


def _combine(dst_flat, gates_pad, x2, ys):
    t, d = x2.shape
    tb = min(ROUTE_TB, t)
    return pl.pallas_call(
        _combine_kernel,
        grid=(t // tb,),
        in_specs=[
            pl.BlockSpec((tb * TOP_K,), lambda i: (i,), memory_space=pltpu.SMEM),
            pl.BlockSpec((tb, LANES), lambda i: (i, 0)),
            pl.BlockSpec((tb, d), lambda i: (i, 0)),
            pl.BlockSpec(memory_space=pl.ANY),
        ],
        out_specs=pl.BlockSpec((tb, d), lambda i: (i, 0)),
        out_shape=jax.ShapeDtypeStruct((t, d), _F32),
        scratch_shapes=[pltpu.VMEM((TOP_K, tb, d), _F32), pltpu.SemaphoreType.DMA((2,))],
        compiler_params=_cparams(1),
        name="combine_rows",
    )(dst_flat, gates_pad, x2, ys)


def _ffn_kernel(te_ref, tr_ref, tblk_ref, x_ref, wg_ref, wl_ref, bg_ref, bl_ref, wd_ref, bd_ref,
                o_ref, xb_ref, *, sub):
    del te_ref, tblk_ref
    i = pl.program_id(0)
    j = pl.program_id(1)
    n_rows = tr_ref[i]
    tm, d = o_ref.shape

    @pl.when(jnp.logical_and(j == 0, n_rows > 0))
    def _():
        o_ref[...] = jnp.broadcast_to(bd_ref[...], (tm, d))

    @pl.when(jnp.logical_and(j == 0, n_rows == 0))
    def _():
        o_ref[...] = jnp.zeros((tm, d), _F32)

    def rows_block(r0, size):
        rs = pl.ds(r0, size)

        @pl.when(j == 0)
        def _():
            xb_ref[rs, :] = x_ref[rs, :].astype(_BF16)

        xb = xb_ref[rs, :]
        hg = jnp.dot(xb, wg_ref[...], preferred_element_type=_F32) + bg_ref[...]
        hl = jnp.dot(xb, wl_ref[...], preferred_element_type=_F32) + bl_ref[...]
        gate = jnp.minimum(hg, SWIGLU_LIMIT)
        lin = jnp.clip(hl, -SWIGLU_LIMIT, SWIGLU_LIMIT)
        act = gate * (1.0 / (1.0 + jnp.exp(-SWIGLU_ALPHA * gate))) * (lin + 1.0)
        o_ref[rs, :] += jnp.dot(act.astype(_BF16), wd_ref[...], preferred_element_type=_F32)

    @pl.when(n_rows == tm)
    def _():
        rows_block(0, tm)

    @pl.when(n_rows < tm)
    def _():
        def sub_block(s, c):
            rows_block(pl.multiple_of(s * sub, sub), sub)
            return c

        lax.fori_loop(0, (n_rows + sub - 1) // sub, sub_block, 0)


def _expert_ffn(tile_expert, tile_rows, tile_blk, xs, w_gu_bf16, b_gu, w_d_bf16, b_d, *, tm, tn, sub):
    n_slots, d = xs.shape
    n_exp, _, de2 = w_gu_bf16.shape
    de = de2 // 2
    nj = de // tn
    n_tiles = n_slots // tm
    assert de % tn == 0 and n_slots % tm == 0 and tm % sub == 0
    grid_spec = pltpu.PrefetchScalarGridSpec(
        num_scalar_prefetch=3,
        grid=(n_tiles, nj),
        in_specs=[
            pl.BlockSpec((tm, d), lambda i, j, te, tr, tb: (tb[i], 0)),
            pl.BlockSpec((None, d, tn), lambda i, j, te, tr, tb: (te[i], 0, j)),
            pl.BlockSpec((None, d, tn), lambda i, j, te, tr, tb: (te[i], 0, nj + j)),
            pl.BlockSpec((None, 1, tn), lambda i, j, te, tr, tb: (te[i], 0, j)),
            pl.BlockSpec((None, 1, tn), lambda i, j, te, tr, tb: (te[i], 0, nj + j)),
            pl.BlockSpec((None, tn, d), lambda i, j, te, tr, tb: (te[i], j, 0)),
            pl.BlockSpec((None, 1, d), lambda i, j, te, tr, tb: (te[i], 0, 0)),
        ],
        out_specs=pl.BlockSpec((tm, d), lambda i, j, te, tr, tb: (i, 0)),
        scratch_shapes=[pltpu.VMEM((tm, d), _BF16)],
    )
    return pl.pallas_call(
        functools.partial(_ffn_kernel, sub=sub),
        grid_spec=grid_spec,
        out_shape=jax.ShapeDtypeStruct((n_slots, d), _F32),
        compiler_params=_cparams(2),
        name="expert_ffn",
    )(tile_expert, tile_rows, tile_blk, xs, w_gu_bf16, w_gu_bf16,
      b_gu.reshape(n_exp, 1, de2), b_gu.reshape(n_exp, 1, de2), w_d_bf16, b_d.reshape(n_exp, 1, d))


def _tile_plan(counts, *, tm, n_tiles):
    n_exp = counts.shape[0]
    tiles_e = (counts + tm - 1) // tm
    tile_end = jnp.cumsum(tiles_e)
    tile_beg = tile_end - tiles_e
    starts = tile_beg * tm
    n_active = tile_end[-1]
    tid = jnp.minimum(jnp.arange(n_tiles, dtype=jnp.int32), n_active - 1)
    expert = jnp.minimum(jnp.sum(tile_end[None, :] <= tid[:, None], axis=1), n_exp - 1).astype(jnp.int32)
    rows = jnp.clip(counts[expert] - (tid - tile_beg[expert]) * tm, 0, tm)
    rows = jnp.where(jnp.arange(n_tiles) < n_active, rows, 0).astype(jnp.int32)
    pad_beg = jnp.concatenate([starts + counts, (n_active * tm)[None]])
    pad_end = jnp.concatenate([tile_end * tm, jnp.full((1,), n_tiles * tm, jnp.int32)])
    pad_mid = jnp.minimum((pad_beg + ZERO_ROWS - 1) // ZERO_ROWS * ZERO_ROWS, pad_end)
    pads = tuple(p.astype(jnp.int32) for p in (pad_beg, pad_mid, pad_end))
    return starts, expert, rows, tid.astype(jnp.int32), pads


def _layer(x, norm1_g, w_in, q_norm_g, k_norm_g, rpb, sgu_norm_g, w_spatial, b_spatial, attn_out_g,
           sgu_out_g, w_out, norm2_g, w_router, b_router, w_gate_up, b_gate_up, w_down, b_down):
    batch, seq, d = x.shape
    t = batch * seq
    head_dim = q_norm_g.shape[-1]
    n_heads = rpb.shape[0]
    attn_width = n_heads * head_dim
    n_groups, chunk, _ = w_spatial.shape
    sgu_width = sgu_norm_g.shape[-1]
    group_dim = sgu_width // n_groups
    win_rows_max = (rpb.shape[1] + 1) // 2
    win_cols = (rpb.shape[2] + 1) // 2
    rows = seq // GRID_W
    kh = min(win_rows_max, rows)
    n_exp = w_router.shape[-1]
    assert TOP_K <= n_exp <= LANES // 2 and seq % GRID_W == 0 and w_in.shape[1] == 3 * attn_width + 2 * sgu_width

    xf = x.reshape(t, d)
    row = lambda v: v.reshape(1, -1).astype(_F32)

    w_in_b = w_in.astype(_BF16)
    pair = lambda g: jnp.concatenate([g, g]).reshape(1, 1, LANES)
    head_gain = jnp.concatenate([pair(q_norm_g), pair(k_norm_g), jnp.ones((1, 1, LANES), _F32)], axis=0)
    qkv = _norm_proj(xf, row(norm1_g), w_in_b[:, :3 * attn_width], head_gain,
                     slab_out=True, attn_width=attn_width, head_dim=head_dim)
    ug = _norm_proj(xf, row(norm1_g), w_in_b[:, 3 * attn_width:], head_gain,
                    slab_out=False, attn_width=attn_width, head_dim=head_dim)

    bias_tab = _attention_bias_table(rpb, kh=kh, win_cols=win_cols)
    w_gu2d = w_gate_up.reshape(-1, w_gate_up.shape[-1])
    w_d2d = w_down.reshape(-1, w_down.shape[-1])
    n_steps = _attention_steps(batch, seq, n_heads, kh)
    ride = all(w.shape[0] % (16 * n_steps) == 0 for w in (w_gu2d, w_d2d))
    a_out, cast = _attention(qkv, bias_tab, (w_gu2d, w_d2d) if ride else (),
                             batch=batch, seq=seq, n_heads=n_heads, head_dim=head_dim, kh=kh)
    w_gu_b, w_d_b = cast if ride else (w_gu2d.astype(_BF16), w_d2d.astype(_BF16))
    w_gu_b = w_gu_b.reshape(w_gate_up.shape)
    w_d_b = w_d_b.reshape(w_down.shape)

    b_full = jnp.repeat(b_spatial.T, group_dim, axis=1)
    s_out = _sgu(ug, row(sgu_norm_g), w_spatial.astype(_BF16), b_full, row(sgu_out_g),
                 width=sgu_width, chunk=chunk, group_dim=group_dim)

    half = LANES // 2
    w_r_hi = w_router.astype(_BF16)
    w_r_lo = (w_router - w_r_hi.astype(_F32)).astype(_BF16)
    blank = jnp.zeros((d, LANES), _BF16)
    w_router_pad = jnp.concatenate([blank.at[:, :n_exp].set(w_r_hi).at[:, half:half + n_exp].set(w_r_lo),
                                    blank.at[:, :n_exp].set(w_r_hi)], axis=0)
    b_router_pad = jnp.full((1, LANES), NEG_BIG, _F32).at[0, :n_exp].set(b_router)
    x2, xn, idx_pad, gates_pad = _out_router(a_out, s_out, xf, row(attn_out_g), w_out.astype(_BF16),
                                             row(norm2_g), w_router_pad, b_router_pad)

    de = w_down.shape[1]
    tm = min(1024, t)
    tn = min(512, de)
    sub = min(256, tm)
    n_tiles = (t * TOP_K) // tm + n_exp
    counts = _route_counts(idx_pad)[0, :n_exp].astype(jnp.int32)
    assert tm % ZERO_ROWS == 0
    starts, tile_expert, tile_rows, tile_blk, pads = _tile_plan(counts, tm=tm, n_tiles=n_tiles)
    starts_pad = jnp.zeros((1, LANES), _F32).at[0, :n_exp].set(starts.astype(_F32))
    dst_pad = _route_slots(idx_pad, starts_pad)
    dst_flat = dst_pad[:, :TOP_K].reshape(-1)

    xs = _scatter_rows(*pads, dst_flat, xn, n_tiles * tm)
    ys = _expert_ffn(tile_expert, tile_rows, tile_blk, xs, w_gu_b, b_gate_up, w_d_b, b_down, tm=tm, tn=tn, sub=sub)
    out = _combine(dst_flat, gates_pad, x2, ys)
    return out.reshape(batch, seq, d)


def kernel(x, norm1_g, w_in, q_norm_g, k_norm_g, rpb, sgu_norm_g, w_spatial, b_spatial, attn_out_g,
           sgu_out_g, w_out, norm2_g, w_router, b_router, w_gate_up, b_gate_up, w_down, b_down):
    depth = norm1_g.shape[0]
    for l in range(depth):
        x = _layer(x, norm1_g[l], w_in[l], q_norm_g[l], k_norm_g[l], rpb[l], sgu_norm_g[l], w_spatial[l],
                   b_spatial[l], attn_out_g[l], sgu_out_g[l], w_out[l], norm2_g[l], w_router[l], b_router[l],
                   w_gate_up[l], b_gate_up[l], w_down[l], b_down[l])
    return x
```

```python
import functools

import jax
import jax.numpy as jnp
import numpy as np
from jax import lax
from jax.experimental import pallas as pl
from jax.experimental.pallas import tpu as pltpu

GRID_W = 64
TOP_K = 4
SWIGLU_LIMIT = 7.0
SWIGLU_ALPHA = 1.702
EPS = 1e-6
LANES = 128
NEG_BIG = -1e30
VMEM_LIMIT_BYTES = 56 * 1024 * 1024

_F32 = jnp.float32
_BF16 = jnp.bfloat16


def _cparams(n_axes):
    return pltpu.CompilerParams(
        dimension_semantics=("arbitrary",) * n_axes,
        vmem_limit_bytes=VMEM_LIMIT_BYTES)


def _rms(x):
    return x * lax.rsqrt(jnp.mean(x * x, axis=-1, keepdims=True) + EPS)


def _gelu(x):
    return 0.5 * x * (1.0 + lax.erf(x * 0.7071067811865476))


def _norm_proj_kernel(x_ref, g_ref, w_ref, hg_ref, o_ref, h_ref, *, slab_out, n_norm_tiles, head_dim):
    j = pl.program_id(1)

    @pl.when(j == 0)
    def _():
        h_ref[...] = (_rms(x_ref[...]) * g_ref[...]).astype(_BF16)

    if not slab_out:
        o_ref[...] = jnp.dot(h_ref[...], w_ref[...], preferred_element_type=_F32)
        return

    left = lax.broadcasted_iota(jnp.int32, (1, LANES), 1) < head_dim
    normed = j < n_norm_tiles
    tn = w_ref.shape[1]
    piece = min(2 * LANES, tn)
    for p in range(tn // piece):
        res = jnp.dot(h_ref[...], w_ref[:, p * piece:(p + 1) * piece], preferred_element_type=_F32)
        for s in range(piece // LANES):
            r = res[:, s * LANES:(s + 1) * LANES]
            ss = r * r
            ss_first = jnp.sum(jnp.where(left, ss, 0.0), axis=-1, keepdims=True)
            ss_second = jnp.sum(jnp.where(left, 0.0, ss), axis=-1, keepdims=True)
            inv = jnp.where(left, lax.rsqrt(ss_first / head_dim + EPS),
                            lax.rsqrt(ss_second / head_dim + EPS)) * hg_ref[0]
            o_ref[p * (piece // LANES) + s] = (r * jnp.where(normed, inv, 1.0)).astype(o_ref.dtype)


def _norm_proj(x, g, w_bf16, head_gain, *, slab_out, attn_width, head_dim):
    t, d = x.shape
    n = w_bf16.shape[1]
    tm = min(1024, t)
    tn = min(1024, attn_width if slab_out else n)
    assert t % tm == 0 and n % tn == 0 and tn % LANES == 0
    if slab_out:
        assert attn_width % tn == 0 and 2 * head_dim == LANES
        tiles_per_part = attn_width // tn
        out_shape = jax.ShapeDtypeStruct((n // LANES, t, LANES), _BF16)
        out_spec = pl.BlockSpec((tn // LANES, tm, LANES), lambda i, j: (j, i, 0))
        hg_spec = pl.BlockSpec((1, 1, LANES), lambda i, j: (jnp.minimum(j // tiles_per_part, 2), 0, 0))
        n_norm_tiles = 2 * tiles_per_part
    else:
        out_shape = jax.ShapeDtypeStruct((t, n), _F32)
        out_spec = pl.BlockSpec((tm, tn), lambda i, j: (i, j))
        hg_spec = pl.BlockSpec((1, 1, LANES), lambda i, j: (0, 0, 0))
        n_norm_tiles = 0
    kern = functools.partial(_norm_proj_kernel, slab_out=slab_out, n_norm_tiles=n_norm_tiles, head_dim=head_dim)
    return pl.pallas_call(
        kern,
        grid=(t // tm, n // tn),
        in_specs=[
            pl.BlockSpec((tm, d), lambda i, j: (i, 0)),
            pl.BlockSpec((1, d), lambda i, j: (0, 0)),
            pl.BlockSpec((d, tn), lambda i, j: (0, j)),
            hg_spec,
        ],
        out_specs=out_spec,
        out_shape=out_shape,
        scratch_shapes=[pltpu.VMEM((tm, d), _BF16)],
        compiler_params=_cparams(2),
        name="norm_proj_qkv" if slab_out else "norm_proj_ug",
    )(x, g, w_bf16, head_gain)


def _attn_kernel(q_ref, kp_ref, kc_ref, kn_ref, vp_ref, vc_ref, vn_ref, bias_ref, *rest,
                 rows, rblk, kh, head_dim, scale, n_cast):
    cast_in, (o_ref,), cast_out, (kwin, vwin) = (rest[:n_cast], rest[n_cast:n_cast + 1],
                                                rest[n_cast + 1:2 * n_cast + 1], rest[2 * n_cast + 1:])
    for src, dst in zip(cast_in, cast_out):
        dst[...] = src[...].astype(dst.dtype)
    rb = pl.program_id(2)
    blk = rblk * GRID_W
    for w, (kr, vr) in enumerate(((kp_ref, vp_ref), (kc_ref, vc_ref), (kn_ref, vn_ref))):
        kwin[w * blk:(w + 1) * blk, :] = kr[...]
        vwin[w * blk:(w + 1) * blk, :] = vr[...]
    lane = lax.broadcasted_iota(jnp.int32, (1, LANES), 1)
    left = lane < head_dim
    nk = (kh + 2) * GRID_W
    scores, offsets = [], []
    for pair in range(rblk // 2):
        r = rb * rblk + 2 * pair
        rs_a = jnp.clip(r - kh // 2, 0, rows - kh)
        rs_b = jnp.clip(r + 1 - kh // 2, 0, rows - kh)
        kind = _pair_kind_index(r - rs_a, rs_b - rs_a, kh)
        off = pl.multiple_of((rs_a - (rb * rblk - rblk)) * GRID_W, GRID_W)
        q2 = q_ref[2 * pair * GRID_W:(2 * pair + 2) * GRID_W, :] * scale
        qa, qb = q2[:GRID_W], q2[GRID_W:]
        qm = jnp.concatenate([jnp.where(left, qa, 0.0), jnp.where(left, qb, 0.0),
                              jnp.where(left, 0.0, qa), jnp.where(left, 0.0, qb)], axis=0).astype(_BF16)
        s = lax.dot_general(qm, kwin[pl.ds(off, nk), :], (((1,), (1,)), ((), ())), preferred_element_type=_F32)
        scores.append(s + bias_ref[kind])
        offsets.append(off)
    for pair in range(rblk // 2):
        s = scores[pair]
        p = jnp.exp(s - jnp.max(s, axis=-1, keepdims=True))
        denom = jnp.sum(p, axis=-1, keepdims=True)
        pv = jnp.dot(p.astype(_BF16), vwin[pl.ds(offsets[pair], nk), :], preferred_element_type=_F32) / denom
        o_ref[2 * pair * GRID_W:(2 * pair + 2) * GRID_W, :] = jnp.where(
            left, pv[0:2 * GRID_W], pv[2 * GRID_W:4 * GRID_W])


def _attention_steps(batch, seq, n_heads, kh):
    return batch * (n_heads // 2) * (seq // GRID_W // kh)


def _attention(qkv, bias_tab, cast_srcs, *, batch, seq, n_heads, head_dim, kh):
    n_hp = n_heads // 2
    rows = seq // GRID_W
    rblk = kh
    assert rows % rblk == 0 and rows >= kh and rblk >= 4
    nrb = rows // rblk
    blk = rblk * GRID_W
    t = batch * seq
    n_steps = batch * n_hp * nrb
    cast_specs = []
    for src in cast_srcs:
        assert src.shape[0] % n_steps == 0 and (src.shape[0] // n_steps) % 16 == 0
        cast_specs.append(pl.BlockSpec((src.shape[0] // n_steps, src.shape[1]),
                                       lambda b, hp, rb: ((b * n_hp + hp) * nrb + rb, 0)))

    def qmap(b, hp, rb):
        return (hp, b * nrb + rb, 0)

    def kvmap(part, delta):
        def f(b, hp, rb):
            return (part * n_hp + hp, b * nrb + jnp.clip(rb + delta, 0, nrb - 1), 0)
        return f

    slab = lambda m: pl.BlockSpec((None, blk, LANES), m)
    kern = functools.partial(_attn_kernel, rows=rows, rblk=rblk, kh=kh, head_dim=head_dim,
                             scale=float(head_dim) ** -0.5, n_cast=len(cast_srcs))
    outs = pl.pallas_call(
        kern,
        grid=(batch, n_hp, nrb),
        in_specs=[
            slab(qmap),
            slab(kvmap(1, -1)), slab(kvmap(1, 0)), slab(kvmap(1, 1)),
            slab(kvmap(2, -1)), slab(kvmap(2, 0)), slab(kvmap(2, 1)),
            pl.BlockSpec((len(_pair_kinds(kh)), None, 4 * GRID_W, (kh + 2) * GRID_W),
                         lambda b, hp, rb: (0, hp, 0, 0)),
        ] + cast_specs,
        out_specs=[pl.BlockSpec((blk, LANES), lambda b, hp, rb: (b * nrb + rb, hp))] + cast_specs,
        out_shape=[jax.ShapeDtypeStruct((t, n_hp * LANES), _F32)]
                  + [jax.ShapeDtypeStruct(src.shape, _BF16) for src in cast_srcs],
        scratch_shapes=[pltpu.VMEM((3 * blk, LANES), _BF16), pltpu.VMEM((3 * blk, LANES), _BF16)],
        compiler_params=_cparams(3),
        name="nbr_attention",
    )(qkv, qkv, qkv, qkv, qkv, qkv, qkv, bias_tab, *cast_srcs)
    return outs[0], outs[1:]


def _pair_kinds(kh):
    assert kh % 4 == 0
    return sorted([(s, 0) for s in range(0, kh, 2)] + [(kh // 2, 1)], key=lambda sd: sd[0] + sd[1])


def _pair_kind_index(shift_a, delta, kh):
    code = shift_a + delta
    return code // 2 + (code > kh // 2).astype(jnp.int32)


def _attention_bias_table(rpb, *, kh, win_cols):
    n_heads, n_rpb_rows, n_rpb_cols = rpb.shape
    win_rows_max = (n_rpb_rows + 1) // 2
    nkr = kh + 2
    qc = np.arange(GRID_W)[:, None]
    kc = np.arange(GRID_W)[None, :]
    col_start = np.clip(qc - win_cols // 2, 0, GRID_W - win_cols)
    col_valid = (kc >= col_start) & (kc < col_start + win_cols)
    col_sel = np.zeros((GRID_W, GRID_W, n_rpb_cols), np.float32)
    qi, ki = np.nonzero(col_valid)
    col_sel[qi, ki, ki - qi + win_cols - 1] = 1.0
    kinds = _pair_kinds(kh)
    row_sel = np.zeros((len(kinds), 2, nkr, n_rpb_rows), np.float32)
    row_valid = np.zeros((len(kinds), 2, nkr), bool)
    for kind, (shift_a, delta) in enumerate(kinds):
        for member, (shift, dlt) in enumerate(((shift_a, 0), (shift_a + 1 - delta, delta))):
            for jj in range(nkr):
                if 0 <= jj - dlt < kh:
                    row_sel[kind, member, jj, jj - dlt - shift + win_rows_max - 1] = 1.0
                    row_valid[kind, member, jj] = True
    exact = lax.Precision.HIGHEST
    cols = jnp.einsum("hrc,qkc->rhqk", rpb.astype(_F32), col_sel, precision=exact)
    tab = jnp.einsum("ymjr,rhqk->yhmqjk", row_sel, cols, precision=exact)
    keep = row_valid[:, None, :, None, :, None] & col_valid[None, None, None, :, None, :]
    tab = jnp.where(keep, tab, NEG_BIG)
    return tab.reshape(len(kinds), n_heads // 2, 4 * GRID_W, nkr * GRID_W)


def _sgu_kernel(u_ref, g_ref, ng_ref, w_ref, b_ref, og_ref, o_ref, *, chunk, group_dim):
    tm, width = u_ref.shape
    gg = _gelu(g_ref[...])
    gn = (_rms(gg) * ng_ref[...]).astype(_BF16)
    lane = lax.broadcasted_iota(jnp.int32, (1, LANES), 1)
    left = lane < group_dim
    for c in range(tm // chunk):
        rs = slice(c * chunk, (c + 1) * chunk)
        for gp in range(width // LANES):
            cs = slice(gp * LANES, (gp + 1) * LANES)
            xg = gn[rs, cs]
            ma = jnp.dot(w_ref[2 * gp], xg, preferred_element_type=_F32)
            mb = jnp.dot(w_ref[2 * gp + 1], xg, preferred_element_type=_F32)
            mixed = jnp.where(left, ma, mb) + b_ref[:, cs]
            s = _gelu(u_ref[rs, cs]) * mixed
            o_ref[rs, cs] = s
    s_all = o_ref[...]
    o_ref[...] = _rms(s_all) * og_ref[...]


def _sgu(ug, norm_g, w_sp_bf16, b_full, out_g, *, width, chunk, group_dim):
    t = ug.shape[0]
    tm = min(512, t)
    assert t % tm == 0 and tm % chunk == 0 and width % LANES == 0 and 2 * group_dim == LANES
    n_groups = w_sp_bf16.shape[0]
    kern = functools.partial(_sgu_kernel, chunk=chunk, group_dim=group_dim)
    return pl.pallas_call(
        kern,
        grid=(t // tm,),
        in_specs=[
            pl.BlockSpec((tm, width), lambda i: (i, 0)),
            pl.BlockSpec((tm, width), lambda i: (i, 1)),
            pl.BlockSpec((1, width), lambda i: (0, 0)),
            pl.BlockSpec((n_groups, chunk, chunk), lambda i: (0, 0, 0)),
            pl.BlockSpec((chunk, width), lambda i: (0, 0)),
            pl.BlockSpec((1, width), lambda i: (0, 0)),
        ],
        out_specs=pl.BlockSpec((tm, width), lambda i: (i, 0)),
        out_shape=jax.ShapeDtypeStruct((t, width), _F32),
        compiler_params=_cparams(1),
        name="spatial_gating",
    )(ug, ug, norm_g, w_sp_bf16, b_full, out_g)


ROUTER_ROW_CHUNK = 128


def _out_router_kernel(a_ref, s_ref, x_ref, ag_ref, w_ref, n2g_ref, wr_ref, br_ref,
                       x2_ref, xn_ref, idx_ref, gate_ref):
    chunks = [slice(c * ROUTER_ROW_CHUNK, (c + 1) * ROUTER_ROW_CHUNK)
              for c in range(x_ref.shape[0] // ROUTER_ROW_CHUNK)]
    for rs in chunks:
        an = _rms(a_ref[rs, :]) * ag_ref[...]
        mix = jnp.concatenate([an, s_ref[rs, :]], axis=-1).astype(_BF16)
        x2_ref[rs, :] = x_ref[rs, :] + jnp.dot(mix, w_ref[...], preferred_element_type=_F32)
    all_logits = []
    for rs in chunks:
        xn = _rms(x2_ref[rs, :]) * n2g_ref[...]
        xn_ref[rs, :] = xn
        xh = xn.astype(_BF16)
        xl = (xn - xh.astype(_F32)).astype(_BF16)
        parts = jnp.dot(jnp.concatenate([xh, xl], axis=-1), wr_ref[...], preferred_element_type=_F32)
        all_logits.append(parts + pltpu.roll(parts, LANES // 2, axis=1) + br_ref[...])
    for rs, logits in zip(chunks, all_logits):
        rc = logits.shape[0]
        lane = lax.broadcasted_iota(jnp.int32, (rc, LANES), 1)
        lane_f = lane.astype(_F32)
        idx_acc = jnp.zeros((rc, LANES), _F32)
        vals = []
        cur = logits
        for k in range(TOP_K):
            m = jnp.max(cur, axis=-1, keepdims=True)
            ik = jnp.min(jnp.where(cur == m, lane_f, float(LANES)), axis=-1, keepdims=True)
            vals.append(m)
            idx_acc = jnp.where(lane == k, ik, idx_acc)
            cur = jnp.where(lane_f == ik, -jnp.inf, cur)
        exps = [jnp.exp(v - vals[0]) for v in vals]
        denom = exps[0]
        for e in exps[1:]:
            denom = denom + e
        gate_acc = jnp.zeros((rc, LANES), _F32)
        for k in range(TOP_K):
            gate_acc = jnp.where(lane == k, exps[k] / denom, gate_acc)
        idx_ref[rs, :] = idx_acc.astype(jnp.int32)
        gate_ref[rs, :] = gate_acc


def _out_router(a_out, s_out, x, attn_out_g, w_out_bf16, norm2_g, w_router_pad, b_router_pad):
    t, d = x.shape
    wa = a_out.shape[1]
    ws = s_out.shape[1]
    tm = min(512, t)
    assert t % tm == 0
    row = lambda n: pl.BlockSpec((tm, n), lambda i: (i, 0))
    full = lambda a, b: pl.BlockSpec((a, b), lambda i: (0, 0))
    return pl.pallas_call(
        _out_router_kernel,
        grid=(t // tm,),
        in_specs=[row(wa), row(ws), row(d), full(1, wa), full(wa + ws, d), full(1, d),
                  full(2 * d, LANES), full(1, LANES)],
        out_specs=[row(d), row(d), row(LANES), row(LANES)],
        out_shape=[jax.ShapeDtypeStruct((t, d), _F32), jax.ShapeDtypeStruct((t, d), _F32),
                   jax.ShapeDtypeStruct((t, LANES), jnp.int32), jax.ShapeDtypeStruct((t, LANES), _F32)],
        compiler_params=_cparams(1),
        name="out_proj_router",
    )(a_out, s_out, x, attn_out_g, w_out_bf16, norm2_g, w_router_pad, b_router_pad)


def _membership(idx):
    lane = lax.broadcasted_iota(jnp.int32, idx.shape, 1)
    return [lane == idx[:, k:k + 1] for k in range(TOP_K)]


def _count_kernel(idx_ref, cnt_ref):
    @pl.when(pl.program_id(0) == 0)
    def _():
        cnt_ref[...] = jnp.zeros_like(cnt_ref)

    member = sum(oh.astype(_F32) for oh in _membership(idx_ref[...]))
    cnt_ref[...] += jnp.sum(member, axis=0, keepdims=True)


def _slot_kernel(idx_ref, start_ref, dst_ref, carry_ref):
    @pl.when(pl.program_id(0) == 0)
    def _():
        carry_ref[...] = jnp.zeros_like(carry_ref)

    onehots = _membership(idx_ref[...])
    member = sum(oh.astype(_F32) for oh in onehots)
    tb = member.shape[0]
    earlier = (lax.broadcasted_iota(jnp.int32, (tb, tb), 0) > lax.broadcasted_iota(jnp.int32, (tb, tb), 1))
    before = jnp.dot(earlier.astype(_BF16), member.astype(_BF16), preferred_element_type=_F32)
    slot_e = start_ref[...] + carry_ref[...] + before
    lane = lax.broadcasted_iota(jnp.int32, member.shape, 1)
    dst = jnp.zeros(member.shape, _F32)
    for k in range(TOP_K):
        dk = jnp.sum(jnp.where(onehots[k], slot_e, 0.0), axis=-1, keepdims=True)
        dst = jnp.where(lane == k, dk, dst)
    dst_ref[...] = dst.astype(jnp.int32)
    carry_ref[...] += jnp.sum(member, axis=0, keepdims=True)


def _route_counts(idx_pad):
    t = idx_pad.shape[0]
    tb = min(512, t)
    return pl.pallas_call(
        _count_kernel,
        grid=(t // tb,),
        in_specs=[pl.BlockSpec((tb, LANES), lambda i: (i, 0))],
        out_specs=pl.BlockSpec((1, LANES), lambda i: (0, 0)),
        out_shape=jax.ShapeDtypeStruct((1, LANES), _F32),
        compiler_params=_cparams(1),
        name="route_counts",
    )(idx_pad)


def _route_slots(idx_pad, starts_f32):
    t = idx_pad.shape[0]
    tb = min(512, t)
    return pl.pallas_call(
        _slot_kernel,
        grid=(t // tb,),
        in_specs=[pl.BlockSpec((tb, LANES), lambda i: (i, 0)), pl.BlockSpec((1, LANES), lambda i: (0, 0))],
        out_specs=pl.BlockSpec((tb, LANES), lambda i: (i, 0)),
        out_shape=jax.ShapeDtypeStruct((t, LANES), jnp.int32),
        scratch_shapes=[pltpu.VMEM((1, LANES), _F32)],
        compiler_params=_cparams(1),
        name="route_slots",
    )(idx_pad, starts_f32)


ROUTE_TB = 512


ZERO_ROWS = 256


def _scatter_rows_kernel(pad_beg_ref, pad_mid_ref, pad_end_ref, dst_ref, xn_ref, xs_ref, zero_ref, sem, zsem):
    tb = xn_ref.shape[0]
    first = pl.program_id(0) == 0
    n_regions = pad_beg_ref.shape[0]

    def slot(ref, r, n=1):
        return ref.at[pl.ds(r, n)]

    def issue(t, c):
        for k in range(TOP_K):
            d = dst_ref[t * TOP_K + k]
            pltpu.make_async_copy(slot(xn_ref, t), slot(xs_ref, d), sem).start(priority=k % 2)
        return c

    def drain():
        for _ in range(TOP_K):
            pltpu.make_async_copy(xn_ref, slot(xs_ref, 0, tb), sem).wait()

    def clear_padding(start):
        def region(e, c):
            beg, mid, end = pad_beg_ref[e], pad_mid_ref[e], pad_end_ref[e]

            def one_row(r, c2):
                cp = pltpu.make_async_copy(slot(zero_ref, 0), slot(xs_ref, r), zsem)
                cp.start() if start else cp.wait()
                return c2

            def one_block(b, c2):
                r0 = pl.multiple_of(mid + b * ZERO_ROWS, ZERO_ROWS)
                cp = pltpu.make_async_copy(zero_ref, slot(xs_ref, r0, ZERO_ROWS), zsem)
                cp.start() if start else cp.wait()
                return c2

            lax.fori_loop(beg, mid, one_row, 0)
            lax.fori_loop(0, (end - mid) // ZERO_ROWS, one_block, 0)
            return c

        lax.fori_loop(0, n_regions, region, 0)

    @pl.when(first)
    def _():
        zero_ref[...] = jnp.zeros_like(zero_ref)
        clear_padding(True)

    lax.fori_loop(0, tb, issue, 0, unroll=4)
    drain()

    @pl.when(first)
    def _():
        clear_padding(False)


def _scatter_rows(pad_beg, pad_mid, pad_end, dst_flat, xn, n_slots):
    t, d = xn.shape
    tb = min(ROUTE_TB, t)
    grid_spec = pltpu.PrefetchScalarGridSpec(
        num_scalar_prefetch=3,
        grid=(t // tb,),
        in_specs=[
            pl.BlockSpec((tb * TOP_K,), lambda i, *_: (i,), memory_space=pltpu.SMEM),
            pl.BlockSpec((tb, d), lambda i, *_: (i, 0)),
        ],
        out_specs=pl.BlockSpec(memory_space=pl.ANY),
        scratch_shapes=[pltpu.VMEM((ZERO_ROWS, d), xn.dtype), pltpu.SemaphoreType.DMA(()),
                        pltpu.SemaphoreType.DMA(())],
    )
    return pl.pallas_call(
        _scatter_rows_kernel,
        grid_spec=grid_spec,
        out_shape=jax.ShapeDtypeStruct((n_slots, d), xn.dtype),
        compiler_params=_cparams(1),
        name="scatter_rows",
    )(pad_beg, pad_mid, pad_end, dst_flat, xn)


def _combine_kernel(dst_ref, gate_ref, x2_ref, ys_ref, o_ref, rows_ref, sems):
    tb = x2_ref.shape[0]

    def issue(t, c, sem):
        for k in range(TOP_K):
            src = dst_ref[t * TOP_K + k]
            pltpu.make_async_copy(ys_ref.at[pl.ds(src, 1)], rows_ref.at[k, pl.ds(t, 1)], sem).start(priority=k % 2)
        return c

    half = tb // 2
    for part in range(2):
        lax.fori_loop(part * half, (part + 1) * half, functools.partial(issue, sem=sems.at[part]), 0, unroll=4)
    for part in range(2):
        t0 = part * half
        for k in range(TOP_K):
            pltpu.make_async_copy(ys_ref.at[pl.ds(0, half)], rows_ref.at[k, pl.ds(t0, half)], sems.at[part]).wait()
        acc = x2_ref[t0:t0 + half, :]
        gates = gate_ref[t0:t0 + half, :]
        for k in range(TOP_K):
            acc = acc + gates[:, k:k + 1] * rows_ref[k, t0:t0 + half, :]
        o_ref[t0:t0 + half, :] = acc


def _combine(dst_flat, gates_pad, x2, ys):
    t, d = x2.shape
    tb = min(ROUTE_TB, t)
    return pl.pallas_call(
        _combine_kernel,
        grid=(t // tb,),
        in_specs=[
            pl.BlockSpec((tb * TOP_K,), lambda i: (i,), memory_space=pltpu.SMEM),
            pl.BlockSpec((tb, LANES), lambda i: (i, 0)),
            pl.BlockSpec((tb, d), lambda i: (i, 0)),
            pl.BlockSpec(memory_space=pl.ANY),
        ],
        out_specs=pl.BlockSpec((tb, d), lambda i: (i, 0)),
        out_shape=jax.ShapeDtypeStruct((t, d), _F32),
        scratch_shapes=[pltpu.VMEM((TOP_K, tb, d), _F32), pltpu.SemaphoreType.DMA((2,))],
        compiler_params=_cparams(1),
        name="combine_rows",
    )(dst_flat, gates_pad, x2, ys)


def _ffn_kernel(te_ref, tr_ref, tblk_ref, x_ref, wg_ref, wl_ref, bg_ref, bl_ref, wd_ref, bd_ref,
                o_ref, xb_ref, *, sub):
    del te_ref, tblk_ref
    i = pl.program_id(0)
    j = pl.program_id(1)
    n_rows = tr_ref[i]
    tm, d = o_ref.shape

    @pl.when(jnp.logical_and(j == 0, n_rows > 0))
    def _():
        o_ref[...] = jnp.broadcast_to(bd_ref[...], (tm, d))

    @pl.when(jnp.logical_and(j == 0, n_rows == 0))
    def _():
        o_ref[...] = jnp.zeros((tm, d), _F32)

    def rows_block(r0, size):
        rs = pl.ds(r0, size)

        @pl.when(j == 0)
        def _():
            xb_ref[rs, :] = x_ref[rs, :].astype(_BF16)

        xb = xb_ref[rs, :]
        hg = jnp.dot(xb, wg_ref[...], preferred_element_type=_F32) + bg_ref[...]
        hl = jnp.dot(xb, wl_ref[...], preferred_element_type=_F32) + bl_ref[...]
        gate = jnp.minimum(hg, SWIGLU_LIMIT)
        lin = jnp.clip(hl, -SWIGLU_LIMIT, SWIGLU_LIMIT)
        act = gate * (1.0 / (1.0 + jnp.exp(-SWIGLU_ALPHA * gate))) * (lin + 1.0)
        o_ref[rs, :] += jnp.dot(act.astype(_BF16), wd_ref[...], preferred_element_type=_F32)

    @pl.when(n_rows == tm)
    def _():
        rows_block(0, tm)

    @pl.when(n_rows < tm)
    def _():
        def sub_block(s, c):
            rows_block(pl.multiple_of(s * sub, sub), sub)
            return c

        lax.fori_loop(0, (n_rows + sub - 1) // sub, sub_block, 0)


def _expert_ffn(tile_expert, tile_rows, tile_blk, xs, w_gu_bf16, b_gu, w_d_bf16, b_d, *, tm, tn, sub):
    n_slots, d = xs.shape
    n_exp, _, de2 = w_gu_bf16.shape
    de = de2 // 2
    nj = de // tn
    n_tiles = n_slots // tm
    assert de % tn == 0 and n_slots % tm == 0 and tm % sub == 0
    grid_spec = pltpu.PrefetchScalarGridSpec(
        num_scalar_prefetch=3,
        grid=(n_tiles, nj),
        in_specs=[
            pl.BlockSpec((tm, d), lambda i, j, te, tr, tb: (tb[i], 0)),
            pl.BlockSpec((None, d, tn), lambda i, j, te, tr, tb: (te[i], 0, j)),
            pl.BlockSpec((None, d, tn), lambda i, j, te, tr, tb: (te[i], 0, nj + j)),
            pl.BlockSpec((None, 1, tn), lambda i, j, te, tr, tb: (te[i], 0, j)),
            pl.BlockSpec((None, 1, tn), lambda i, j, te, tr, tb: (te[i], 0, nj + j)),
            pl.BlockSpec((None, tn, d), lambda i, j, te, tr, tb: (te[i], j, 0)),
            pl.BlockSpec((None, 1, d), lambda i, j, te, tr, tb: (te[i], 0, 0)),
        ],
        out_specs=pl.BlockSpec((tm, d), lambda i, j, te, tr, tb: (i, 0)),
        scratch_shapes=[pltpu.VMEM((tm, d), _BF16)],
    )
    return pl.pallas_call(
        functools.partial(_ffn_kernel, sub=sub),
        grid_spec=grid_spec,
        out_shape=jax.ShapeDtypeStruct((n_slots, d), _F32),
        compiler_params=_cparams(2),
        name="expert_ffn",
    )(tile_expert, tile_rows, tile_blk, xs, w_gu_bf16, w_gu_bf16,
      b_gu.reshape(n_exp, 1, de2), b_gu.reshape(n_exp, 1, de2), w_d_bf16, b_d.reshape(n_exp, 1, d))


def _tile_plan(counts, *, tm, n_tiles):
    n_exp = counts.shape[0]
    tiles_e = (counts + tm - 1) // tm
    tile_end = jnp.cumsum(tiles_e)
    tile_beg = tile_end - tiles_e
    starts = tile_beg * tm
    n_active = tile_end[-1]
    tid = jnp.minimum(jnp.arange(n_tiles, dtype=jnp.int32), n_active - 1)
    expert = jnp.minimum(jnp.sum(tile_end[None, :] <= tid[:, None], axis=1), n_exp - 1).astype(jnp.int32)
    rows = jnp.clip(counts[expert] - (tid - tile_beg[expert]) * tm, 0, tm)
    rows = jnp.where(jnp.arange(n_tiles) < n_active, rows, 0).astype(jnp.int32)
    pad_beg = jnp.concatenate([starts + counts, (n_active * tm)[None]])
    pad_end = jnp.concatenate([tile_end * tm, jnp.full((1,), n_tiles * tm, jnp.int32)])
    pad_mid = jnp.minimum((pad_beg + ZERO_ROWS - 1) // ZERO_ROWS * ZERO_ROWS, pad_end)
    pads = tuple(p.astype(jnp.int32) for p in (pad_beg, pad_mid, pad_end))
    return starts, expert, rows, tid.astype(jnp.int32), pads


def _layer(x, norm1_g, w_in, q_norm_g, k_norm_g, rpb, sgu_norm_g, w_spatial, b_spatial, attn_out_g,
           sgu_out_g, w_out, norm2_g, w_router, b_router, w_gate_up, b_gate_up, w_down, b_down):
    batch, seq, d = x.shape
    t = batch * seq
    head_dim = q_norm_g.shape[-1]
    n_heads = rpb.shape[0]
    attn_width = n_heads * head_dim
    n_groups, chunk, _ = w_spatial.shape
    sgu_width = sgu_norm_g.shape[-1]
    group_dim = sgu_width // n_groups
    win_rows_max = (rpb.shape[1] + 1) // 2
    win_cols = (rpb.shape[2] + 1) // 2
    rows = seq // GRID_W
    kh = min(win_rows_max, rows)
    n_exp = w_router.shape[-1]
    assert TOP_K <= n_exp <= LANES // 2 and seq % GRID_W == 0 and w_in.shape[1] == 3 * attn_width + 2 * sgu_width

    xf = x.reshape(t, d)
    row = lambda v: v.reshape(1, -1).astype(_F32)

    w_in_b = w_in.astype(_BF16)
    pair = lambda g: jnp.concatenate([g, g]).reshape(1, 1, LANES)
    head_gain = jnp.concatenate([pair(q_norm_g), pair(k_norm_g), jnp.ones((1, 1, LANES), _F32)], axis=0)
    qkv = _norm_proj(xf, row(norm1_g), w_in_b[:, :3 * attn_width], head_gain,
                     slab_out=True, attn_width=attn_width, head_dim=head_dim)
    ug = _norm_proj(xf, row(norm1_g), w_in_b[:, 3 * attn_width:], head_gain,
                    slab_out=False, attn_width=attn_width, head_dim=head_dim)

    bias_tab = _attention_bias_table(rpb, kh=kh, win_cols=win_cols)
    w_gu2d = w_gate_up.reshape(-1, w_gate_up.shape[-1])
    w_d2d = w_down.reshape(-1, w_down.shape[-1])
    n_steps = _attention_steps(batch, seq, n_heads, kh)
    ride = all(w.shape[0] % (16 * n_steps) == 0 for w in (w_gu2d, w_d2d))
    a_out, cast = _attention(qkv, bias_tab, (w_gu2d, w_d2d) if ride else (),
                             batch=batch, seq=seq, n_heads=n_heads, head_dim=head_dim, kh=kh)
    w_gu_b, w_d_b = cast if ride else (w_gu2d.astype(_BF16), w_d2d.astype(_BF16))
    w_gu_b = w_gu_b.reshape(w_gate_up.shape)
    w_d_b = w_d_b.reshape(w_down.shape)

    b_full = jnp.repeat(b_spatial.T, group_dim, axis=1)
    s_out = _sgu(ug, row(sgu_norm_g), w_spatial.astype(_BF16), b_full, row(sgu_out_g),
                 width=sgu_width, chunk=chunk, group_dim=group_dim)

    half = LANES // 2
    w_r_hi = w_router.astype(_BF16)
    w_r_lo = (w_router - w_r_hi.astype(_F32)).astype(_BF16)
    blank = jnp.zeros((d, LANES), _BF16)
    w_router_pad = jnp.concatenate([blank.at[:, :n_exp].set(w_r_hi).at[:, half:half + n_exp].set(w_r_lo),
                                    blank.at[:, :n_exp].set(w_r_hi)], axis=0)
    b_router_pad = jnp.full((1, LANES), NEG_BIG, _F32).at[0, :n_exp].set(b_router)
    x2, xn, idx_pad, gates_pad = _out_router(a_out, s_out, xf, row(attn_out_g), w_out.astype(_BF16),
                                             row(norm2_g), w_router_pad, b_router_pad)

    de = w_down.shape[1]
    tm = min(1024, t)
    tn = min(512, de)
    sub = min(256, tm)
    n_tiles = (t * TOP_K) // tm + n_exp
    counts = _route_counts(idx_pad)[0, :n_exp].astype(jnp.int32)
    assert tm % ZERO_ROWS == 0
    starts, tile_expert, tile_rows, tile_blk, pads = _tile_plan(counts, tm=tm, n_tiles=n_tiles)
    starts_pad = jnp.zeros((1, LANES), _F32).at[0, :n_exp].set(starts.astype(_F32))
    dst_pad = _route_slots(idx_pad, starts_pad)
    dst_flat = dst_pad[:, :TOP_K].reshape(-1)

    xs = _scatter_rows(*pads, dst_flat, xn, n_tiles * tm)
    ys = _expert_ffn(tile_expert, tile_rows, tile_blk, xs, w_gu_b, b_gate_up, w_d_b, b_down, tm=tm, tn=tn, sub=sub)
    out = _combine(dst_flat, gates_pad, x2, ys)
    return out.reshape(batch, seq, d)


def kernel(x, norm1_g, w_in, q_norm_g, k_norm_g, rpb, sgu_norm_g, w_spatial, b_spatial, attn_out_g,
           sgu_out_g, w_out, norm2_g, w_router, b_router, w_gate_up, b_gate_up, w_down, b_down):
    depth = norm1_g.shape[0]
    for l in range(depth):
        x = _layer(x, norm1_g[l], w_in[l], q_norm_g[l], k_norm_g[l], rpb[l], sgu_norm_g[l], w_spatial[l],
                   b_spatial[l], attn_out_g[l], sgu_out_g[l], w_out[l], norm2_g[l], w_router[l], b_router[l],
                   w_gate_up[l], b_gate_up[l], w_down[l], b_down[l])
    return x
```

```python
import functools

import jax
import jax.numpy as jnp
import numpy as np
from jax import lax
from jax.experimental import pallas as pl
from jax.experimental.pallas import tpu as pltpu

GRID_W = 64
TOP_K = 4
SWIGLU_LIMIT = 7.0
SWIGLU_ALPHA = 1.702
EPS = 1e-6
LANES = 128
NEG_BIG = -1e30
VMEM_LIMIT_BYTES = 56 * 1024 * 1024

_F32 = jnp.float32
_BF16 = jnp.bfloat16


def _cparams(n_axes):
    return pltpu.CompilerParams(
        dimension_semantics=("arbitrary",) * n_axes,
        vmem_limit_bytes=VMEM_LIMIT_BYTES)


def _rms(x):
    return x * lax.rsqrt(jnp.mean(x * x, axis=-1, keepdims=True) + EPS)


def _gelu(x):
    return 0.5 * x * (1.0 + lax.erf(x * 0.7071067811865476))


def _norm_proj_kernel(x_ref, g_ref, w_ref, hg_ref, o_ref, h_ref, *, slab_out, n_norm_tiles, head_dim):
    j = pl.program_id(1)

    @pl.when(j == 0)
    def _():
        h_ref[...] = (_rms(x_ref[...]) * g_ref[...]).astype(_BF16)

    if not slab_out:
        o_ref[...] = jnp.dot(h_ref[...], w_ref[...], preferred_element_type=_F32)
        return

    left = lax.broadcasted_iota(jnp.int32, (1, LANES), 1) < head_dim
    normed = j < n_norm_tiles
    tn = w_ref.shape[1]
    piece = min(2 * LANES, tn)
    for p in range(tn // piece):
        res = jnp.dot(h_ref[...], w_ref[:, p * piece:(p + 1) * piece], preferred_element_type=_F32)
        for s in range(piece // LANES):
            r = res[:, s * LANES:(s + 1) * LANES]
            ss = r * r
            ss_first = jnp.sum(jnp.where(left, ss, 0.0), axis=-1, keepdims=True)
            ss_second = jnp.sum(jnp.where(left, 0.0, ss), axis=-1, keepdims=True)
            inv = jnp.where(left, lax.rsqrt(ss_first / head_dim + EPS),
                            lax.rsqrt(ss_second / head_dim + EPS)) * hg_ref[0]
            o_ref[p * (piece // LANES) + s] = (r * jnp.where(normed, inv, 1.0)).astype(o_ref.dtype)


def _norm_proj(x, g, w_bf16, head_gain, *, slab_out, attn_width, head_dim):
    t, d = x.shape
    n = w_bf16.shape[1]
    tm = min(1024, t)
    tn = min(1024, attn_width if slab_out else n)
    assert t % tm == 0 and n % tn == 0 and tn % LANES == 0
    if slab_out:
        assert attn_width % tn == 0 and 2 * head_dim == LANES
        tiles_per_part = attn_width // tn
        out_shape = jax.ShapeDtypeStruct((n // LANES, t, LANES), _BF16)
        out_spec = pl.BlockSpec((tn // LANES, tm, LANES), lambda i, j: (j, i, 0))
        hg_spec = pl.BlockSpec((1, 1, LANES), lambda i, j: (jnp.minimum(j // tiles_per_part, 2), 0, 0))
        n_norm_tiles = 2 * tiles_per_part
    else:
        out_shape = jax.ShapeDtypeStruct((t, n), _F32)
        out_spec = pl.BlockSpec((tm, tn), lambda i, j: (i, j))
        hg_spec = pl.BlockSpec((1, 1, LANES), lambda i, j: (0, 0, 0))
        n_norm_tiles = 0
    kern = functools.partial(_norm_proj_kernel, slab_out=slab_out, n_norm_tiles=n_norm_tiles, head_dim=head_dim)
    return pl.pallas_call(
        kern,
        grid=(t // tm, n // tn),
        in_specs=[
            pl.BlockSpec((tm, d), lambda i, j: (i, 0)),
            pl.BlockSpec((1, d), lambda i, j: (0, 0)),
            pl.BlockSpec((d, tn), lambda i, j: (0, j)),
            hg_spec,
        ],
        out_specs=out_spec,
        out_shape=out_shape,
        scratch_shapes=[pltpu.VMEM((tm, d), _BF16)],
        compiler_params=_cparams(2),
        name="norm_proj_qkv" if slab_out else "norm_proj_ug",
    )(x, g, w_bf16, head_gain)


def _attn_kernel(q_ref, kp_ref, kc_ref, kn_ref, vp_ref, vc_ref, vn_ref, bias_ref, *rest,
                 rows, rblk, kh, head_dim, scale, n_cast):
    cast_in, (o_ref,), cast_out, (kwin, vwin) = (rest[:n_cast], rest[n_cast:n_cast + 1],
                                                rest[n_cast + 1:2 * n_cast + 1], rest[2 * n_cast + 1:])
    for src, dst in zip(cast_in, cast_out):
        dst[...] = src[...].astype(dst.dtype)
    rb = pl.program_id(2)
    blk = rblk * GRID_W
    for w, (kr, vr) in enumerate(((kp_ref, vp_ref), (kc_ref, vc_ref), (kn_ref, vn_ref))):
        kwin[w * blk:(w + 1) * blk, :] = kr[...]
        vwin[w * blk:(w + 1) * blk, :] = vr[...]
    lane = lax.broadcasted_iota(jnp.int32, (1, LANES), 1)
    left = lane < head_dim
    nk = (kh + 2) * GRID_W
    scores, offsets = [], []
    for pair in range(rblk // 2):
        r = rb * rblk + 2 * pair
        rs_a = jnp.clip(r - kh // 2, 0, rows - kh)
        rs_b = jnp.clip(r + 1 - kh // 2, 0, rows - kh)
        kind = _pair_kind_index(r - rs_a, rs_b - rs_a, kh)
        off = pl.multiple_of((rs_a - (rb * rblk - rblk)) * GRID_W, GRID_W)
        q2 = q_ref[2 * pair * GRID_W:(2 * pair + 2) * GRID_W, :] * scale
        qa, qb = q2[:GRID_W], q2[GRID_W:]
        qm = jnp.concatenate([jnp.where(left, qa, 0.0), jnp.where(left, qb, 0.0),
                              jnp.where(left, 0.0, qa), jnp.where(left, 0.0, qb)], axis=0).astype(_BF16)
        s = lax.dot_general(qm, kwin[pl.ds(off, nk), :], (((1,), (1,)), ((), ())), preferred_element_type=_F32)
        scores.append(s + bias_ref[kind])
        offsets.append(off)
    for pair in range(rblk // 2):
        s = scores[pair]
        p = jnp.exp(s - jnp.max(s, axis=-1, keepdims=True))
        denom = jnp.sum(p, axis=-1, keepdims=True)
        pv = jnp.dot(p.astype(_BF16), vwin[pl.ds(offsets[pair], nk), :], preferred_element_type=_F32) / denom
        o_ref[2 * pair * GRID_W:(2 * pair + 2) * GRID_W, :] = jnp.where(
            left, pv[0:2 * GRID_W], pv[2 * GRID_W:4 * GRID_W])


def _attention_steps(batch, seq, n_heads, kh):
    return batch * (n_heads // 2) * (seq // GRID_W // kh)


def _attention(qkv, bias_tab, cast_srcs, *, batch, seq, n_heads, head_dim, kh):
    n_hp = n_heads // 2
    rows = seq // GRID_W
    rblk = kh
    assert rows % rblk == 0 and rows >= kh and rblk >= 4
    nrb = rows // rblk
    blk = rblk * GRID_W
    t = batch * seq
    n_steps = batch * n_hp * nrb
    cast_specs = []
    for src in cast_srcs:
        assert src.shape[0] % n_steps == 0 and (src.shape[0] // n_steps) % 16 == 0
        cast_specs.append(pl.BlockSpec((src.shape[0] // n_steps, src.shape[1]),
                                       lambda b, hp, rb: ((b * n_hp + hp) * nrb + rb, 0)))

    def qmap(b, hp, rb):
        return (hp, b * nrb + rb, 0)

    def kvmap(part, delta):
        def f(b, hp, rb):
            return (part * n_hp + hp, b * nrb + jnp.clip(rb + delta, 0, nrb - 1), 0)
        return f

    slab = lambda m: pl.BlockSpec((None, blk, LANES), m)
    kern = functools.partial(_attn_kernel, rows=rows, rblk=rblk, kh=kh, head_dim=head_dim,
                             scale=float(head_dim) ** -0.5, n_cast=len(cast_srcs))
    outs = pl.pallas_call(
        kern,
        grid=(batch, n_hp, nrb),
        in_specs=[
            slab(qmap),
            slab(kvmap(1, -1)), slab(kvmap(1, 0)), slab(kvmap(1, 1)),
            slab(kvmap(2, -1)), slab(kvmap(2, 0)), slab(kvmap(2, 1)),
            pl.BlockSpec((len(_pair_kinds(kh)), None, 4 * GRID_W, (kh + 2) * GRID_W),
                         lambda b, hp, rb: (0, hp, 0, 0)),
        ] + cast_specs,
        out_specs=[pl.BlockSpec((blk, LANES), lambda b, hp, rb: (b * nrb + rb, hp))] + cast_specs,
        out_shape=[jax.ShapeDtypeStruct((t, n_hp * LANES), _F32)]
                  + [jax.ShapeDtypeStruct(src.shape, _BF16) for src in cast_srcs],
        scratch_shapes=[pltpu.VMEM((3 * blk, LANES), _BF16), pltpu.VMEM((3 * blk, LANES), _BF16)],
        compiler_params=_cparams(3),
        name="nbr_attention",
    )(qkv, qkv, qkv, qkv, qkv, qkv, qkv, bias_tab, *cast_srcs)
    return outs[0], outs[1:]


def _pair_kinds(kh):
    assert kh % 4 == 0
    return sorted([(s, 0) for s in range(0, kh, 2)] + [(kh // 2, 1)], key=lambda sd: sd[0] + sd[1])


def _pair_kind_index(shift_a, delta, kh):
    code = shift_a + delta
    return code // 2 + (code > kh // 2).astype(jnp.int32)


def _attention_bias_table(rpb, *, kh, win_cols):
    n_heads, n_rpb_rows, n_rpb_cols = rpb.shape
    win_rows_max = (n_rpb_rows + 1) // 2
    nkr = kh + 2
    qc = np.arange(GRID_W)[:, None]
    kc = np.arange(GRID_W)[None, :]
    col_start = np.clip(qc - win_cols // 2, 0, GRID_W - win_cols)
    col_valid = (kc >= col_start) & (kc < col_start + win_cols)
    col_sel = np.zeros((GRID_W, GRID_W, n_rpb_cols), np.float32)
    qi, ki = np.nonzero(col_valid)
    col_sel[qi, ki, ki - qi + win_cols - 1] = 1.0
    kinds = _pair_kinds(kh)
    row_sel = np.zeros((len(kinds), 2, nkr, n_rpb_rows), np.float32)
    row_valid = np.zeros((len(kinds), 2, nkr), bool)
    for kind, (shift_a, delta) in enumerate(kinds):
        for member, (shift, dlt) in enumerate(((shift_a, 0), (shift_a + 1 - delta, delta))):
            for jj in range(nkr):
                if 0 <= jj - dlt < kh:
                    row_sel[kind, member, jj, jj - dlt - shift + win_rows_max - 1] = 1.0
                    row_valid[kind, member, jj] = True
    exact = lax.Precision.HIGHEST
    cols = jnp.einsum("hrc,qkc->rhqk", rpb.astype(_F32), col_sel, precision=exact)
    tab = jnp.einsum("ymjr,rhqk->yhmqjk", row_sel, cols, precision=exact)
    keep = row_valid[:, None, :, None, :, None] & col_valid[None, None, None, :, None, :]
    tab = jnp.where(keep, tab, NEG_BIG)
    return tab.reshape(len(kinds), n_heads // 2, 4 * GRID_W, nkr * GRID_W)


def _sgu_kernel(u_ref, g_ref, ng_ref, w_ref, b_ref, og_ref, o_ref, *, chunk, group_dim):
    tm, width = u_ref.shape
    gg = _gelu(g_ref[...])
    gn = (_rms(gg) * ng_ref[...]).astype(_BF16)
    lane = lax.broadcasted_iota(jnp.int32, (1, LANES), 1)
    left = lane < group_dim
    for c in range(tm // chunk):
        rs = slice(c * chunk, (c + 1) * chunk)
        for gp in range(width // LANES):
            cs = slice(gp * LANES, (gp + 1) * LANES)
            xg = gn[rs, cs]
            ma = jnp.dot(w_ref[2 * gp], xg, preferred_element_type=_F32)
            mb = jnp.dot(w_ref[2 * gp + 1], xg, preferred_element_type=_F32)
            mixed = jnp.where(left, ma, mb) + b_ref[:, cs]
            s = _gelu(u_ref[rs, cs]) * mixed
            o_ref[rs, cs] = s
    s_all = o_ref[...]
    o_ref[...] = _rms(s_all) * og_ref[...]


def _sgu(ug, norm_g, w_sp_bf16, b_full, out_g, *, width, chunk, group_dim):
    t = ug.shape[0]
    tm = min(512, t)
    assert t % tm == 0 and tm % chunk == 0 and width % LANES == 0 and 2 * group_dim == LANES
    n_groups = w_sp_bf16.shape[0]
    kern = functools.partial(_sgu_kernel, chunk=chunk, group_dim=group_dim)
    return pl.pallas_call(
        kern,
        grid=(t // tm,),
        in_specs=[
            pl.BlockSpec((tm, width), lambda i: (i, 0)),
            pl.BlockSpec((tm, width), lambda i: (i, 1)),
            pl.BlockSpec((1, width), lambda i: (0, 0)),
            pl.BlockSpec((n_groups, chunk, chunk), lambda i: (0, 0, 0)),
            pl.BlockSpec((chunk, width), lambda i: (0, 0)),
            pl.BlockSpec((1, width), lambda i: (0, 0)),
        ],
        out_specs=pl.BlockSpec((tm, width), lambda i: (i, 0)),
        out_shape=jax.ShapeDtypeStruct((t, width), _F32),
        compiler_params=_cparams(1),
        name="spatial_gating",
    )(ug, ug, norm_g, w_sp_bf16, b_full, out_g)


ROUTER_ROW_CHUNK = 128


def _out_router_kernel(a_ref, s_ref, x_ref, ag_ref, w_ref, n2g_ref, wr_ref, br_ref,
                       x2_ref, xn_ref, idx_ref, gate_ref):
    chunks = [slice(c * ROUTER_ROW_CHUNK, (c + 1) * ROUTER_ROW_CHUNK)
              for c in range(x_ref.shape[0] // ROUTER_ROW_CHUNK)]
    for rs in chunks:
        an = _rms(a_ref[rs, :]) * ag_ref[...]
        mix = jnp.concatenate([an, s_ref[rs, :]], axis=-1).astype(_BF16)
        x2_ref[rs, :] = x_ref[rs, :] + jnp.dot(mix, w_ref[...], preferred_element_type=_F32)
    all_logits = []
    for rs in chunks:
        xn = _rms(x2_ref[rs, :]) * n2g_ref[...]
        xn_ref[rs, :] = xn
        xh = xn.astype(_BF16)
        xl = (xn - xh.astype(_F32)).astype(_BF16)
        parts = jnp.dot(jnp.concatenate([xh, xl], axis=-1), wr_ref[...], preferred_element_type=_F32)
        all_logits.append(parts + pltpu.roll(parts, LANES // 2, axis=1) + br_ref[...])
    for rs, logits in zip(chunks, all_logits):
        rc = logits.shape[0]
        lane = lax.broadcasted_iota(jnp.int32, (rc, LANES), 1)
        lane_f = lane.astype(_F32)
        idx_acc = jnp.zeros((rc, LANES), _F32)
        vals = []
        cur = logits
        for k in range(TOP_K):
            m = jnp.max(cur, axis=-1, keepdims=True)
            ik = jnp.min(jnp.where(cur == m, lane_f, float(LANES)), axis=-1, keepdims=True)
            vals.append(m)
            idx_acc = jnp.where(lane == k, ik, idx_acc)
            cur = jnp.where(lane_f == ik, -jnp.inf, cur)
        exps = [jnp.exp(v - vals[0]) for v in vals]
        denom = exps[0]
        for e in exps[1:]:
            denom = denom + e
        gate_acc = jnp.zeros((rc, LANES), _F32)
        for k in range(TOP_K):
            gate_acc = jnp.where(lane == k, exps[k] / denom, gate_acc)
        idx_ref[rs, :] = idx_acc.astype(jnp.int32)
        gate_ref[rs, :] = gate_acc


def _out_router(a_out, s_out, x, attn_out_g, w_out_bf16, norm2_g, w_router_pad, b_router_pad):
    t, d = x.shape
    wa = a_out.shape[1]
    ws = s_out.shape[1]
    tm = min(512, t)
    assert t % tm == 0
    row = lambda n: pl.BlockSpec((tm, n), lambda i: (i, 0))
    full = lambda a, b: pl.BlockSpec((a, b), lambda i: (0, 0))
    return pl.pallas_call(
        _out_router_kernel,
        grid=(t // tm,),
        in_specs=[row(wa), row(ws), row(d), full(1, wa), full(wa + ws, d), full(1, d),
                  full(2 * d, LANES), full(1, LANES)],
        out_specs=[row(d), row(d), row(LANES), row(LANES)],
        out_shape=[jax.ShapeDtypeStruct((t, d), _F32), jax.ShapeDtypeStruct((t, d), _F32),
                   jax.ShapeDtypeStruct((t, LANES), jnp.int32), jax.ShapeDtypeStruct((t, LANES), _F32)],
        compiler_params=_cparams(1),
        name="out_proj_router",
    )(a_out, s_out, x, attn_out_g, w_out_bf16, norm2_g, w_router_pad, b_router_pad)


def _membership(idx):
    lane = lax.broadcasted_iota(jnp.int32, idx.shape, 1)
    return [lane == idx[:, k:k + 1] for k in range(TOP_K)]


def _count_kernel(idx_ref, cnt_ref):
    @pl.when(pl.program_id(0) == 0)
    def _():
        cnt_ref[...] = jnp.zeros_like(cnt_ref)

    member = sum(oh.astype(_F32) for oh in _membership(idx_ref[...]))
    cnt_ref[...] += jnp.sum(member, axis=0, keepdims=True)


def _slot_kernel(idx_ref, start_ref, dst_ref, carry_ref):
    @pl.when(pl.program_id(0) == 0)
    def _():
        carry_ref[...] = jnp.zeros_like(carry_ref)

    onehots = _membership(idx_ref[...])
    member = sum(oh.astype(_F32) for oh in onehots)
    tb = member.shape[0]
    earlier = (lax.broadcasted_iota(jnp.int32, (tb, tb), 0) > lax.broadcasted_iota(jnp.int32, (tb, tb), 1))
    before = jnp.dot(earlier.astype(_BF16), member.astype(_BF16), preferred_element_type=_F32)
    slot_e = start_ref[...] + carry_ref[...] + before
    lane = lax.broadcasted_iota(jnp.int32, member.shape, 1)
    dst = jnp.zeros(member.shape, _F32)
    for k in range(TOP_K):
        dk = jnp.sum(jnp.where(onehots[k], slot_e, 0.0), axis=-1, keepdims=True)
        dst = jnp.where(lane == k, dk, dst)
    dst_ref[...] = dst.astype(jnp.int32)
    carry_ref[...] += jnp.sum(member, axis=0, keepdims=True)


def _route_counts(idx_pad):
    t = idx_pad.shape[0]
    tb = min(512, t)
    return pl.pallas_call(
        _count_kernel,
        grid=(t // tb,),
        in_specs=[pl.BlockSpec((tb, LANES), lambda i: (i, 0))],
        out_specs=pl.BlockSpec((1, LANES), lambda i: (0, 0)),
        out_shape=jax.ShapeDtypeStruct((1, LANES), _F32),
        compiler_params=_cparams(1),
        name="route_counts",
    )(idx_pad)


def _route_slots(idx_pad, starts_f32):
    t = idx_pad.shape[0]
    tb = min(512, t)
    return pl.pallas_call(
        _slot_kernel,
        grid=(t // tb,),
        in_specs=[pl.BlockSpec((tb, LANES), lambda i: (i, 0)), pl.BlockSpec((1, LANES), lambda i: (0, 0))],
        out_specs=pl.BlockSpec((tb, LANES), lambda i: (i, 0)),
        out_shape=jax.ShapeDtypeStruct((t, LANES), jnp.int32),
        scratch_shapes=[pltpu.VMEM((1, LANES), _F32)],
        compiler_params=_cparams(1),
        name="route_slots",
    )(idx_pad, starts_f32)


ROUTE_TB = 512


ZERO_ROWS = 256


def _scatter_rows_kernel(pad_beg_ref, pad_mid_ref, pad_end_ref, dst_ref, xn_ref, xs_ref, zero_ref, sem, zsem):
    tb = xn_ref.shape[0]
    first = pl.program_id(0) == 0
    n_regions = pad_beg_ref.shape[0]

    def slot(ref, r, n=1):
        return ref.at[pl.ds(r, n)]

    def issue(t, c):
        for k in range(TOP_K):
            d = dst_ref[t * TOP_K + k]
            pltpu.make_async_copy(slot(xn_ref, t), slot(xs_ref, d), sem).start(priority=k % 2)
        return c

    def drain():
        for _ in range(TOP_K):
            pltpu.make_async_copy(xn_ref, slot(xs_ref, 0, tb), sem).wait()

    def clear_padding(start):
        def region(e, c):
            beg, mid, end = pad_beg_ref[e], pad_mid_ref[e], pad_end_ref[e]

            def one_row(r, c2):
                cp = pltpu.make_async_copy(slot(zero_ref, 0), slot(xs_ref, r), zsem)
                cp.start() if start else cp.wait()
                return c2

            def one_block(b, c2):
                r0 = pl.multiple_of(mid + b * ZERO_ROWS, ZERO_ROWS)
                cp = pltpu.make_async_copy(zero_ref, slot(xs_ref, r0, ZERO_ROWS), zsem)
                cp.start() if start else cp.wait()
                return c2

            lax.fori_loop(beg, mid, one_row, 0)
            lax.fori_loop(0, (end - mid) // ZERO_ROWS, one_block, 0)
            return c

        lax.fori_loop(0, n_regions, region, 0)

    @pl.when(first)
    def _():
        zero_ref[...] = jnp.zeros_like(zero_ref)
        clear_padding(True)

    lax.fori_loop(0, tb, issue, 0, unroll=4)
    drain()

    @pl.when(first)
    def _():
        clear_padding(False)


def _scatter_rows(pad_beg, pad_mid, pad_end, dst_flat, xn, n_slots):
    t, d = xn.shape
    tb = min(ROUTE_TB, t)
    grid_spec = pltpu.PrefetchScalarGridSpec(
        num_scalar_prefetch=3,
        grid=(t // tb,),
        in_specs=[
            pl.BlockSpec((tb * TOP_K,), lambda i, *_: (i,), memory_space=pltpu.SMEM),
            pl.BlockSpec((tb, d), lambda i, *_: (i, 0)),
        ],
        out_specs=pl.BlockSpec(memory_space=pl.ANY),
        scratch_shapes=[pltpu.VMEM((ZERO_ROWS, d), xn.dtype), pltpu.SemaphoreType.DMA(()),
                        pltpu.SemaphoreType.DMA(())],
    )
    return pl.pallas_call(
        _scatter_rows_kernel,
        grid_spec=grid_spec,
        out_shape=jax.ShapeDtypeStruct((n_slots, d), xn.dtype),
        compiler_params=_cparams(1),
        name="scatter_rows",
    )(pad_beg, pad_mid, pad_end, dst_flat, xn)


def _combine_kernel(dst_ref, gate_ref, x2_ref, ys_ref, o_ref, rows_ref, sems):
    tb = x2_ref.shape[0]

    def issue(t, c, sem):
        for k in range(TOP_K):
            src = dst_ref[t * TOP_K + k]
            pltpu.make_async_copy(ys_ref.at[pl.ds(src, 1)], rows_ref.at[k, pl.ds(t, 1)], sem).start(priority=k % 2)
        return c

    half = tb // 2
    for part in range(2):
        lax.fori_loop(part * half, (part + 1) * half, functools.partial(issue, sem=sems.at[part]), 0, unroll=4)
    for part in range(2):
        t0 = part * half
        for k in range(TOP_K):
            pltpu.make_async_copy(ys_ref.at[pl.ds(0, half)], rows_ref.at[k, pl.ds(t0, half)], sems.at[part]).wait()
        acc = x2_ref[t0:t0 + half, :]
        gates = gate_ref[t0:t0 + half, :]
        for k in range(TOP_K):
            acc = acc + gates[:, k:k + 1] * rows_ref[k, t0:t0 + half, :]
        o_ref[t0:t0 + half, :] = acc


def _combine(dst_flat, gates_pad, x2, ys):
    t, d = x2.shape
    tb = min(ROUTE_TB, t)
    return pl.pallas_call(
        _combine_kernel,
        grid=(t // tb,),
        in_specs=[
            pl.BlockSpec((tb * TOP_K,), lambda i: (i,), memory_space=pltpu.SMEM),
            pl.BlockSpec((tb, LANES), lambda i: (i, 0)),
            pl.BlockSpec((tb, d), lambda i: (i, 0)),
            pl.BlockSpec(memory_space=pl.ANY),
        ],
        out_specs=pl.BlockSpec((tb, d), lambda i: (i, 0)),
        out_shape=jax.ShapeDtypeStruct((t, d), _F32),
        scratch_shapes=[pltpu.VMEM((TOP_K, tb, d), _F32), pltpu.SemaphoreType.DMA((2,))],
        compiler_params=_cparams(1),
        name="combine_rows",
    )(dst_flat, gates_pad, x2, ys)


def _ffn_kernel(te_ref, tr_ref, tblk_ref, x_ref, wg_ref, wl_ref, bg_ref, bl_ref, wd_ref, bd_ref,
                o_ref, xb_ref, *, sub):
    del te_ref, tblk_ref
    i = pl.program_id(0)
    j = pl.program_id(1)
    n_rows = tr_ref[i]
    tm, d = o_ref.shape

    @pl.when(jnp.logical_and(j == 0, n_rows > 0))
    def _():
        o_ref[...] = jnp.broadcast_to(bd_ref[...], (tm, d))

    @pl.when(jnp.logical_and(j == 0, n_rows == 0))
    def _():
        o_ref[...] = jnp.zeros((tm, d), _F32)

    def rows_block(r0, size):
        rs = pl.ds(r0, size)

        @pl.when(j == 0)
        def _():
            xb_ref[rs, :] = x_ref[rs, :].astype(_BF16)

        xb = xb_ref[rs, :]
        hg = jnp.dot(xb, wg_ref[...], preferred_element_type=_F32) + bg_ref[...]
        hl = jnp.dot(xb, wl_ref[...], preferred_element_type=_F32) + bl_ref[...]
        gate = jnp.minimum(hg, SWIGLU_LIMIT)
        lin = jnp.clip(hl, -SWIGLU_LIMIT, SWIGLU_LIMIT)
        act = gate * (1.0 / (1.0 + jnp.exp(-SWIGLU_ALPHA * gate))) * (lin + 1.0)
        o_ref[rs, :] += jnp.dot(act.astype(_BF16), wd_ref[...], preferred_element_type=_F32)

    @pl.when(n_rows == tm)
    def _():
        rows_block(0, tm)

    @pl.when(n_rows < tm)
    def _():
        def sub_block(s, c):
            rows_block(pl.multiple_of(s * sub, sub), sub)
            return c

        lax.fori_loop(0, (n_rows + sub - 1) // sub, sub_block, 0)


def _expert_ffn(tile_expert, tile_rows, tile_blk, xs, w_gu_bf16, b_gu, w_d_bf16, b_d, *, tm, tn, sub):
    n_slots, d = xs.shape
    n_exp, _, de2 = w_gu_bf16.shape
    de = de2 // 2
    nj = de // tn
    n_tiles = n_slots // tm
    assert de % tn == 0 and n_slots % tm == 0 and tm % sub == 0
    grid_spec = pltpu.PrefetchScalarGridSpec(
        num_scalar_prefetch=3,
        grid=(n_tiles, nj),
        in_specs=[
            pl.BlockSpec((tm, d), lambda i, j, te, tr, tb: (tb[i], 0)),
            pl.BlockSpec((None, d, tn), lambda i, j, te, tr, tb: (te[i], 0, j)),
            pl.BlockSpec((None, d, tn), lambda i, j, te, tr, tb: (te[i], 0, nj + j)),
            pl.BlockSpec((None, 1, tn), lambda i, j, te, tr, tb: (te[i], 0, j)),
            pl.BlockSpec((None, 1, tn), lambda i, j, te, tr, tb: (te[i], 0, nj + j)),
            pl.BlockSpec((None, tn, d), lambda i, j, te, tr, tb: (te[i], j, 0)),
            pl.BlockSpec((None, 1, d), lambda i, j, te, tr, tb: (te[i], 0, 0)),
        ],
        out_specs=pl.BlockSpec((tm, d), lambda i, j, te, tr, tb: (i, 0)),
        scratch_shapes=[pltpu.VMEM((tm, d), _BF16)],
    )
    return pl.pallas_call(
        functools.partial(_ffn_kernel, sub=sub),
        grid_spec=grid_spec,
        out_shape=jax.ShapeDtypeStruct((n_slots, d), _F32),
        compiler_params=_cparams(2),
        name="expert_ffn",
    )(tile_expert, tile_rows, tile_blk, xs, w_gu_bf16, w_gu_bf16,
      b_gu.reshape(n_exp, 1, de2), b_gu.reshape(n_exp, 1, de2), w_d_bf16, b_d.reshape(n_exp, 1, d))


def _tile_plan(counts, *, tm, n_tiles):
    n_exp = counts.shape[0]
    tiles_e = (counts + tm - 1) // tm
    tile_end = jnp.cumsum(tiles_e)
    tile_beg = tile_end - tiles_e
    starts = tile_beg * tm
    n_active = tile_end[-1]
    tid = jnp.minimum(jnp.arange(n_tiles, dtype=jnp.int32), n_active - 1)
    expert = jnp.minimum(jnp.sum(tile_end[None, :] <= tid[:, None], axis=1), n_exp - 1).astype(jnp.int32)
    rows = jnp.clip(counts[expert] - (tid - tile_beg[expert]) * tm, 0, tm)
    rows = jnp.where(jnp.arange(n_tiles) < n_active, rows, 0).astype(jnp.int32)
    pad_beg = jnp.concatenate([starts + counts, (n_active * tm)[None]])
    pad_end = jnp.concatenate([tile_end * tm, jnp.full((1,), n_tiles * tm, jnp.int32)])
    pad_mid = jnp.minimum((pad_beg + ZERO_ROWS - 1) // ZERO_ROWS * ZERO_ROWS, pad_end)
    pads = tuple(p.astype(jnp.int32) for p in (pad_beg, pad_mid, pad_end))
    return starts, expert, rows, tid.astype(jnp.int32), pads


def _layer(x, norm1_g, w_in, q_norm_g, k_norm_g, rpb, sgu_norm_g, w_spatial, b_spatial, attn_out_g,
           sgu_out_g, w_out, norm2_g, w_router, b_router, w_gate_up, b_gate_up, w_down, b_down):
    batch, seq, d = x.shape
    t = batch * seq
    head_dim = q_norm_g.shape[-1]
    n_heads = rpb.shape[0]
    attn_width = n_heads * head_dim
    n_groups, chunk, _ = w_spatial.shape
    sgu_width = sgu_norm_g.shape[-1]
    group_dim = sgu_width // n_groups
    win_rows_max = (rpb.shape[1] + 1) // 2
    win_cols = (rpb.shape[2] + 1) // 2
    rows = seq // GRID_W
    kh = min(win_rows_max, rows)
    n_exp = w_router.shape[-1]
    assert TOP_K <= n_exp <= LANES // 2 and seq % GRID_W == 0 and w_in.shape[1] == 3 * attn_width + 2 * sgu_width

    xf = x.reshape(t, d)
    row = lambda v: v.reshape(1, -1).astype(_F32)

    w_in_b = w_in.astype(_BF16)
    pair = lambda g: jnp.concatenate([g, g]).reshape(1, 1, LANES)
    head_gain = jnp.concatenate([pair(q_norm_g), pair(k_norm_g), jnp.ones((1, 1, LANES), _F32)], axis=0)
    qkv = _norm_proj(xf, row(norm1_g), w_in_b[:, :3 * attn_width], head_gain,
                     slab_out=True, attn_width=attn_width, head_dim=head_dim)
    ug = _norm_proj(xf, row(norm1_g), w_in_b[:, 3 * attn_width:], head_gain,
                    slab_out=False, attn_width=attn_width, head_dim=head_dim)

    bias_tab = _attention_bias_table(rpb, kh=kh, win_cols=win_cols)
    w_gu2d = w_gate_up.reshape(-1, w_gate_up.shape[-1])
    n_steps = _attention_steps(batch, seq, n_heads, kh)
    ride = w_gu2d.shape[0] % (16 * n_steps) == 0
    a_out, cast = _attention(qkv, bias_tab, (w_gu2d,) if ride else (),
                             batch=batch, seq=seq, n_heads=n_heads, head_dim=head_dim, kh=kh)
    w_gu_b = (cast[0] if ride else w_gu2d.astype(_BF16)).reshape(w_gate_up.shape)
    w_d_b = w_down.astype(_BF16)

    b_full = jnp.repeat(b_spatial.T, group_dim, axis=1)
    s_out = _sgu(ug, row(sgu_norm_g), w_spatial.astype(_BF16), b_full, row(sgu_out_g),
                 width=sgu_width, chunk=chunk, group_dim=group_dim)

    half = LANES // 2
    w_r_hi = w_router.astype(_BF16)
    w_r_lo = (w_router - w_r_hi.astype(_F32)).astype(_BF16)
    blank = jnp.zeros((d, LANES), _BF16)
    w_router_pad = jnp.concatenate([blank.at[:, :n_exp].set(w_r_hi).at[:, half:half + n_exp].set(w_r_lo),
                                    blank.at[:, :n_exp].set(w_r_hi)], axis=0)
    b_router_pad = jnp.full((1, LANES), NEG_BIG, _F32).at[0, :n_exp].set(b_router)
    x2, xn, idx_pad, gates_pad = _out_router(a_out, s_out, xf, row(attn_out_g), w_out.astype(_BF16),
                                             row(norm2_g), w_router_pad, b_router_pad)

    de = w_down.shape[1]
    tm = min(1024, t)
    tn = min(512, de)
    sub = min(256, tm)
    n_tiles = (t * TOP_K) // tm + n_exp
    counts = _route_counts(idx_pad)[0, :n_exp].astype(jnp.int32)
    assert tm % ZERO_ROWS == 0
    starts, tile_expert, tile_rows, tile_blk, pads = _tile_plan(counts, tm=tm, n_tiles=n_tiles)
    starts_pad = jnp.zeros((1, LANES), _F32).at[0, :n_exp].set(starts.astype(_F32))
    dst_pad = _route_slots(idx_pad, starts_pad)
    dst_flat = dst_pad[:, :TOP_K].reshape(-1)

    xs = _scatter_rows(*pads, dst_flat, xn, n_tiles * tm)
    ys = _expert_ffn(tile_expert, tile_rows, tile_blk, xs, w_gu_b, b_gate_up, w_d_b, b_down, tm=tm, tn=tn, sub=sub)
    out = _combine(dst_flat, gates_pad, x2, ys)
    return out.reshape(batch, seq, d)


def kernel(x, norm1_g, w_in, q_norm_g, k_norm_g, rpb, sgu_norm_g, w_spatial, b_spatial, attn_out_g,
           sgu_out_g, w_out, norm2_g, w_router, b_router, w_gate_up, b_gate_up, w_down, b_down):
    depth = norm1_g.shape[0]
    for l in range(depth):
        x = _layer(x, norm1_g[l], w_in[l], q_norm_g[l], k_norm_g[l], rpb[l], sgu_norm_g[l], w_spatial[l],
                   b_spatial[l], attn_out_g[l], sgu_out_g[l], w_out[l], norm2_g[l], w_router[l], b_router[l],
                   w_gate_up[l], b_gate_up[l], w_down[l], b_down[l])
    return x
```

```python
import functools

import jax
import jax.numpy as jnp
import numpy as np
from jax import lax
from jax.experimental import pallas as pl
from jax.experimental.pallas import tpu as pltpu

GRID_W = 64
TOP_K = 4
SWIGLU_LIMIT = 7.0
SWIGLU_ALPHA = 1.702
EPS = 1e-6
LANES = 128
NEG_BIG = -1e30
VMEM_LIMIT_BYTES = 56 * 1024 * 1024

_F32 = jnp.float32
_BF16 = jnp.bfloat16


def _cparams(n_axes):
    return pltpu.CompilerParams(
        dimension_semantics=("arbitrary",) * n_axes,
        vmem_limit_bytes=VMEM_LIMIT_BYTES)


def _rms(x):
    return x * lax.rsqrt(jnp.mean(x * x, axis=-1, keepdims=True) + EPS)


def _gelu(x):
    return 0.5 * x * (1.0 + lax.erf(x * 0.7071067811865476))


def _norm_proj_kernel(x_ref, g_ref, w_ref, hg_ref, o_ref, h_ref, *, slab_out, n_norm_tiles, head_dim):
    j = pl.program_id(1)

    @pl.when(j == 0)
    def _():
        h_ref[...] = (_rms(x_ref[...]) * g_ref[...]).astype(_BF16)

    if not slab_out:
        o_ref[...] = jnp.dot(h_ref[...], w_ref[...], preferred_element_type=_F32)
        return

    left = lax.broadcasted_iota(jnp.int32, (1, LANES), 1) < head_dim
    normed = j < n_norm_tiles
    tn = w_ref.shape[1]
    piece = min(2 * LANES, tn)
    for p in range(tn // piece):
        res = jnp.dot(h_ref[...], w_ref[:, p * piece:(p + 1) * piece], preferred_element_type=_F32)
        for s in range(piece // LANES):
            r = res[:, s * LANES:(s + 1) * LANES]
            ss = r * r
            ss_first = jnp.sum(jnp.where(left, ss, 0.0), axis=-1, keepdims=True)
            ss_second = jnp.sum(jnp.where(left, 0.0, ss), axis=-1, keepdims=True)
            inv = jnp.where(left, lax.rsqrt(ss_first / head_dim + EPS),
                            lax.rsqrt(ss_second / head_dim + EPS)) * hg_ref[0]
            o_ref[p * (piece // LANES) + s] = (r * jnp.where(normed, inv, 1.0)).astype(o_ref.dtype)


def _norm_proj(x, g, w_bf16, head_gain, *, slab_out, attn_width, head_dim):
    t, d = x.shape
    n = w_bf16.shape[1]
    tm = min(1024, t)
    tn = min(1024, attn_width if slab_out else n)
    assert t % tm == 0 and n % tn == 0 and tn % LANES == 0
    if slab_out:
        assert attn_width % tn == 0 and 2 * head_dim == LANES
        tiles_per_part = attn_width // tn
        out_shape = jax.ShapeDtypeStruct((n // LANES, t, LANES), _BF16)
        out_spec = pl.BlockSpec((tn // LANES, tm, LANES), lambda i, j: (j, i, 0))
        hg_spec = pl.BlockSpec((1, 1, LANES), lambda i, j: (jnp.minimum(j // tiles_per_part, 2), 0, 0))
        n_norm_tiles = 2 * tiles_per_part
    else:
        out_shape = jax.ShapeDtypeStruct((t, n), _F32)
        out_spec = pl.BlockSpec((tm, tn), lambda i, j: (i, j))
        hg_spec = pl.BlockSpec((1, 1, LANES), lambda i, j: (0, 0, 0))
        n_norm_tiles = 0
    kern = functools.partial(_norm_proj_kernel, slab_out=slab_out, n_norm_tiles=n_norm_tiles, head_dim=head_dim)
    return pl.pallas_call(
        kern,
        grid=(t // tm, n // tn),
        in_specs=[
            pl.BlockSpec((tm, d), lambda i, j: (i, 0)),
            pl.BlockSpec((1, d), lambda i, j: (0, 0)),
            pl.BlockSpec((d, tn), lambda i, j: (0, j)),
            hg_spec,
        ],
        out_specs=out_spec,
        out_shape=out_shape,
        scratch_shapes=[pltpu.VMEM((tm, d), _BF16)],
        compiler_params=_cparams(2),
        name="norm_proj_qkv" if slab_out else "norm_proj_ug",
    )(x, g, w_bf16, head_gain)


def _attn_kernel(q_ref, kp_ref, kc_ref, kn_ref, vp_ref, vc_ref, vn_ref, bias_ref, *rest,
                 rows, rblk, kh, head_dim, scale, n_cast):
    cast_in, (o_ref,), cast_out, (kwin, vwin) = (rest[:n_cast], rest[n_cast:n_cast + 1],
                                                rest[n_cast + 1:2 * n_cast + 1], rest[2 * n_cast + 1:])
    for src, dst in zip(cast_in, cast_out):
        dst[...] = src[...].astype(dst.dtype)
    rb = pl.program_id(2)
    blk = rblk * GRID_W
    for w, (kr, vr) in enumerate(((kp_ref, vp_ref), (kc_ref, vc_ref), (kn_ref, vn_ref))):
        kwin[w * blk:(w + 1) * blk, :] = kr[...]
        vwin[w * blk:(w + 1) * blk, :] = vr[...]
    lane = lax.broadcasted_iota(jnp.int32, (1, LANES), 1)
    left = lane < head_dim
    nk = (kh + 2) * GRID_W
    scores, offsets = [], []
    for pair in range(rblk // 2):
        r = rb * rblk + 2 * pair
        rs_a = jnp.clip(r - kh // 2, 0, rows - kh)
        rs_b = jnp.clip(r + 1 - kh // 2, 0, rows - kh)
        kind = _pair_kind_index(r - rs_a, rs_b - rs_a, kh)
        off = pl.multiple_of((rs_a - (rb * rblk - rblk)) * GRID_W, GRID_W)
        q2 = q_ref[2 * pair * GRID_W:(2 * pair + 2) * GRID_W, :] * scale
        qa, qb = q2[:GRID_W], q2[GRID_W:]
        qm = jnp.concatenate([jnp.where(left, qa, 0.0), jnp.where(left, qb, 0.0),
                              jnp.where(left, 0.0, qa), jnp.where(left, 0.0, qb)], axis=0).astype(_BF16)
        s = lax.dot_general(qm, kwin[pl.ds(off, nk), :], (((1,), (1,)), ((), ())), preferred_element_type=_F32)
        scores.append(s + bias_ref[kind])
        offsets.append(off)
    for pair in range(rblk // 2):
        s = scores[pair]
        p = jnp.exp(s - jnp.max(s, axis=-1, keepdims=True))
        denom = jnp.sum(p, axis=-1, keepdims=True)
        pv = jnp.dot(p.astype(_BF16), vwin[pl.ds(offsets[pair], nk), :], preferred_element_type=_F32) / denom
        o_ref[2 * pair * GRID_W:(2 * pair + 2) * GRID_W, :] = jnp.where(
            left, pv[0:2 * GRID_W], pv[2 * GRID_W:4 * GRID_W])


def _attention_steps(batch, seq, n_heads, kh):
    return batch * (n_heads // 2) * (seq // GRID_W // kh)


def _attention(qkv, bias_tab, cast_srcs, *, batch, seq, n_heads, head_dim, kh):
    n_hp = n_heads // 2
    rows = seq // GRID_W
    rblk = kh
    assert rows % rblk == 0 and rows >= kh and rblk >= 4
    nrb = rows // rblk
    blk = rblk * GRID_W
    t = batch * seq
    n_steps = batch * n_hp * nrb
    cast_specs = []
    for src in cast_srcs:
        assert src.shape[0] % n_steps == 0 and (src.shape[0] // n_steps) % 16 == 0
        cast_specs.append(pl.BlockSpec((src.shape[0] // n_steps, src.shape[1]),
                                       lambda b, hp, rb: ((b * n_hp + hp) * nrb + rb, 0)))

    def qmap(b, hp, rb):
        return (hp, b * nrb + rb, 0)

    def kvmap(part, delta):
        def f(b, hp, rb):
            return (part * n_hp + hp, b * nrb + jnp.clip(rb + delta, 0, nrb - 1), 0)
        return f

    slab = lambda m: pl.BlockSpec((None, blk, LANES), m)
    kern = functools.partial(_attn_kernel, rows=rows, rblk=rblk, kh=kh, head_dim=head_dim,
                             scale=float(head_dim) ** -0.5, n_cast=len(cast_srcs))
    outs = pl.pallas_call(
        kern,
        grid=(batch, n_hp, nrb),
        in_specs=[
            slab(qmap),
            slab(kvmap(1, -1)), slab(kvmap(1, 0)), slab(kvmap(1, 1)),
            slab(kvmap(2, -1)), slab(kvmap(2, 0)), slab(kvmap(2, 1)),
            pl.BlockSpec((len(_pair_kinds(kh)), None, 4 * GRID_W, (kh + 2) * GRID_W),
                         lambda b, hp, rb: (0, hp, 0, 0)),
        ] + cast_specs,
        out_specs=[pl.BlockSpec((blk, LANES), lambda b, hp, rb: (b * nrb + rb, hp))] + cast_specs,
        out_shape=[jax.ShapeDtypeStruct((t, n_hp * LANES), _F32)]
                  + [jax.ShapeDtypeStruct(src.shape, _BF16) for src in cast_srcs],
        scratch_shapes=[pltpu.VMEM((3 * blk, LANES), _BF16), pltpu.VMEM((3 * blk, LANES), _BF16)],
        compiler_params=_cparams(3),
        name="nbr_attention",
    )(qkv, qkv, qkv, qkv, qkv, qkv, qkv, bias_tab, *cast_srcs)
    return outs[0], outs[1:]


def _pair_kinds(kh):
    assert kh % 4 == 0
    return sorted([(s, 0) for s in range(0, kh, 2)] + [(kh // 2, 1)], key=lambda sd: sd[0] + sd[1])


def _pair_kind_index(shift_a, delta, kh):
    code = shift_a + delta
    return code // 2 + (code > kh // 2).astype(jnp.int32)


def _attention_bias_table(rpb, *, kh, win_cols):
    n_heads, n_rpb_rows, n_rpb_cols = rpb.shape
    win_rows_max = (n_rpb_rows + 1) // 2
    nkr = kh + 2
    qc = np.arange(GRID_W)[:, None]
    kc = np.arange(GRID_W)[None, :]
    col_start = np.clip(qc - win_cols // 2, 0, GRID_W - win_cols)
    col_valid = (kc >= col_start) & (kc < col_start + win_cols)
    col_sel = np.zeros((GRID_W, GRID_W, n_rpb_cols), np.float32)
    qi, ki = np.nonzero(col_valid)
    col_sel[qi, ki, ki - qi + win_cols - 1] = 1.0
    kinds = _pair_kinds(kh)
    row_sel = np.zeros((len(kinds), 2, nkr, n_rpb_rows), np.float32)
    row_valid = np.zeros((len(kinds), 2, nkr), bool)
    for kind, (shift_a, delta) in enumerate(kinds):
        for member, (shift, dlt) in enumerate(((shift_a, 0), (shift_a + 1 - delta, delta))):
            for jj in range(nkr):
                if 0 <= jj - dlt < kh:
                    row_sel[kind, member, jj, jj - dlt - shift + win_rows_max - 1] = 1.0
                    row_valid[kind, member, jj] = True
    exact = lax.Precision.HIGHEST
    cols = jnp.einsum("hrc,qkc->rhqk", rpb.astype(_F32), col_sel, precision=exact)
    tab = jnp.einsum("ymjr,rhqk->yhmqjk", row_sel, cols, precision=exact)
    keep = row_valid[:, None, :, None, :, None] & col_valid[None, None, None, :, None, :]
    tab = jnp.where(keep, tab, NEG_BIG)
    return tab.reshape(len(kinds), n_heads // 2, 4 * GRID_W, nkr * GRID_W)


def _sgu_kernel(u_ref, g_ref, ng_ref, w_ref, b_ref, og_ref, o_ref, *, chunk, group_dim):
    tm, width = u_ref.shape
    gg = _gelu(g_ref[...])
    gn = (_rms(gg) * ng_ref[...]).astype(_BF16)
    lane = lax.broadcasted_iota(jnp.int32, (1, LANES), 1)
    left = lane < group_dim
    for c in range(tm // chunk):
        rs = slice(c * chunk, (c + 1) * chunk)
        for gp in range(width // LANES):
            cs = slice(gp * LANES, (gp + 1) * LANES)
            xg = gn[rs, cs]
            ma = jnp.dot(w_ref[2 * gp], xg, preferred_element_type=_F32)
            mb = jnp.dot(w_ref[2 * gp + 1], xg, preferred_element_type=_F32)
            mixed = jnp.where(left, ma, mb) + b_ref[:, cs]
            s = _gelu(u_ref[rs, cs]) * mixed
            o_ref[rs, cs] = s
    s_all = o_ref[...]
    o_ref[...] = _rms(s_all) * og_ref[...]


def _sgu(ug, norm_g, w_sp_bf16, b_full, out_g, *, width, chunk, group_dim):
    t = ug.shape[0]
    tm = min(512, t)
    assert t % tm == 0 and tm % chunk == 0 and width % LANES == 0 and 2 * group_dim == LANES
    n_groups = w_sp_bf16.shape[0]
    kern = functools.partial(_sgu_kernel, chunk=chunk, group_dim=group_dim)
    return pl.pallas_call(
        kern,
        grid=(t // tm,),
        in_specs=[
            pl.BlockSpec((tm, width), lambda i: (i, 0)),
            pl.BlockSpec((tm, width), lambda i: (i, 1)),
            pl.BlockSpec((1, width), lambda i: (0, 0)),
            pl.BlockSpec((n_groups, chunk, chunk), lambda i: (0, 0, 0)),
            pl.BlockSpec((chunk, width), lambda i: (0, 0)),
            pl.BlockSpec((1, width), lambda i: (0, 0)),
        ],
        out_specs=pl.BlockSpec((tm, width), lambda i: (i, 0)),
        out_shape=jax.ShapeDtypeStruct((t, width), _F32),
        compiler_params=_cparams(1),
        name="spatial_gating",
    )(ug, ug, norm_g, w_sp_bf16, b_full, out_g)


ROUTER_ROW_CHUNK = 128


def _out_router_kernel(a_ref, s_ref, x_ref, ag_ref, w_ref, n2g_ref, wr_ref, br_ref,
                       x2_ref, xn_ref, idx_ref, gate_ref):
    chunks = [slice(c * ROUTER_ROW_CHUNK, (c + 1) * ROUTER_ROW_CHUNK)
              for c in range(x_ref.shape[0] // ROUTER_ROW_CHUNK)]
    for rs in chunks:
        an = _rms(a_ref[rs, :]) * ag_ref[...]
        mix = jnp.concatenate([an, s_ref[rs, :]], axis=-1).astype(_BF16)
        x2_ref[rs, :] = x_ref[rs, :] + jnp.dot(mix, w_ref[...], preferred_element_type=_F32)
    all_logits = []
    for rs in chunks:
        xn = _rms(x2_ref[rs, :]) * n2g_ref[...]
        xn_ref[rs, :] = xn
        xh = xn.astype(_BF16)
        xl = (xn - xh.astype(_F32)).astype(_BF16)
        parts = jnp.dot(jnp.concatenate([xh, xl], axis=-1), wr_ref[...], preferred_element_type=_F32)
        all_logits.append(parts + pltpu.roll(parts, LANES // 2, axis=1) + br_ref[...])
    for rs, logits in zip(chunks, all_logits):
        rc = logits.shape[0]
        lane = lax.broadcasted_iota(jnp.int32, (rc, LANES), 1)
        lane_f = lane.astype(_F32)
        idx_acc = jnp.zeros((rc, LANES), _F32)
        vals = []
        cur = logits
        for k in range(TOP_K):
            m = jnp.max(cur, axis=-1, keepdims=True)
            ik = jnp.min(jnp.where(cur == m, lane_f, float(LANES)), axis=-1, keepdims=True)
            vals.append(m)
            idx_acc = jnp.where(lane == k, ik, idx_acc)
            cur = jnp.where(lane_f == ik, -jnp.inf, cur)
        exps = [jnp.exp(v - vals[0]) for v in vals]
        denom = exps[0]
        for e in exps[1:]:
            denom = denom + e
        gate_acc = jnp.zeros((rc, LANES), _F32)
        for k in range(TOP_K):
            gate_acc = jnp.where(lane == k, exps[k] / denom, gate_acc)
        idx_ref[rs, :] = idx_acc.astype(jnp.int32)
        gate_ref[rs, :] = gate_acc


def _out_router(a_out, s_out, x, attn_out_g, w_out_bf16, norm2_g, w_router_pad, b_router_pad):
    t, d = x.shape
    wa = a_out.shape[1]
    ws = s_out.shape[1]
    tm = min(512, t)
    assert t % tm == 0
    row = lambda n: pl.BlockSpec((tm, n), lambda i: (i, 0))
    full = lambda a, b: pl.BlockSpec((a, b), lambda i: (0, 0))
    return pl.pallas_call(
        _out_router_kernel,
        grid=(t // tm,),
        in_specs=[row(wa), row(ws), row(d), full(1, wa), full(wa + ws, d), full(1, d),
                  full(2 * d, LANES), full(1, LANES)],
        out_specs=[row(d), row(d), row(LANES), row(LANES)],
        out_shape=[jax.ShapeDtypeStruct((t, d), _F32), jax.ShapeDtypeStruct((t, d), _F32),
                   jax.ShapeDtypeStruct((t, LANES), jnp.int32), jax.ShapeDtypeStruct((t, LANES), _F32)],
        compiler_params=_cparams(1),
        name="out_proj_router",
    )(a_out, s_out, x, attn_out_g, w_out_bf16, norm2_g, w_router_pad, b_router_pad)


def _membership(idx):
    lane = lax.broadcasted_iota(jnp.int32, idx.shape, 1)
    return [lane == idx[:, k:k + 1] for k in range(TOP_K)]


def _count_kernel(idx_ref, cnt_ref):
    @pl.when(pl.program_id(0) == 0)
    def _():
        cnt_ref[...] = jnp.zeros_like(cnt_ref)

    member = sum(oh.astype(_F32) for oh in _membership(idx_ref[...]))
    cnt_ref[...] += jnp.sum(member, axis=0, keepdims=True)


def _slot_kernel(idx_ref, start_ref, dst_ref, carry_ref):
    @pl.when(pl.program_id(0) == 0)
    def _():
        carry_ref[...] = jnp.zeros_like(carry_ref)

    onehots = _membership(idx_ref[...])
    member = sum(oh.astype(_F32) for oh in onehots)
    tb = member.shape[0]
    earlier = (lax.broadcasted_iota(jnp.int32, (tb, tb), 0) > lax.broadcasted_iota(jnp.int32, (tb, tb), 1))
    before = jnp.dot(earlier.astype(_BF16), member.astype(_BF16), preferred_element_type=_F32)
    slot_e = start_ref[...] + carry_ref[...] + before
    lane = lax.broadcasted_iota(jnp.int32, member.shape, 1)
    dst = jnp.zeros(member.shape, _F32)
    for k in range(TOP_K):
        dk = jnp.sum(jnp.where(onehots[k], slot_e, 0.0), axis=-1, keepdims=True)
        dst = jnp.where(lane == k, dk, dst)
    dst_ref[...] = dst.astype(jnp.int32)
    carry_ref[...] += jnp.sum(member, axis=0, keepdims=True)


def _route_counts(idx_pad):
    t = idx_pad.shape[0]
    tb = min(512, t)
    return pl.pallas_call(
        _count_kernel,
        grid=(t // tb,),
        in_specs=[pl.BlockSpec((tb, LANES), lambda i: (i, 0))],
        out_specs=pl.BlockSpec((1, LANES), lambda i: (0, 0)),
        out_shape=jax.ShapeDtypeStruct((1, LANES), _F32),
        compiler_params=_cparams(1),
        name="route_counts",
    )(idx_pad)


def _route_slots(idx_pad, starts_f32):
    t = idx_pad.shape[0]
    tb = min(512, t)
    return pl.pallas_call(
        _slot_kernel,
        grid=(t // tb,),
        in_specs=[pl.BlockSpec((tb, LANES), lambda i: (i, 0)), pl.BlockSpec((1, LANES), lambda i: (0, 0))],
        out_specs=pl.BlockSpec((tb, LANES), lambda i: (i, 0)),
        out_shape=jax.ShapeDtypeStruct((t, LANES), jnp.int32),
        scratch_shapes=[pltpu.VMEM((1, LANES), _F32)],
        compiler_params=_cparams(1),
        name="route_slots",
    )(idx_pad, starts_f32)


ROUTE_TB = 512


ZERO_ROWS = 256


def _scatter_rows_kernel(pad_beg_ref, pad_mid_ref, pad_end_ref, dst_ref, xn_ref, xs_ref, zero_ref, sem, zsem):
    tb = xn_ref.shape[0]
    first = pl.program_id(0) == 0
    n_regions = pad_beg_ref.shape[0]

    def slot(ref, r, n=1):
        return ref.at[pl.ds(r, n)]

    def issue(t, c):
        for k in range(TOP_K):
            d = dst_ref[t * TOP_K + k]
            pltpu.make_async_copy(slot(xn_ref, t), slot(xs_ref, d), sem).start(priority=k % 2)
        return c

    def drain():
        for _ in range(TOP_K):
            pltpu.make_async_copy(xn_ref, slot(xs_ref, 0, tb), sem).wait()

    def clear_padding(start):
        def region(e, c):
            beg, mid, end = pad_beg_ref[e], pad_mid_ref[e], pad_end_ref[e]

            def one_row(r, c2):
                cp = pltpu.make_async_copy(slot(zero_ref, 0), slot(xs_ref, r), zsem)
                cp.start() if start else cp.wait()
                return c2

            def one_block(b, c2):
                r0 = pl.multiple_of(mid + b * ZERO_ROWS, ZERO_ROWS)
                cp = pltpu.make_async_copy(zero_ref, slot(xs_ref, r0, ZERO_ROWS), zsem)
                cp.start() if start else cp.wait()
                return c2

            lax.fori_loop(beg, mid, one_row, 0)
            lax.fori_loop(0, (end - mid) // ZERO_ROWS, one_block, 0)
            return c

        lax.fori_loop(0, n_regions, region, 0)

    @pl.when(first)
    def _():
        zero_ref[...] = jnp.zeros_like(zero_ref)
        clear_padding(True)

    lax.fori_loop(0, tb, issue, 0, unroll=4)
    drain()

    @pl.when(first)
    def _():
        clear_padding(False)


def _scatter_rows(pad_beg, pad_mid, pad_end, dst_flat, xn, n_slots):
    t, d = xn.shape
    tb = min(ROUTE_TB, t)
    grid_spec = pltpu.PrefetchScalarGridSpec(
        num_scalar_prefetch=3,
        grid=(t // tb,),
        in_specs=[
            pl.BlockSpec((tb * TOP_K,), lambda i, *_: (i,), memory_space=pltpu.SMEM),
            pl.BlockSpec((tb, d), lambda i, *_: (i, 0)),
        ],
        out_specs=pl.BlockSpec(memory_space=pl.ANY),
        scratch_shapes=[pltpu.VMEM((ZERO_ROWS, d), xn.dtype), pltpu.SemaphoreType.DMA(()),
                        pltpu.SemaphoreType.DMA(())],
    )
    return pl.pallas_call(
        _scatter_rows_kernel,
        grid_spec=grid_spec,
        out_shape=jax.ShapeDtypeStruct((n_slots, d), xn.dtype),
        compiler_params=_cparams(1),
        name="scatter_rows",
    )(pad_beg, pad_mid, pad_end, dst_flat, xn)


def _combine_kernel(dst_ref, gate_ref, x2_ref, ys_ref, o_ref, rows_ref, sems):
    tb = x2_ref.shape[0]

    def issue(t, c, sem):
        for k in range(TOP_K):
            src = dst_ref[t * TOP_K + k]
            pltpu.make_async_copy(ys_ref.at[pl.ds(src, 1)], rows_ref.at[k, pl.ds(t, 1)], sem).start(priority=k % 2)
        return c

    half = tb // 2
    for part in range(2):
        lax.fori_loop(part * half, (part + 1) * half, functools.partial(issue, sem=sems.at[part]), 0, unroll=4)
    for part in range(2):
        t0 = part * half
        for k in range(TOP_K):
            pltpu.make_async_copy(ys_ref.at[pl.ds(0, half)], rows_ref.at[k, pl.ds(t0, half)], sems.at[part]).wait()
        acc = x2_ref[t0:t0 + half, :]
        gates = gate_ref[t0:t0 + half, :]
        for k in range(TOP_K):
            acc = acc + gates[:, k:k + 1] * rows_ref[k, t0:t0 + half, :]
        o_ref[t0:t0 + half, :] = acc


def _combine(dst_flat, gates_pad, x2, ys):
    t, d = x2.shape
    tb = min(ROUTE_TB, t)
    return pl.pallas_call(
        _combine_kernel,
        grid=(t // tb,),
        in_specs=[
            pl.BlockSpec((tb * TOP_K,), lambda i: (i,), memory_space=pltpu.SMEM),
            pl.BlockSpec((tb, LANES), lambda i: (i, 0)),
            pl.BlockSpec((tb, d), lambda i: (i, 0)),
            pl.BlockSpec(memory_space=pl.ANY),
        ],
        out_specs=pl.BlockSpec((tb, d), lambda i: (i, 0)),
        out_shape=jax.ShapeDtypeStruct((t, d), _F32),
        scratch_shapes=[pltpu.VMEM((TOP_K, tb, d), _F32), pltpu.SemaphoreType.DMA((2,))],
        compiler_params=_cparams(1),
        name="combine_rows",
    )(dst_flat, gates_pad, x2, ys)


def _ffn_kernel(te_ref, tr_ref, tblk_ref, x_ref, wg_ref, wl_ref, bg_ref, bl_ref, wd_ref, bd_ref,
                o_ref, xb_ref, *, sub):
    del te_ref, tblk_ref
    i = pl.program_id(0)
    j = pl.program_id(1)
    n_rows = tr_ref[i]
    tm, d = o_ref.shape

    @pl.when(jnp.logical_and(j == 0, jnp.logical_and(n_rows > 0, n_rows < tm)))
    def _():
        o_ref[...] = jnp.broadcast_to(bd_ref[...], (tm, d))

    @pl.when(jnp.logical_and(j == 0, n_rows == 0))
    def _():
        o_ref[...] = jnp.zeros((tm, d), _F32)

    def rows_block(r0, size, first):
        rs = pl.ds(r0, size)
        if first is None:
            @pl.when(j == 0)
            def _():
                xb_ref[rs, :] = x_ref[rs, :].astype(_BF16)
        elif first:
            xb_ref[rs, :] = x_ref[rs, :].astype(_BF16)

        xb = xb_ref[rs, :]
        hg = jnp.dot(xb, wg_ref[...], preferred_element_type=_F32) + bg_ref[...]
        hl = jnp.dot(xb, wl_ref[...], preferred_element_type=_F32) + bl_ref[...]
        gate = jnp.minimum(hg, SWIGLU_LIMIT)
        lin = jnp.clip(hl, -SWIGLU_LIMIT, SWIGLU_LIMIT)
        act = gate * (1.0 / (1.0 + jnp.exp(-SWIGLU_ALPHA * gate))) * (lin + 1.0)
        part = jnp.dot(act.astype(_BF16), wd_ref[...], preferred_element_type=_F32)
        if first:
            o_ref[rs, :] = bd_ref[...] + part
        else:
            o_ref[rs, :] += part

    @pl.when(jnp.logical_and(n_rows == tm, j == 0))
    def _():
        rows_block(0, tm, True)

    @pl.when(jnp.logical_and(n_rows == tm, j > 0))
    def _():
        rows_block(0, tm, False)

    @pl.when(n_rows < tm)
    def _():
        def sub_block(s, c):
            rows_block(pl.multiple_of(s * sub, sub), sub, None)
            return c

        lax.fori_loop(0, (n_rows + sub - 1) // sub, sub_block, 0)


def _expert_ffn(tile_expert, tile_rows, tile_blk, xs, w_gu_bf16, b_gu, w_d_bf16, b_d, *, tm, tn, sub):
    n_slots, d = xs.shape
    n_exp, _, de2 = w_gu_bf16.shape
    de = de2 // 2
    nj = de // tn
    n_tiles = n_slots // tm
    assert de % tn == 0 and n_slots % tm == 0 and tm % sub == 0
    grid_spec = pltpu.PrefetchScalarGridSpec(
        num_scalar_prefetch=3,
        grid=(n_tiles, nj),
        in_specs=[
            pl.BlockSpec((tm, d), lambda i, j, te, tr, tb: (tb[i], 0)),
            pl.BlockSpec((None, d, tn), lambda i, j, te, tr, tb: (te[i], 0, j)),
            pl.BlockSpec((None, d, tn), lambda i, j, te, tr, tb: (te[i], 0, nj + j)),
            pl.BlockSpec((None, 1, tn), lambda i, j, te, tr, tb: (te[i], 0, j)),
            pl.BlockSpec((None, 1, tn), lambda i, j, te, tr, tb: (te[i], 0, nj + j)),
            pl.BlockSpec((None, tn, d), lambda i, j, te, tr, tb: (te[i], j, 0)),
            pl.BlockSpec((None, 1, d), lambda i, j, te, tr, tb: (te[i], 0, 0)),
        ],
        out_specs=pl.BlockSpec((tm, d), lambda i, j, te, tr, tb: (i, 0)),
        scratch_shapes=[pltpu.VMEM((tm, d), _BF16)],
    )
    return pl.pallas_call(
        functools.partial(_ffn_kernel, sub=sub),
        grid_spec=grid_spec,
        out_shape=jax.ShapeDtypeStruct((n_slots, d), _F32),
        compiler_params=_cparams(2),
        name="expert_ffn",
    )(tile_expert, tile_rows, tile_blk, xs, w_gu_bf16, w_gu_bf16,
      b_gu.reshape(n_exp, 1, de2), b_gu.reshape(n_exp, 1, de2), w_d_bf16, b_d.reshape(n_exp, 1, d))


def _tile_plan(counts, *, tm, n_tiles):
    n_exp = counts.shape[0]
    tiles_e = (counts + tm - 1) // tm
    tile_end = jnp.cumsum(tiles_e)
    tile_beg = tile_end - tiles_e
    starts = tile_beg * tm
    n_active = tile_end[-1]
    tid = jnp.minimum(jnp.arange(n_tiles, dtype=jnp.int32), n_active - 1)
    expert = jnp.minimum(jnp.sum(tile_end[None, :] <= tid[:, None], axis=1), n_exp - 1).astype(jnp.int32)
    rows = jnp.clip(counts[expert] - (tid - tile_beg[expert]) * tm, 0, tm)
    rows = jnp.where(jnp.arange(n_tiles) < n_active, rows, 0).astype(jnp.int32)
    pad_beg = jnp.concatenate([starts + counts, (n_active * tm)[None]])
    pad_end = jnp.concatenate([tile_end * tm, jnp.full((1,), n_tiles * tm, jnp.int32)])
    pad_mid = jnp.minimum((pad_beg + ZERO_ROWS - 1) // ZERO_ROWS * ZERO_ROWS, pad_end)
    pads = tuple(p.astype(jnp.int32) for p in (pad_beg, pad_mid, pad_end))
    return starts, expert, rows, tid.astype(jnp.int32), pads


def _layer(x, norm1_g, w_in, q_norm_g, k_norm_g, rpb, sgu_norm_g, w_spatial, b_spatial, attn_out_g,
           sgu_out_g, w_out, norm2_g, w_router, b_router, w_gate_up, b_gate_up, w_down, b_down):
    batch, seq, d = x.shape
    t = batch * seq
    head_dim = q_norm_g.shape[-1]
    n_heads = rpb.shape[0]
    attn_width = n_heads * head_dim
    n_groups, chunk, _ = w_spatial.shape
    sgu_width = sgu_norm_g.shape[-1]
    group_dim = sgu_width // n_groups
    win_rows_max = (rpb.shape[1] + 1) // 2
    win_cols = (rpb.shape[2] + 1) // 2
    rows = seq // GRID_W
    kh = min(win_rows_max, rows)
    n_exp = w_router.shape[-1]
    assert TOP_K <= n_exp <= LANES // 2 and seq % GRID_W == 0 and w_in.shape[1] == 3 * attn_width + 2 * sgu_width

    xf = x.reshape(t, d)
    row = lambda v: v.reshape(1, -1).astype(_F32)

    w_in_b = w_in.astype(_BF16)
    pair = lambda g: jnp.concatenate([g, g]).reshape(1, 1, LANES)
    head_gain = jnp.concatenate([pair(q_norm_g), pair(k_norm_g), jnp.ones((1, 1, LANES), _F32)], axis=0)
    qkv = _norm_proj(xf, row(norm1_g), w_in_b[:, :3 * attn_width], head_gain,
                     slab_out=True, attn_width=attn_width, head_dim=head_dim)
    ug = _norm_proj(xf, row(norm1_g), w_in_b[:, 3 * attn_width:], head_gain,
                    slab_out=False, attn_width=attn_width, head_dim=head_dim)

    bias_tab = _attention_bias_table(rpb, kh=kh, win_cols=win_cols)
    w_gu2d = w_gate_up.reshape(-1, w_gate_up.shape[-1])
    w_d2d = w_down.reshape(-1, w_down.shape[-1])
    n_steps = _attention_steps(batch, seq, n_heads, kh)
    ride = all(w.shape[0] % (16 * n_steps) == 0 for w in (w_gu2d, w_d2d))
    a_out, cast = _attention(qkv, bias_tab, (w_gu2d, w_d2d) if ride else (),
                             batch=batch, seq=seq, n_heads=n_heads, head_dim=head_dim, kh=kh)
    w_gu_b, w_d_b = cast if ride else (w_gu2d.astype(_BF16), w_d2d.astype(_BF16))
    w_gu_b = w_gu_b.reshape(w_gate_up.shape)
    w_d_b = w_d_b.reshape(w_down.shape)

    b_full = jnp.repeat(b_spatial.T, group_dim, axis=1)
    s_out = _sgu(ug, row(sgu_norm_g), w_spatial.astype(_BF16), b_full, row(sgu_out_g),
                 width=sgu_width, chunk=chunk, group_dim=group_dim)

    half = LANES // 2
    w_r_hi = w_router.astype(_BF16)
    w_r_lo = (w_router - w_r_hi.astype(_F32)).astype(_BF16)
    blank = jnp.zeros((d, LANES), _BF16)
    w_router_pad = jnp.concatenate([blank.at[:, :n_exp].set(w_r_hi).at[:, half:half + n_exp].set(w_r_lo),
                                    blank.at[:, :n_exp].set(w_r_hi)], axis=0)
    b_router_pad = jnp.full((1, LANES), NEG_BIG, _F32).at[0, :n_exp].set(b_router)
    x2, xn, idx_pad, gates_pad = _out_router(a_out, s_out, xf, row(attn_out_g), w_out.astype(_BF16),
                                             row(norm2_g), w_router_pad, b_router_pad)

    de = w_down.shape[1]
    tm = min(1024, t)
    tn = min(512, de)
    sub = min(256, tm)
    n_tiles = (t * TOP_K) // tm + n_exp
    counts = _route_counts(idx_pad)[0, :n_exp].astype(jnp.int32)
    assert tm % ZERO_ROWS == 0
    starts, tile_expert, tile_rows, tile_blk, pads = _tile_plan(counts, tm=tm, n_tiles=n_tiles)
    starts_pad = jnp.zeros((1, LANES), _F32).at[0, :n_exp].set(starts.astype(_F32))
    dst_pad = _route_slots(idx_pad, starts_pad)
    dst_flat = dst_pad[:, :TOP_K].reshape(-1)

    xs = _scatter_rows(*pads, dst_flat, xn, n_tiles * tm)
    ys = _expert_ffn(tile_expert, tile_rows, tile_blk, xs, w_gu_b, b_gate_up, w_d_b, b_down, tm=tm, tn=tn, sub=sub)
    out = _combine(dst_flat, gates_pad, x2, ys)
    return out.reshape(batch, seq, d)


def kernel(x, norm1_g, w_in, q_norm_g, k_norm_g, rpb, sgu_norm_g, w_spatial, b_spatial, attn_out_g,
           sgu_out_g, w_out, norm2_g, w_router, b_router, w_gate_up, b_gate_up, w_down, b_down):
    depth = norm1_g.shape[0]
    for l in range(depth):
        x = _layer(x, norm1_g[l], w_in[l], q_norm_g[l], k_norm_g[l], rpb[l], sgu_norm_g[l], w_spatial[l],
                   b_spatial[l], attn_out_g[l], sgu_out_g[l], w_out[l], norm2_g[l], w_router[l], b_router[l],
                   w_gate_up[l], b_gate_up[l], w_down[l], b_down[l])
    return x
```

```python
import functools

import jax
import jax.numpy as jnp
import numpy as np
from jax import lax
from jax.experimental import pallas as pl
from jax.experimental.pallas import tpu as pltpu

GRID_W = 64
TOP_K = 4
SWIGLU_LIMIT = 7.0
SWIGLU_ALPHA = 1.702
EPS = 1e-6
LANES = 128
NEG_BIG = -1e30
VMEM_LIMIT_BYTES = 56 * 1024 * 1024

_F32 = jnp.float32
_BF16 = jnp.bfloat16


def _cparams(n_axes):
    return pltpu.CompilerParams(
        dimension_semantics=("arbitrary",) * n_axes,
        vmem_limit_bytes=VMEM_LIMIT_BYTES)


def _rms(x):
    return x * lax.rsqrt(jnp.mean(x * x, axis=-1, keepdims=True) + EPS)


def _gelu(x):
    return 0.5 * x * (1.0 + lax.erf(x * 0.7071067811865476))


def _norm_proj_kernel(x_ref, g_ref, w_ref, hg_ref, o_ref, h_ref, *, slab_out, n_norm_tiles, head_dim):
    j = pl.program_id(1)
    tm = x_ref.shape[0]
    tn = w_ref.shape[1]
    left = lax.broadcasted_iota(jnp.int32, (1, LANES), 1) < head_dim
    normed = j < n_norm_tiles
    piece = min(2 * LANES, tn)

    def project(rs, h):
        if not slab_out:
            o_ref[rs, :] = jnp.dot(h, w_ref[...], preferred_element_type=_F32)
            return
        for p in range(tn // piece):
            res = jnp.dot(h, w_ref[:, p * piece:(p + 1) * piece], preferred_element_type=_F32)
            for s in range(piece // LANES):
                r = res[:, s * LANES:(s + 1) * LANES]
                ss = r * r
                ss_first = jnp.sum(jnp.where(left, ss, 0.0), axis=-1, keepdims=True)
                ss_second = jnp.sum(jnp.where(left, 0.0, ss), axis=-1, keepdims=True)
                inv = jnp.where(left, lax.rsqrt(ss_first / head_dim + EPS),
                                lax.rsqrt(ss_second / head_dim + EPS)) * hg_ref[0]
                o_ref[p * (piece // LANES) + s, rs, :] = (r * jnp.where(normed, inv, 1.0)).astype(o_ref.dtype)

    @pl.when(j == 0)
    def _():
        rc = min(256, tm)
        for c in range(tm // rc):
            rs = slice(c * rc, (c + 1) * rc)
            h = (_rms(x_ref[rs, :]) * g_ref[...]).astype(_BF16)
            h_ref[rs, :] = h
            project(rs, h)

    @pl.when(j > 0)
    def _():
        project(slice(0, tm), h_ref[...])


def _norm_proj(x, g, w_bf16, head_gain, *, slab_out, attn_width, head_dim):
    t, d = x.shape
    n = w_bf16.shape[1]
    tm = min(1024, t)
    tn = min(1024, attn_width if slab_out else n)
    assert t % tm == 0 and n % tn == 0 and tn % LANES == 0
    if slab_out:
        assert attn_width % tn == 0 and 2 * head_dim == LANES
        tiles_per_part = attn_width // tn
        out_shape = jax.ShapeDtypeStruct((n // LANES, t, LANES), _BF16)
        out_spec = pl.BlockSpec((tn // LANES, tm, LANES), lambda i, j: (j, i, 0))
        hg_spec = pl.BlockSpec((1, 1, LANES), lambda i, j: (jnp.minimum(j // tiles_per_part, 2), 0, 0))
        n_norm_tiles = 2 * tiles_per_part
    else:
        out_shape = jax.ShapeDtypeStruct((t, n), _F32)
        out_spec = pl.BlockSpec((tm, tn), lambda i, j: (i, j))
        hg_spec = pl.BlockSpec((1, 1, LANES), lambda i, j: (0, 0, 0))
        n_norm_tiles = 0
    kern = functools.partial(_norm_proj_kernel, slab_out=slab_out, n_norm_tiles=n_norm_tiles, head_dim=head_dim)
    return pl.pallas_call(
        kern,
        grid=(t // tm, n // tn),
        in_specs=[
            pl.BlockSpec((tm, d), lambda i, j: (i, 0)),
            pl.BlockSpec((1, d), lambda i, j: (0, 0)),
            pl.BlockSpec((d, tn), lambda i, j: (0, j)),
            hg_spec,
        ],
        out_specs=out_spec,
        out_shape=out_shape,
        scratch_shapes=[pltpu.VMEM((tm, d), _BF16)],
        compiler_params=_cparams(2),
        name="norm_proj_qkv" if slab_out else "norm_proj_ug",
    )(x, g, w_bf16, head_gain)


def _attn_kernel(q_ref, kp_ref, kc_ref, kn_ref, vp_ref, vc_ref, vn_ref, bias_ref, *rest,
                 rows, rblk, kh, head_dim, scale, n_cast):
    cast_in, (o_ref,), cast_out, (kwin, vwin) = (rest[:n_cast], rest[n_cast:n_cast + 1],
                                                rest[n_cast + 1:2 * n_cast + 1], rest[2 * n_cast + 1:])
    for src, dst in zip(cast_in, cast_out):
        dst[...] = src[...].astype(dst.dtype)
    rb = pl.program_id(2)
    blk = rblk * GRID_W
    for w, (kr, vr) in enumerate(((kp_ref, vp_ref), (kc_ref, vc_ref), (kn_ref, vn_ref))):
        kwin[w * blk:(w + 1) * blk, :] = kr[...]
        vwin[w * blk:(w + 1) * blk, :] = vr[...]
    lane = lax.broadcasted_iota(jnp.int32, (1, LANES), 1)
    left = lane < head_dim
    nk = (kh + 2) * GRID_W
    scores, offsets = [], []
    for pair in range(rblk // 2):
        r = rb * rblk + 2 * pair
        rs_a = jnp.clip(r - kh // 2, 0, rows - kh)
        rs_b = jnp.clip(r + 1 - kh // 2, 0, rows - kh)
        kind = _pair_kind_index(r - rs_a, rs_b - rs_a, kh)
        off = pl.multiple_of((rs_a - (rb * rblk - rblk)) * GRID_W, GRID_W)
        q2 = q_ref[2 * pair * GRID_W:(2 * pair + 2) * GRID_W, :] * scale
        qa, qb = q2[:GRID_W], q2[GRID_W:]
        qm = jnp.concatenate([jnp.where(left, qa, 0.0), jnp.where(left, qb, 0.0),
                              jnp.where(left, 0.0, qa), jnp.where(left, 0.0, qb)], axis=0).astype(_BF16)
        s = lax.dot_general(qm, kwin[pl.ds(off, nk), :], (((1,), (1,)), ((), ())), preferred_element_type=_F32)
        scores.append(s + bias_ref[kind])
        offsets.append(off)
    for pair in range(rblk // 2):
        s = scores[pair]
        p = jnp.exp(s - jnp.max(s, axis=-1, keepdims=True))
        denom = jnp.sum(p, axis=-1, keepdims=True)
        pv = jnp.dot(p.astype(_BF16), vwin[pl.ds(offsets[pair], nk), :], preferred_element_type=_F32) / denom
        o_ref[2 * pair * GRID_W:(2 * pair + 2) * GRID_W, :] = jnp.where(
            left, pv[0:2 * GRID_W], pv[2 * GRID_W:4 * GRID_W])


def _attention_steps(batch, seq, n_heads, kh):
    return batch * (n_heads // 2) * (seq // GRID_W // kh)


def _attention(qkv, bias_tab, cast_srcs, *, batch, seq, n_heads, head_dim, kh):
    n_hp = n_heads // 2
    rows = seq // GRID_W
    rblk = kh
    assert rows % rblk == 0 and rows >= kh and rblk >= 4
    nrb = rows // rblk
    blk = rblk * GRID_W
    t = batch * seq
    n_steps = batch * n_hp * nrb
    cast_specs = []
    for src in cast_srcs:
        assert src.shape[0] % n_steps == 0 and (src.shape[0] // n_steps) % 16 == 0
        cast_specs.append(pl.BlockSpec((src.shape[0] // n_steps, src.shape[1]),
                                       lambda b, hp, rb: ((b * n_hp + hp) * nrb + rb, 0)))

    def qmap(b, hp, rb):
        return (hp, b * nrb + rb, 0)

    def kvmap(part, delta):
        def f(b, hp, rb):
            return (part * n_hp + hp, b * nrb + jnp.clip(rb + delta, 0, nrb - 1), 0)
        return f

    slab = lambda m: pl.BlockSpec((None, blk, LANES), m)
    kern = functools.partial(_attn_kernel, rows=rows, rblk=rblk, kh=kh, head_dim=head_dim,
                             scale=float(head_dim) ** -0.5, n_cast=len(cast_srcs))
    outs = pl.pallas_call(
        kern,
        grid=(batch, n_hp, nrb),
        in_specs=[
            slab(qmap),
            slab(kvmap(1, -1)), slab(kvmap(1, 0)), slab(kvmap(1, 1)),
            slab(kvmap(2, -1)), slab(kvmap(2, 0)), slab(kvmap(2, 1)),
            pl.BlockSpec((len(_pair_kinds(kh)), None, 4 * GRID_W, (kh + 2) * GRID_W),
                         lambda b, hp, rb: (0, hp, 0, 0)),
        ] + cast_specs,
        out_specs=[pl.BlockSpec((blk, LANES), lambda b, hp, rb: (b * nrb + rb, hp))] + cast_specs,
        out_shape=[jax.ShapeDtypeStruct((t, n_hp * LANES), _F32)]
                  + [jax.ShapeDtypeStruct(src.shape, _BF16) for src in cast_srcs],
        scratch_shapes=[pltpu.VMEM((3 * blk, LANES), _BF16), pltpu.VMEM((3 * blk, LANES), _BF16)],
        compiler_params=_cparams(3),
        name="nbr_attention",
    )(qkv, qkv, qkv, qkv, qkv, qkv, qkv, bias_tab, *cast_srcs)
    return outs[0], outs[1:]


def _pair_kinds(kh):
    assert kh % 4 == 0
    return sorted([(s, 0) for s in range(0, kh, 2)] + [(kh // 2, 1)], key=lambda sd: sd[0] + sd[1])


def _pair_kind_index(shift_a, delta, kh):
    code = shift_a + delta
    return code // 2 + (code > kh // 2).astype(jnp.int32)


def _attention_bias_table(rpb, *, kh, win_cols):
    n_heads, n_rpb_rows, n_rpb_cols = rpb.shape
    win_rows_max = (n_rpb_rows + 1) // 2
    nkr = kh + 2
    qc = np.arange(GRID_W)[:, None]
    kc = np.arange(GRID_W)[None, :]
    col_start = np.clip(qc - win_cols // 2, 0, GRID_W - win_cols)
    col_valid = (kc >= col_start) & (kc < col_start + win_cols)
    col_sel = np.zeros((GRID_W, GRID_W, n_rpb_cols), np.float32)
    qi, ki = np.nonzero(col_valid)
    col_sel[qi, ki, ki - qi + win_cols - 1] = 1.0
    kinds = _pair_kinds(kh)
    row_sel = np.zeros((len(kinds), 2, nkr, n_rpb_rows), np.float32)
    row_valid = np.zeros((len(kinds), 2, nkr), bool)
    for kind, (shift_a, delta) in enumerate(kinds):
        for member, (shift, dlt) in enumerate(((shift_a, 0), (shift_a + 1 - delta, delta))):
            for jj in range(nkr):
                if 0 <= jj - dlt < kh:
                    row_sel[kind, member, jj, jj - dlt - shift + win_rows_max - 1] = 1.0
                    row_valid[kind, member, jj] = True
    exact = lax.Precision.HIGHEST
    cols = jnp.einsum("hrc,qkc->rhqk", rpb.astype(_F32), col_sel, precision=exact)
    tab = jnp.einsum("ymjr,rhqk->yhmqjk", row_sel, cols, precision=exact)
    keep = row_valid[:, None, :, None, :, None] & col_valid[None, None, None, :, None, :]
    tab = jnp.where(keep, tab, NEG_BIG)
    return tab.reshape(len(kinds), n_heads // 2, 4 * GRID_W, nkr * GRID_W)


def _sgu_kernel(u_ref, g_ref, ng_ref, w_ref, b_ref, og_ref, o_ref, *, chunk, group_dim):
    tm, width = u_ref.shape
    gg = _gelu(g_ref[...])
    gn = (_rms(gg) * ng_ref[...]).astype(_BF16)
    lane = lax.broadcasted_iota(jnp.int32, (1, LANES), 1)
    left = lane < group_dim
    for c in range(tm // chunk):
        rs = slice(c * chunk, (c + 1) * chunk)
        for gp in range(width // LANES):
            cs = slice(gp * LANES, (gp + 1) * LANES)
            xg = gn[rs, cs]
            ma = jnp.dot(w_ref[2 * gp], xg, preferred_element_type=_F32)
            mb = jnp.dot(w_ref[2 * gp + 1], xg, preferred_element_type=_F32)
            mixed = jnp.where(left, ma, mb) + b_ref[:, cs]
            s = _gelu(u_ref[rs, cs]) * mixed
            o_ref[rs, cs] = s
    s_all = o_ref[...]
    o_ref[...] = _rms(s_all) * og_ref[...]


def _sgu(ug, norm_g, w_sp_bf16, b_full, out_g, *, width, chunk, group_dim):
    t = ug.shape[0]
    tm = min(512, t)
    assert t % tm == 0 and tm % chunk == 0 and width % LANES == 0 and 2 * group_dim == LANES
    n_groups = w_sp_bf16.shape[0]
    kern = functools.partial(_sgu_kernel, chunk=chunk, group_dim=group_dim)
    return pl.pallas_call(
        kern,
        grid=(t // tm,),
        in_specs=[
            pl.BlockSpec((tm, width), lambda i: (i, 0)),
            pl.BlockSpec((tm, width), lambda i: (i, 1)),
            pl.BlockSpec((1, width), lambda i: (0, 0)),
            pl.BlockSpec((n_groups, chunk, chunk), lambda i: (0, 0, 0)),
            pl.BlockSpec((chunk, width), lambda i: (0, 0)),
            pl.BlockSpec((1, width), lambda i: (0, 0)),
        ],
        out_specs=pl.BlockSpec((tm, width), lambda i: (i, 0)),
        out_shape=jax.ShapeDtypeStruct((t, width), _F32),
        compiler_params=_cparams(1),
        name="spatial_gating",
    )(ug, ug, norm_g, w_sp_bf16, b_full, out_g)


ROUTER_ROW_CHUNK = 128


def _out_router_kernel(a_ref, s_ref, x_ref, ag_ref, w_ref, n2g_ref, wr_ref, br_ref,
                       x2_ref, xn_ref, idx_ref, gate_ref):
    chunks = [slice(c * ROUTER_ROW_CHUNK, (c + 1) * ROUTER_ROW_CHUNK)
              for c in range(x_ref.shape[0] // ROUTER_ROW_CHUNK)]
    for rs in chunks:
        an = _rms(a_ref[rs, :]) * ag_ref[...]
        mix = jnp.concatenate([an, s_ref[rs, :]], axis=-1).astype(_BF16)
        x2_ref[rs, :] = x_ref[rs, :] + jnp.dot(mix, w_ref[...], preferred_element_type=_F32)
    all_logits = []
    for rs in chunks:
        xn = _rms(x2_ref[rs, :]) * n2g_ref[...]
        xn_ref[rs, :] = xn
        xh = xn.astype(_BF16)
        xl = (xn - xh.astype(_F32)).astype(_BF16)
        parts = jnp.dot(jnp.concatenate([xh, xl], axis=-1), wr_ref[...], preferred_element_type=_F32)
        all_logits.append(parts + pltpu.roll(parts, LANES // 2, axis=1) + br_ref[...])
    for rs, logits in zip(chunks, all_logits):
        rc = logits.shape[0]
        lane = lax.broadcasted_iota(jnp.int32, (rc, LANES), 1)
        lane_f = lane.astype(_F32)
        idx_acc = jnp.zeros((rc, LANES), _F32)
        vals = []
        cur = logits
        for k in range(TOP_K):
            m = jnp.max(cur, axis=-1, keepdims=True)
            ik = jnp.min(jnp.where(cur == m, lane_f, float(LANES)), axis=-1, keepdims=True)
            vals.append(m)
            idx_acc = jnp.where(lane == k, ik, idx_acc)
            cur = jnp.where(lane_f == ik, -jnp.inf, cur)
        exps = [jnp.exp(v - vals[0]) for v in vals]
        denom = exps[0]
        for e in exps[1:]:
            denom = denom + e
        gate_acc = jnp.zeros((rc, LANES), _F32)
        for k in range(TOP_K):
            gate_acc = jnp.where(lane == k, exps[k] / denom, gate_acc)
        idx_ref[rs, :] = idx_acc.astype(jnp.int32)
        gate_ref[rs, :] = gate_acc


def _out_router(a_out, s_out, x, attn_out_g, w_out_bf16, norm2_g, w_router_pad, b_router_pad):
    t, d = x.shape
    wa = a_out.shape[1]
    ws = s_out.shape[1]
    tm = min(512, t)
    assert t % tm == 0
    row = lambda n: pl.BlockSpec((tm, n), lambda i: (i, 0))
    full = lambda a, b: pl.BlockSpec((a, b), lambda i: (0, 0))
    return pl.pallas_call(
        _out_router_kernel,
        grid=(t // tm,),
        in_specs=[row(wa), row(ws), row(d), full(1, wa), full(wa + ws, d), full(1, d),
                  full(2 * d, LANES), full(1, LANES)],
        out_specs=[row(d), row(d), row(LANES), row(LANES)],
        out_shape=[jax.ShapeDtypeStruct((t, d), _F32), jax.ShapeDtypeStruct((t, d), _F32),
                   jax.ShapeDtypeStruct((t, LANES), jnp.int32), jax.ShapeDtypeStruct((t, LANES), _F32)],
        compiler_params=_cparams(1),
        name="out_proj_router",
    )(a_out, s_out, x, attn_out_g, w_out_bf16, norm2_g, w_router_pad, b_router_pad)


def _membership(idx):
    lane = lax.broadcasted_iota(jnp.int32, idx.shape, 1)
    return [lane == idx[:, k:k + 1] for k in range(TOP_K)]


def _count_kernel(idx_ref, cnt_ref):
    @pl.when(pl.program_id(0) == 0)
    def _():
        cnt_ref[...] = jnp.zeros_like(cnt_ref)

    member = sum(oh.astype(_F32) for oh in _membership(idx_ref[...]))
    cnt_ref[...] += jnp.sum(member, axis=0, keepdims=True)


def _slot_kernel(idx_ref, start_ref, dst_ref, carry_ref):
    @pl.when(pl.program_id(0) == 0)
    def _():
        carry_ref[...] = jnp.zeros_like(carry_ref)

    onehots = _membership(idx_ref[...])
    member = sum(oh.astype(_F32) for oh in onehots)
    tb = member.shape[0]
    earlier = (lax.broadcasted_iota(jnp.int32, (tb, tb), 0) > lax.broadcasted_iota(jnp.int32, (tb, tb), 1))
    before = jnp.dot(earlier.astype(_BF16), member.astype(_BF16), preferred_element_type=_F32)
    slot_e = start_ref[...] + carry_ref[...] + before
    lane = lax.broadcasted_iota(jnp.int32, member.shape, 1)
    dst = jnp.zeros(member.shape, _F32)
    for k in range(TOP_K):
        dk = jnp.sum(jnp.where(onehots[k], slot_e, 0.0), axis=-1, keepdims=True)
        dst = jnp.where(lane == k, dk, dst)
    dst_ref[...] = dst.astype(jnp.int32)
    carry_ref[...] += jnp.sum(member, axis=0, keepdims=True)


def _route_counts(idx_pad):
    t = idx_pad.shape[0]
    tb = min(512, t)
    return pl.pallas_call(
        _count_kernel,
        grid=(t // tb,),
        in_specs=[pl.BlockSpec((tb, LANES), lambda i: (i, 0))],
        out_specs=pl.BlockSpec((1, LANES), lambda i: (0, 0)),
        out_shape=jax.ShapeDtypeStruct((1, LANES), _F32),
        compiler_params=_cparams(1),
        name="route_counts",
    )(idx_pad)


def _route_slots(idx_pad, starts_f32):
    t = idx_pad.shape[0]
    tb = min(512, t)
    return pl.pallas_call(
        _slot_kernel,
        grid=(t // tb,),
        in_specs=[pl.BlockSpec((tb, LANES), lambda i: (i, 0)), pl.BlockSpec((1, LANES), lambda i: (0, 0))],
        out_specs=pl.BlockSpec((tb, LANES), lambda i: (i, 0)),
        out_shape=jax.ShapeDtypeStruct((t, LANES), jnp.int32),
        scratch_shapes=[pltpu.VMEM((1, LANES), _F32)],
        compiler_params=_cparams(1),
        name="route_slots",
    )(idx_pad, starts_f32)


ROUTE_TB = 512


ZERO_ROWS = 256


def _scatter_rows_kernel(pad_beg_ref, pad_mid_ref, pad_end_ref, dst_ref, xn_ref, xs_ref, zero_ref, sem, zsem):
    tb = xn_ref.shape[0]
    first = pl.program_id(0) == 0
    n_regions = pad_beg_ref.shape[0]

    def slot(ref, r, n=1):
        return ref.at[pl.ds(r, n)]

    def issue(t, c):
        for k in range(TOP_K):
            d = dst_ref[t * TOP_K + k]
            pltpu.make_async_copy(slot(xn_ref, t), slot(xs_ref, d), sem).start(priority=k % 2)
        return c

    def drain():
        for _ in range(TOP_K):
            pltpu.make_async_copy(xn_ref, slot(xs_ref, 0, tb), sem).wait()

    def clear_padding(start):
        def region(e, c):
            beg, mid, end = pad_beg_ref[e], pad_mid_ref[e], pad_end_ref[e]

            def one_row(r, c2):
                cp = pltpu.make_async_copy(slot(zero_ref, 0), slot(xs_ref, r), zsem)
                cp.start() if start else cp.wait()
                return c2

            def one_block(b, c2):
                r0 = pl.multiple_of(mid + b * ZERO_ROWS, ZERO_ROWS)
                cp = pltpu.make_async_copy(zero_ref, slot(xs_ref, r0, ZERO_ROWS), zsem)
                cp.start() if start else cp.wait()
                return c2

            lax.fori_loop(beg, mid, one_row, 0)
            lax.fori_loop(0, (end - mid) // ZERO_ROWS, one_block, 0)
            return c

        lax.fori_loop(0, n_regions, region, 0)

    @pl.when(first)
    def _():
        zero_ref[...] = jnp.zeros_like(zero_ref)
        clear_padding(True)

    lax.fori_loop(0, tb, issue, 0, unroll=4)
    drain()

    @pl.when(first)
    def _():
        clear_padding(False)


def _scatter_rows(pad_beg, pad_mid, pad_end, dst_flat, xn, n_slots):
    t, d = xn.shape
    tb = min(ROUTE_TB, t)
    grid_spec = pltpu.PrefetchScalarGridSpec(
        num_scalar_prefetch=3,
        grid=(t // tb,),
        in_specs=[
            pl.BlockSpec((tb * TOP_K,), lambda i, *_: (i,), memory_space=pltpu.SMEM),
            pl.BlockSpec((tb, d), lambda i, *_: (i, 0)),
        ],
        out_specs=pl.BlockSpec(memory_space=pl.ANY),
        scratch_shapes=[pltpu.VMEM((ZERO_ROWS, d), xn.dtype), pltpu.SemaphoreType.DMA(()),
                        pltpu.SemaphoreType.DMA(())],
    )
    return pl.pallas_call(
        _scatter_rows_kernel,
        grid_spec=grid_spec,
        out_shape=jax.ShapeDtypeStruct((n_slots, d), xn.dtype),
        compiler_params=_cparams(1),
        name="scatter_rows",
    )(pad_beg, pad_mid, pad_end, dst_flat, xn)


def _combine_kernel(dst_ref, gate_ref, x2_ref, ys_ref, o_ref, rows_ref, sems):
    tb = x2_ref.shape[0]

    def issue(t, c, sem):
        for k in range(TOP_K):
            src = dst_ref[t * TOP_K + k]
            pltpu.make_async_copy(ys_ref.at[pl.ds(src, 1)], rows_ref.at[k, pl.ds(t, 1)], sem).start(priority=k % 2)
        return c

    half = tb // 2
    for part in range(2):
        lax.fori_loop(part * half, (part + 1) * half, functools.partial(issue, sem=sems.at[part]), 0, unroll=4)
    for part in range(2):
        t0 = part * half
        for k in range(TOP_K):
            pltpu.make_async_copy(ys_ref.at[pl.ds(0, half)], rows_ref.at[k, pl.ds(t0, half)], sems.at[part]).wait()
        acc = x2_ref[t0:t0 + half, :]
        gates = gate_ref[t0:t0 + half, :]
        for k in range(TOP_K):
            acc = acc + gates[:, k:k + 1] * rows_ref[k, t0:t0 + half, :]
        o_ref[t0:t0 + half, :] = acc


def _combine(dst_flat, gates_pad, x2, ys):
    t, d = x2.shape
    tb = min(ROUTE_TB, t)
    return pl.pallas_call(
        _combine_kernel,
        grid=(t // tb,),
        in_specs=[
            pl.BlockSpec((tb * TOP_K,), lambda i: (i,), memory_space=pltpu.SMEM),
            pl.BlockSpec((tb, LANES), lambda i: (i, 0)),
            pl.BlockSpec((tb, d), lambda i: (i, 0)),
            pl.BlockSpec(memory_space=pl.ANY),
        ],
        out_specs=pl.BlockSpec((tb, d), lambda i: (i, 0)),
        out_shape=jax.ShapeDtypeStruct((t, d), _F32),
        scratch_shapes=[pltpu.VMEM((TOP_K, tb, d), _F32), pltpu.SemaphoreType.DMA((2,))],
        compiler_params=_cparams(1),
        name="combine_rows",
    )(dst_flat, gates_pad, x2, ys)


def _ffn_kernel(te_ref, tr_ref, tblk_ref, x_ref, wg_ref, wl_ref, bg_ref, bl_ref, wd_ref, bd_ref,
                o_ref, xb_ref, *, sub):
    del te_ref, tblk_ref
    i = pl.program_id(0)
    j = pl.program_id(1)
    n_rows = tr_ref[i]
    tm, d = o_ref.shape

    @pl.when(jnp.logical_and(j == 0, jnp.logical_and(n_rows > 0, n_rows < tm)))
    def _():
        o_ref[...] = jnp.broadcast_to(bd_ref[...], (tm, d))

    @pl.when(jnp.logical_and(j == 0, n_rows == 0))
    def _():
        o_ref[...] = jnp.zeros((tm, d), _F32)

    def rows_block(r0, size, first):
        rs = pl.ds(r0, size)
        if first is None:
            @pl.when(j == 0)
            def _():
                xb_ref[rs, :] = x_ref[rs, :].astype(_BF16)
        elif first:
            xb_ref[rs, :] = x_ref[rs, :].astype(_BF16)

        xb = xb_ref[rs, :]
        hg = jnp.dot(xb, wg_ref[...], preferred_element_type=_F32) + bg_ref[...]
        hl = jnp.dot(xb, wl_ref[...], preferred_element_type=_F32) + bl_ref[...]
        gate = jnp.minimum(hg, SWIGLU_LIMIT)
        lin = jnp.clip(hl, -SWIGLU_LIMIT, SWIGLU_LIMIT)
        act = gate * (1.0 / (1.0 + jnp.exp(-SWIGLU_ALPHA * gate))) * (lin + 1.0)
        part = jnp.dot(act.astype(_BF16), wd_ref[...], preferred_element_type=_F32)
        if first:
            o_ref[rs, :] = bd_ref[...] + part
        else:
            o_ref[rs, :] += part

    @pl.when(jnp.logical_and(n_rows == tm, j == 0))
    def _():
        rows_block(0, tm, True)

    @pl.when(jnp.logical_and(n_rows == tm, j > 0))
    def _():
        rows_block(0, tm, False)

    @pl.when(n_rows < tm)
    def _():
        def sub_block(s, c):
            rows_block(pl.multiple_of(s * sub, sub), sub, None)
            return c

        lax.fori_loop(0, (n_rows + sub - 1) // sub, sub_block, 0)


def _expert_ffn(tile_expert, tile_rows, tile_blk, xs, w_gu_bf16, b_gu, w_d_bf16, b_d, *, tm, tn, sub):
    n_slots, d = xs.shape
    n_exp, _, de2 = w_gu_bf16.shape
    de = de2 // 2
    nj = de // tn
    n_tiles = n_slots // tm
    assert de % tn == 0 and n_slots % tm == 0 and tm % sub == 0
    grid_spec = pltpu.PrefetchScalarGridSpec(
        num_scalar_prefetch=3,
        grid=(n_tiles, nj),
        in_specs=[
            pl.BlockSpec((tm, d), lambda i, j, te, tr, tb: (tb[i], 0)),
            pl.BlockSpec((None, d, tn), lambda i, j, te, tr, tb: (te[i], 0, j)),
            pl.BlockSpec((None, d, tn), lambda i, j, te, tr, tb: (te[i], 0, nj + j)),
            pl.BlockSpec((None, 1, tn), lambda i, j, te, tr, tb: (te[i], 0, j)),
            pl.BlockSpec((None, 1, tn), lambda i, j, te, tr, tb: (te[i], 0, nj + j)),
            pl.BlockSpec((None, tn, d), lambda i, j, te, tr, tb: (te[i], j, 0)),
            pl.BlockSpec((None, 1, d), lambda i, j, te, tr, tb: (te[i], 0, 0)),
        ],
        out_specs=pl.BlockSpec((tm, d), lambda i, j, te, tr, tb: (i, 0)),
        scratch_shapes=[pltpu.VMEM((tm, d), _BF16)],
    )
    return pl.pallas_call(
        functools.partial(_ffn_kernel, sub=sub),
        grid_spec=grid_spec,
        out_shape=jax.ShapeDtypeStruct((n_slots, d), _F32),
        compiler_params=_cparams(2),
        name="expert_ffn",
    )(tile_expert, tile_rows, tile_blk, xs, w_gu_bf16, w_gu_bf16,
      b_gu.reshape(n_exp, 1, de2), b_gu.reshape(n_exp, 1, de2), w_d_bf16, b_d.reshape(n_exp, 1, d))


def _tile_plan(counts, *, tm, n_tiles):
    n_exp = counts.shape[0]
    tiles_e = (counts + tm - 1) // tm
    tile_end = jnp.cumsum(tiles_e)
    tile_beg = tile_end - tiles_e
    starts = tile_beg * tm
    n_active = tile_end[-1]
    tid = jnp.minimum(jnp.arange(n_tiles, dtype=jnp.int32), n_active - 1)
    expert = jnp.minimum(jnp.sum(tile_end[None, :] <= tid[:, None], axis=1), n_exp - 1).astype(jnp.int32)
    rows = jnp.clip(counts[expert] - (tid - tile_beg[expert]) * tm, 0, tm)
    rows = jnp.where(jnp.arange(n_tiles) < n_active, rows, 0).astype(jnp.int32)
    pad_beg = jnp.concatenate([starts + counts, (n_active * tm)[None]])
    pad_end = jnp.concatenate([tile_end * tm, jnp.full((1,), n_tiles * tm, jnp.int32)])
    pad_mid = jnp.minimum((pad_beg + ZERO_ROWS - 1) // ZERO_ROWS * ZERO_ROWS, pad_end)
    pads = tuple(p.astype(jnp.int32) for p in (pad_beg, pad_mid, pad_end))
    return starts, expert, rows, tid.astype(jnp.int32), pads


def _layer(x, norm1_g, w_in, q_norm_g, k_norm_g, rpb, sgu_norm_g, w_spatial, b_spatial, attn_out_g,
           sgu_out_g, w_out, norm2_g, w_router, b_router, w_gate_up, b_gate_up, w_down, b_down):
    batch, seq, d = x.shape
    t = batch * seq
    head_dim = q_norm_g.shape[-1]
    n_heads = rpb.shape[0]
    attn_width = n_heads * head_dim
    n_groups, chunk, _ = w_spatial.shape
    sgu_width = sgu_norm_g.shape[-1]
    group_dim = sgu_width // n_groups
    win_rows_max = (rpb.shape[1] + 1) // 2
    win_cols = (rpb.shape[2] + 1) // 2
    rows = seq // GRID_W
    kh = min(win_rows_max, rows)
    n_exp = w_router.shape[-1]
    assert TOP_K <= n_exp <= LANES // 2 and seq % GRID_W == 0 and w_in.shape[1] == 3 * attn_width + 2 * sgu_width

    xf = x.reshape(t, d)
    row = lambda v: v.reshape(1, -1).astype(_F32)

    w_in_b = w_in.astype(_BF16)
    pair = lambda g: jnp.concatenate([g, g]).reshape(1, 1, LANES)
    head_gain = jnp.concatenate([pair(q_norm_g), pair(k_norm_g), jnp.ones((1, 1, LANES), _F32)], axis=0)
    qkv = _norm_proj(xf, row(norm1_g), w_in_b[:, :3 * attn_width], head_gain,
                     slab_out=True, attn_width=attn_width, head_dim=head_dim)
    ug = _norm_proj(xf, row(norm1_g), w_in_b[:, 3 * attn_width:], head_gain,
                    slab_out=False, attn_width=attn_width, head_dim=head_dim)

    bias_tab = _attention_bias_table(rpb, kh=kh, win_cols=win_cols)
    w_gu2d = w_gate_up.reshape(-1, w_gate_up.shape[-1])
    w_d2d = w_down.reshape(-1, w_down.shape[-1])
    n_steps = _attention_steps(batch, seq, n_heads, kh)
    ride = all(w.shape[0] % (16 * n_steps) == 0 for w in (w_gu2d, w_d2d))
    a_out, cast = _attention(qkv, bias_tab, (w_gu2d, w_d2d) if ride else (),
                             batch=batch, seq=seq, n_heads=n_heads, head_dim=head_dim, kh=kh)
    w_gu_b, w_d_b = cast if ride else (w_gu2d.astype(_BF16), w_d2d.astype(_BF16))
    w_gu_b = w_gu_b.reshape(w_gate_up.shape)
    w_d_b = w_d_b.reshape(w_down.shape)

    b_full = jnp.repeat(b_spatial.T, group_dim, axis=1)
    s_out = _sgu(ug, row(sgu_norm_g), w_spatial.astype(_BF16), b_full, row(sgu_out_g),
                 width=sgu_width, chunk=chunk, group_dim=group_dim)

    half = LANES // 2
    w_r_hi = w_router.astype(_BF16)
    w_r_lo = (w_router - w_r_hi.astype(_F32)).astype(_BF16)
    blank = jnp.zeros((d, LANES), _BF16)
    w_router_pad = jnp.concatenate([blank.at[:, :n_exp].set(w_r_hi).at[:, half:half + n_exp].set(w_r_lo),
                                    blank.at[:, :n_exp].set(w_r_hi)], axis=0)
    b_router_pad = jnp.full((1, LANES), NEG_BIG, _F32).at[0, :n_exp].set(b_router)
    x2, xn, idx_pad, gates_pad = _out_router(a_out, s_out, xf, row(attn_out_g), w_out.astype(_BF16),
                                             row(norm2_g), w_router_pad, b_router_pad)

    de = w_down.shape[1]
    tm = min(1024, t)
    tn = min(512, de)
    sub = min(256, tm)
    n_tiles = (t * TOP_K) // tm + n_exp
    counts = _route_counts(idx_pad)[0, :n_exp].astype(jnp.int32)
    assert tm % ZERO_ROWS == 0
    starts, tile_expert, tile_rows, tile_blk, pads = _tile_plan(counts, tm=tm, n_tiles=n_tiles)
    starts_pad = jnp.zeros((1, LANES), _F32).at[0, :n_exp].set(starts.astype(_F32))
    dst_pad = _route_slots(idx_pad, starts_pad)
    dst_flat = dst_pad[:, :TOP_K].reshape(-1)

    xs = _scatter_rows(*pads, dst_flat, xn, n_tiles * tm)
    ys = _expert_ffn(tile_expert, tile_rows, tile_blk, xs, w_gu_b, b_gate_up, w_d_b, b_down, tm=tm, tn=tn, sub=sub)
    out = _combine(dst_flat, gates_pad, x2, ys)
    return out.reshape(batch, seq, d)


def kernel(x, norm1_g, w_in, q_norm_g, k_norm_g, rpb, sgu_norm_g, w_spatial, b_spatial, attn_out_g,
           sgu_out_g, w_out, norm2_g, w_router, b_router, w_gate_up, b_gate_up, w_down, b_down):
    depth = norm1_g.shape[0]
    for l in range(depth):
        x = _layer(x, norm1_g[l], w_in[l], q_norm_g[l], k_norm_g[l], rpb[l], sgu_norm_g[l], w_spatial[l],
                   b_spatial[l], attn_out_g[l], sgu_out_g[l], w_out[l], norm2_g[l], w_router[l], b_router[l],
                   w_gate_up[l], b_gate_up[l], w_down[l], b_down[l])
    return x
```
